```python
import jax, jax.numpy as jnp
from jax import lax
import numpy as np

D_MODEL = 2048
BATCH = 4
SEQ = 4096
DEPTH = 2

CTX_LEN = 256
GRID_W = 64
HEAD_DIM = 128
MIX_GROUP = D_MODEL // 4
POOL_WINDOWS = (2, 4, 8, 16)
POOL_GROUPS = len(POOL_WINDOWS)
POOL_DIM = MIX_GROUP
POOL_GW = POOL_DIM // POOL_GROUPS
GQA_Q_HEADS = MIX_GROUP // HEAD_DIM
GQA_KV_HEADS = GQA_Q_HEADS // 2
GQA_GROUP = GQA_Q_HEADS // GQA_KV_HEADS
GQA_Q_DIM = GQA_Q_HEADS * HEAD_DIM
GQA_KV_DIM = GQA_KV_HEADS * HEAD_DIM
ROPE_THETA = 10000.0
FNET_GROUPS = 4
FNET_DIM = MIX_GROUP
FNET_GW = FNET_DIM // FNET_GROUPS
NA_HEADS = MIX_GROUP // HEAD_DIM
NA_DIM = NA_HEADS * HEAD_DIM
NA_WIN_R = 8
NA_WIN_C = 16
IN_SPLITS = (POOL_DIM, GQA_Q_DIM, GQA_KV_DIM, GQA_KV_DIM, FNET_DIM, NA_DIM, NA_DIM, NA_DIM)
IN_DIM = sum(IN_SPLITS)
MIX_DIM = POOL_DIM + GQA_Q_DIM + FNET_DIM + NA_DIM
D_FF = ((8 * D_MODEL // 3 + 255) // 256) * 256
BLOCK_Q = 128
EPS = 1e-6
N_MOD = 9

kernel_name = "hybrid_parallel_groups_dit_block"


def rmsnorm(x, g):
    xf = x.astype(jnp.float32)
    y = xf * lax.rsqrt(jnp.mean(xf * xf, axis=-1, keepdims=True) + EPS)
    return (y * g.astype(jnp.float32)).astype(x.dtype)


def modulate(h, shift, scale):
    return h * (1.0 + scale) + shift


def swiglu(h, w_gate, w_up, w_down):
    return (jax.nn.silu(h @ w_gate) * (h @ w_up)) @ w_down


def split_in(z):
    points = np.cumsum(IN_SPLITS)[:-1].tolist()
    return jnp.split(z, points, axis=-1)


def axial_rope_tables(n):
    t = jnp.arange(n, dtype=jnp.int32)
    row = (t // GRID_W).astype(jnp.float32)
    col = (t % GRID_W).astype(jnp.float32)
    n_freq = HEAD_DIM // 4
    inv = 1.0 / (ROPE_THETA ** (jnp.arange(n_freq, dtype=jnp.float32) / n_freq))
    ang = jnp.concatenate([row[:, None] * inv, col[:, None] * inv], axis=-1)
    return jnp.cos(ang), jnp.sin(ang)


def apply_rope(x, cos, sin):
    xf = x.astype(jnp.float32).reshape(x.shape[:-1] + (HEAD_DIM // 2, 2))
    x1, x2 = xf[..., 0], xf[..., 1]
    cs, sn = cos[None, :, None, :], sin[None, :, None, :]
    out = jnp.stack([x1 * cs - x2 * sn, x1 * sn + x2 * cs], axis=-1)
    return out.reshape(x.shape).astype(x.dtype)


def block_attention(q, k, v):
    b, kh, g, n, dh = q.shape
    nb = n // BLOCK_Q
    qb = jnp.moveaxis(q.reshape(b, kh, g, nb, BLOCK_Q, dh), 3, 0)
    scale = dh ** -0.5

    def one(qblk):
        s = jnp.einsum("bkgqd,bkmd->bkgqm", qblk, k).astype(jnp.float32) * scale
        p = jax.nn.softmax(s, axis=-1).astype(v.dtype)
        return jnp.einsum("bkgqm,bkmd->bkgqd", p, v)

    o = lax.map(one, qb)
    return jnp.moveaxis(o, 0, 3).reshape(b, kh, g, n, dh)


def pool_mix(u, w_pool, scale):
    b, n, _ = u.shape
    uf = u.astype(jnp.float32)
    csum = jnp.concatenate([jnp.zeros((b, 1, POOL_DIM), jnp.float32), jnp.cumsum(uf, axis=1)], axis=1)
    t = jnp.arange(n, dtype=jnp.int32)
    outs = []
    for gi, w in enumerate(POOL_WINDOWS):
        lo = jnp.maximum(t - w // 2, 0)
        hi = jnp.minimum(t + w // 2 - 1, n - 1)
        cg = csum[..., gi * POOL_GW:(gi + 1) * POOL_GW]
        mean = (jnp.take(cg, hi + 1, axis=1) - jnp.take(cg, lo, axis=1)) / (hi - lo + 1).astype(jnp.float32)[None, :, None]
        diff = mean - uf[..., gi * POOL_GW:(gi + 1) * POOL_GW]
        outs.append(diff @ w_pool[gi].astype(jnp.float32))
    return (jnp.concatenate(outs, axis=-1) * scale.astype(jnp.float32)).astype(u.dtype)


def gqa_heads_q(q, g_norm):
    b, n, _ = q.shape
    return rmsnorm(q.reshape(b, n, GQA_Q_HEADS, HEAD_DIM), g_norm)


def gqa_ctx_kv(k, v, k_norm):
    b, m, _ = k.shape
    kh = rmsnorm(k.reshape(b, m, GQA_KV_HEADS, HEAD_DIM), k_norm).transpose(0, 2, 1, 3)
    vh = v.reshape(b, m, GQA_KV_HEADS, HEAD_DIM).transpose(0, 2, 1, 3)
    return kh, vh


def gqa_group_q(qh):
    b, n = qh.shape[:2]
    return qh.reshape(b, n, GQA_KV_HEADS, GQA_GROUP, HEAD_DIM).transpose(0, 2, 3, 1, 4)


def gqa_merge(o):
    b, n = o.shape[0], o.shape[3]
    return o.transpose(0, 3, 1, 2, 4).reshape(b, n, GQA_Q_DIM)


def gqa_latent(q, k, v, k_c, v_c, cos, sin, q_norm, k_norm):
    b, n, _ = q.shape
    qh = apply_rope(gqa_heads_q(q, q_norm), cos, sin)
    kh = apply_rope(rmsnorm(k.reshape(b, n, GQA_KV_HEADS, HEAD_DIM), k_norm), cos, sin)
    vh = v.reshape(b, n, GQA_KV_HEADS, HEAD_DIM)
    k_all = jnp.concatenate([kh.transpose(0, 2, 1, 3), k_c], axis=2)
    v_all = jnp.concatenate([vh.transpose(0, 2, 1, 3), v_c], axis=2)
    return gqa_merge(block_attention(gqa_group_q(qh), k_all, v_all))


def gqa_context(q, k_c, v_c, q_norm):
    return gqa_merge(block_attention(gqa_group_q(gqa_heads_q(q, q_norm)), k_c, v_c))


def fourier_mix(u, w_fnet):
    b, n, _ = u.shape
    ug = u.astype(jnp.float32).reshape(b, n, FNET_GROUPS, FNET_GW)
    f = jnp.fft.fft2(ug, axes=(1, 3), norm="ortho").real
    y = jnp.einsum("bngc,gcd->bngd", f, w_fnet.astype(jnp.float32))
    return y.reshape(b, n, FNET_DIM).astype(u.dtype)


def na_tables(rows):
    wr = min(NA_WIN_R, rows)
    t = jnp.arange(rows * GRID_W, dtype=jnp.int32)
    qr, qc = t // GRID_W, t % GRID_W
    rs = jnp.clip(qr - wr // 2, 0, rows - wr)
    cs = jnp.clip(qc - NA_WIN_C // 2, 0, GRID_W - NA_WIN_C)
    j = jnp.arange(wr * NA_WIN_C, dtype=jnp.int32)
    kr = rs[:, None] + (j // NA_WIN_C)[None, :]
    kc = cs[:, None] + (j % NA_WIN_C)[None, :]
    idx = kr * GRID_W + kc
    dr = kr - qr[:, None] + (NA_WIN_R - 1)
    dc = kc - qc[:, None] + (NA_WIN_C - 1)
    return idx, dr, dc


def na_heads(u):
    b, n, _ = u.shape
    return u.reshape(b, n, NA_HEADS, HEAD_DIM).transpose(0, 2, 1, 3)


def na_latent(q, k, v, k_c, v_c, idx, bias):
    b, n, _ = q.shape
    qh, kh, vh = na_heads(q), na_heads(k), na_heads(v)
    nb = n // BLOCK_Q
    kw = idx.shape[-1]
    qb = jnp.moveaxis(qh.reshape(b, NA_HEADS, nb, BLOCK_Q, HEAD_DIM), 2, 0)
    ib = idx.reshape(nb, BLOCK_Q, kw)
    bb = jnp.moveaxis(bias.reshape(NA_HEADS, nb, BLOCK_Q, kw), 1, 0)
    scale = HEAD_DIM ** -0.5

    def one(args):
        qblk, iblk, bblk = args
        kg = jnp.take(kh, iblk, axis=2)
        vg = jnp.take(vh, iblk, axis=2)
        s_loc = jnp.einsum("bhqd,bhqkd->bhqk", qblk, kg).astype(jnp.float32) * scale + bblk[None]
        s_ctx = jnp.einsum("bhqd,bhmd->bhqm", qblk, k_c).astype(jnp.float32) * scale
        p = jax.nn.softmax(jnp.concatenate([s_loc, s_ctx], axis=-1), axis=-1).astype(vh.dtype)
        return (jnp.einsum("bhqk,bhqkd->bhqd", p[..., :kw], vg)
                + jnp.einsum("bhqm,bhmd->bhqd", p[..., kw:], v_c))

    o = lax.map(one, (qb, ib, bb))
    o = jnp.moveaxis(o, 0, 2).reshape(b, NA_HEADS, n, HEAD_DIM)
    return o.transpose(0, 2, 1, 3).reshape(b, n, NA_DIM)


def na_context(q, k_c, v_c):
    b, m, _ = q.shape
    o = block_attention(na_heads(q)[:, :, None], k_c, v_c)[:, :, 0]
    return o.transpose(0, 2, 1, 3).reshape(b, m, NA_DIM)


def setup_inputs(seed: int = 0) -> dict:
    key = jax.random.key(seed)
    ks = jax.random.split(key, 24)
    f32 = jnp.float32
    L, D = DEPTH, D_MODEL

    def nrm(k, shape, s):
        return jax.random.normal(k, shape, f32) * s

    return {
        "x": nrm(ks[0], (BATCH, SEQ, D), 1.0),
        "c": nrm(ks[1], (BATCH, D), 1.0),
        "ctx": nrm(ks[2], (BATCH, CTX_LEN, D), 1.0),
        "c_ctx": nrm(ks[3], (D,), 1.0),
        "w_mod": nrm(ks[4], (L, D, N_MOD * D), 0.5 * D ** -0.5),
        "b_mod": nrm(ks[5], (L, N_MOD * D), 0.01),
        "ffn1_norm": 1.0 + nrm(ks[6], (L, D), 0.05),
        "ffn1_gate": nrm(ks[7], (L, D, D_FF), D ** -0.5),
        "ffn1_up": nrm(ks[8], (L, D, D_FF), D ** -0.5),
        "ffn1_down": nrm(ks[9], (L, D_FF, D), D_FF ** -0.5),
        "mix_norm": 1.0 + nrm(ks[10], (L, D), 0.05),
        "w_in": nrm(ks[11], (L, D, IN_DIM), D ** -0.5),
        "w_out": nrm(ks[12], (L, MIX_DIM, D), MIX_DIM ** -0.5),
        "pool_w": nrm(ks[13], (L, POOL_GROUPS, POOL_GW, POOL_GW), POOL_GW ** -0.5),
        "pool_scale": 1.0 + nrm(ks[14], (L, POOL_DIM), 0.1),
        "q_norm": 1.0 + nrm(ks[15], (L, HEAD_DIM), 0.05),
        "k_norm": 1.0 + nrm(ks[16], (L, HEAD_DIM), 0.05),
        "fnet_w": nrm(ks[17], (L, FNET_GROUPS, FNET_GW, FNET_GW), FNET_GW ** -0.5),
        "na_rpb": nrm(ks[18], (L, NA_HEADS, 2 * NA_WIN_R - 1, 2 * NA_WIN_C - 1), 0.1),
        "ffn2_norm": 1.0 + nrm(ks[19], (L, D), 0.05),
        "ffn2_gate": nrm(ks[20], (L, D, D_FF), D ** -0.5),
        "ffn2_up": nrm(ks[21], (L, D, D_FF), D ** -0.5),
        "ffn2_down": nrm(ks[22], (L, D_FF, D), D_FF ** -0.5),
        "final_norm": 1.0 + nrm(ks[23], (D,), 0.05),
    }


def reference(x, c, ctx, c_ctx, w_mod, b_mod, ffn1_norm, ffn1_gate, ffn1_up, ffn1_down,
              mix_norm, w_in, w_out, pool_w, pool_scale, q_norm, k_norm, fnet_w, na_rpb,
              ffn2_norm, ffn2_gate, ffn2_up, ffn2_down, final_norm):
    n = x.shape[1]
    rows = n // GRID_W
    cos, sin = axial_rope_tables(n)
    idx, dr, dc = na_tables(rows)
    xc = ctx
    for l in range(DEPTH):
        last = l == DEPTH - 1
        mod_x = (jax.nn.silu(c) @ w_mod[l] + b_mod[l])[:, None, :]
        mod_c = (jax.nn.silu(c_ctx)[None, :] @ w_mod[l] + b_mod[l])[:, None, :]
        sh1, sc1, g1, sh2, sc2, g2, sh3, sc3, g3 = jnp.split(mod_x, N_MOD, axis=-1)
        ch1, cs1, cg1, ch2, cs2, cg2, ch3, cs3, cg3 = jnp.split(mod_c, N_MOD, axis=-1)
        f1 = (ffn1_gate[l], ffn1_up[l], ffn1_down[l])
        f2 = (ffn2_gate[l], ffn2_up[l], ffn2_down[l])

        x = x + 0.5 * g1 * swiglu(modulate(rmsnorm(x, ffn1_norm[l]), sh1, sc1), *f1)
        xc = xc + 0.5 * cg1 * swiglu(modulate(rmsnorm(xc, ffn1_norm[l]), ch1, cs1), *f1)

        px, gqx, gkx, gvx, fx, nqx, nkx, nvx = split_in(modulate(rmsnorm(x, mix_norm[l]), sh2, sc2) @ w_in[l])
        pc, gqc, gkc, gvc, fc, nqc, nkc, nvc = split_in(modulate(rmsnorm(xc, mix_norm[l]), ch2, cs2) @ w_in[l])

        gk_c, gv_c = gqa_ctx_kv(gkc, gvc, k_norm[l])
        nk_c, nv_c = na_heads(nkc), na_heads(nvc)
        bias = na_rpb[l][:, dr, dc].astype(jnp.float32)

        y = jnp.concatenate([
            pool_mix(px, pool_w[l], pool_scale[l]),
            gqa_latent(gqx, gkx, gvx, gk_c, gv_c, cos, sin, q_norm[l], k_norm[l]),
            fourier_mix(fx, fnet_w[l]),
            na_latent(nqx, nkx, nvx, nk_c, nv_c, idx, bias),
        ], axis=-1) @ w_out[l]
        x = x + g2 * y

        if not last:
            yc = jnp.concatenate([
                pool_mix(pc, pool_w[l], pool_scale[l]),
                gqa_context(gqc, gk_c, gv_c, q_norm[l]),
                fourier_mix(fc, fnet_w[l]),
                na_context(nqc, nk_c, nv_c),
            ], axis=-1) @ w_out[l]
            xc = xc + cg2 * yc
            xc = xc + 0.5 * cg3 * swiglu(modulate(rmsnorm(xc, ffn2_norm[l]), ch3, cs3), *f2)

        x = x + 0.5 * g3 * swiglu(modulate(rmsnorm(x, ffn2_norm[l]), sh3, sc3), *f2)
    return rmsnorm(x, final_norm)
```

```python
import functools

import numpy as np
import jax
import jax.numpy as jnp
from jax import lax
from jax.experimental import pallas as pl
from jax.experimental.pallas import tpu as pltpu

F32 = jnp.float32
BF16 = jnp.bfloat16

GRID_W = 64
HEAD_DIM = 128
POOL_WINDOWS = (2, 4, 8, 16)
POOL_HALO = 8
GQA_Q_HEADS = 4
GQA_KV_HEADS = 2
FNET_GROUPS = 4
NA_HEADS = 4
NA_WIN_R = 8
NA_WIN_C = 16
ROPE_THETA = 10000.0
IN_SPLITS = (512, 512, 256, 256, 512, 512, 512, 512)
MIX_GROUP = 512
EPS = 1e-6
N_MOD = 9
MOD_ROWS = 8
MASK_VALUE = -1e30

V7X_VMEM_BYTES = 64 * 1024 * 1024
VMEM_CAP = V7X_VMEM_BYTES - 8 * 1024 * 1024


def _params(semantics, block_bytes, scratch_bytes=0):
    need = 2 * block_bytes + scratch_bytes
    limit = min(max(need + need // 2, 32 * 1024 * 1024), VMEM_CAP)
    return pltpu.CompilerParams(dimension_semantics=semantics, vmem_limit_bytes=limit)


def _nbytes(shape, dtype):
    return int(np.prod(shape)) * jnp.dtype(dtype).itemsize


def _norm_modulate(x, gain, shift, scale):
    y = x * lax.rsqrt(jnp.mean(x * x, axis=-1, keepdims=True) + EPS) * gain
    return y * (1.0 + scale) + shift


def _mod_spec(d, row_of_tile, k):
    return pl.BlockSpec((None, 1, d), lambda i, *_: (row_of_tile(i), 0, k))


def _mod_kernel(c_ref, w_ref, b_ref, o_ref):
    c = c_ref[...]
    s = c * jax.nn.sigmoid(c)
    o_ref[...] = jnp.dot(s.astype(BF16), w_ref[...].astype(BF16), preferred_element_type=F32) + b_ref[...]


def modulation(cvec, w_mod, b_mod, *, tn=512):
    nl, d, nd = w_mod.shape
    blocks = _nbytes((d, tn), F32) + _nbytes((MOD_ROWS, d), F32) + 2 * _nbytes((MOD_ROWS, tn), F32)
    return pl.pallas_call(
        _mod_kernel,
        grid=(nl, nd // tn),
        in_specs=[pl.BlockSpec((MOD_ROWS, d), lambda l, j: (0, 0)),
                  pl.BlockSpec((None, d, tn), lambda l, j: (l, 0, j)),
                  pl.BlockSpec((None, 1, tn), lambda l, j: (l, 0, j))],
        out_specs=pl.BlockSpec((None, MOD_ROWS, tn), lambda l, j: (l, 0, j)),
        out_shape=jax.ShapeDtypeStruct((nl, MOD_ROWS, nd), F32),
        compiler_params=_params(("parallel", "parallel"), blocks, _nbytes((d, tn), BF16)),
        name="modulation",
    )(cvec, w_mod, b_mod.reshape(nl, 1, nd))


def _ffn_kernel(*refs, nj, final):
    if final:
        x_ref, sh_ref, sc_ref, gt_ref, ng_ref, wg_ref, wu_ref, wd_ref, fg_ref, o_ref, h_ref = refs
    else:
        x_ref, sh_ref, sc_ref, gt_ref, ng_ref, wg_ref, wu_ref, wd_ref, o_ref, h_ref = refs
    j = pl.program_id(1)

    @pl.when(j == 0)
    def _():
        h_ref[...] = _norm_modulate(x_ref[...], ng_ref[...], sh_ref[...], sc_ref[...]).astype(BF16)
        o_ref[...] = jnp.zeros_like(o_ref)

    h = h_ref[...]
    g = jnp.dot(h, wg_ref[...], preferred_element_type=F32)
    u = jnp.dot(h, wu_ref[...], preferred_element_type=F32)
    a = (g * jax.nn.sigmoid(g) * u).astype(BF16)
    o_ref[...] += jnp.dot(a, wd_ref[...], preferred_element_type=F32)

    @pl.when(j == nj - 1)
    def _():
        r = x_ref[...] + 0.5 * gt_ref[...] * o_ref[...]
        if final:
            r = r * lax.rsqrt(jnp.mean(r * r, axis=-1, keepdims=True) + EPS) * fg_ref[...]
        o_ref[...] = r


def ffn(x, mod3, k0, row_of_tile, norm_g, wg, wu, wd, final_g=None, *, tm=512, tf=512):
    r, d = x.shape
    f = wg.shape[1]
    tm = min(tm, r)
    nj = f // tf
    final = final_g is not None
    row = lambda i, j: (i, 0)
    in_specs = [pl.BlockSpec((tm, d), row),
                _mod_spec(d, row_of_tile, k0), _mod_spec(d, row_of_tile, k0 + 1), _mod_spec(d, row_of_tile, k0 + 2),
                pl.BlockSpec((1, d), lambda i, j: (0, 0)),
                pl.BlockSpec((d, tf), lambda i, j: (0, j)),
                pl.BlockSpec((d, tf), lambda i, j: (0, j)),
                pl.BlockSpec((tf, d), lambda i, j: (j, 0))]
    args = [x, mod3, mod3, mod3, norm_g.reshape(1, d), wg, wu, wd]
    if final:
        in_specs.append(pl.BlockSpec((1, d), lambda i, j: (0, 0)))
        args.append(final_g.reshape(1, d))
    blocks = 2 * _nbytes((tm, d), F32) + 3 * _nbytes((d, tf), BF16) + 5 * _nbytes((1, d), F32)
    temps = _nbytes((tm, d), BF16) + 4 * _nbytes((tm, tf), F32) + _nbytes((tm, d), F32)
    return pl.pallas_call(
        functools.partial(_ffn_kernel, nj=nj, final=final),
        grid=(r // tm, nj),
        in_specs=in_specs,
        out_specs=pl.BlockSpec((tm, d), row),
        out_shape=jax.ShapeDtypeStruct((r, d), F32),
        scratch_shapes=[pltpu.VMEM((tm, d), BF16)],
        compiler_params=_params(("parallel", "arbitrary"), blocks, temps),
        name="ffn",
    )(*args)


def _rms_heads(z, gain, nheads):
    outs = []
    for hd in range(nheads):
        zh = z[:, hd * HEAD_DIM:(hd + 1) * HEAD_DIM]
        outs.append(zh * lax.rsqrt(jnp.mean(zh * zh, axis=-1, keepdims=True) + EPS) * gain)
    return outs


def _rope(zh, cos, sin_signed):
    lane = lax.broadcasted_iota(jnp.int32, zh.shape, 1)
    partner = jnp.where(lane % 2 == 0, pltpu.roll(zh, HEAD_DIM - 1, 1), pltpu.roll(zh, 1, 1))
    return zh * cos + partner * sin_signed


def _inproj_kernel(*refs, rope):
    if rope:
        (x_ref, sh_ref, sc_ref, ng_ref, w_ref, qn_ref, kn_ref, cos_ref, sin_ref,
         px_ref, gq_ref, gk_ref, gv_ref, fx_ref, nq_ref, nk_ref, nv_ref) = refs
    else:
        (x_ref, sh_ref, sc_ref, ng_ref, w_ref, qn_ref, kn_ref,
         px_ref, gq_ref, gk_ref, gv_ref, fx_ref, nq_ref, nk_ref, nv_ref) = refs
    h = _norm_modulate(x_ref[...], ng_ref[...], sh_ref[...], sc_ref[...]).astype(BF16)
    offs = np.cumsum((0,) + IN_SPLITS)

    def proj(s):
        return jnp.dot(h, w_ref[:, offs[s]:offs[s + 1]], preferred_element_type=F32)

    sm_scale = HEAD_DIM ** -0.5
    px_ref[...] = proj(0)
    q_heads = _rms_heads(proj(1), qn_ref[...], GQA_Q_HEADS)
    k_heads = _rms_heads(proj(2), kn_ref[...], GQA_KV_HEADS)
    if rope:
        cos, sin = cos_ref[...], sin_ref[...]
        q_heads = [_rope(zh, cos, sin) for zh in q_heads]
        k_heads = [_rope(zh, cos, sin) for zh in k_heads]
    gq_ref[...] = jnp.concatenate([zh * sm_scale for zh in q_heads], axis=-1).astype(BF16)
    gk_ref[...] = jnp.concatenate(k_heads, axis=-1).astype(BF16)
    gv_ref[...] = proj(3).astype(BF16)
    fx_ref[...] = proj(4).astype(BF16)
    nq_ref[...] = (proj(5) * sm_scale).astype(BF16)
    nk_ref[...] = proj(6).astype(BF16)
    nv_ref[...] = proj(7).astype(BF16)


def in_projection(x, mod3, k0, row_of_tile, norm_g, w_in, q_norm, k_norm, rope_tables=None, *, tm=512):
    r, d = x.shape
    tm = min(tm, r)
    rope = rope_tables is not None
    row = lambda i: (i, 0)
    const = lambda i: (0, 0)
    in_specs = [pl.BlockSpec((tm, d), row),
                _mod_spec(d, row_of_tile, k0), _mod_spec(d, row_of_tile, k0 + 1),
                pl.BlockSpec((1, d), const),
                pl.BlockSpec(w_in.shape, const, pipeline_mode=pl.Buffered(1)),
                pl.BlockSpec((1, HEAD_DIM), const), pl.BlockSpec((1, HEAD_DIM), const)]
    args = [x, mod3, mod3, norm_g.reshape(1, d), w_in, q_norm.reshape(1, HEAD_DIM), k_norm.reshape(1, HEAD_DIM)]
    if rope:
        tiles_per_seq = rope_tables[0].shape[0] // tm
        tab = pl.BlockSpec((tm, HEAD_DIM), lambda i: (i % tiles_per_seq, 0))
        in_specs += [tab, tab]
        args += list(rope_tables)
    widths = (512, 512, 256, 256, 512, 512, 512, 512)
    dtypes = (F32,) + (BF16,) * 7
    out_specs = [pl.BlockSpec((tm, w), row) for w in widths]
    out_shape = [jax.ShapeDtypeStruct((r, w), dt) for w, dt in zip(widths, dtypes)]
    blocks = _nbytes((tm, d), F32) + _nbytes(w_in.shape, BF16) // 2 + _nbytes((tm, 4096), F32)
    temps = _nbytes((tm, d), BF16) + 2 * _nbytes((tm, d), F32) + 6 * _nbytes((tm, 512), F32)
    return pl.pallas_call(
        functools.partial(_inproj_kernel, rope=rope),
        grid=(r // tm,),
        in_specs=in_specs,
        out_specs=out_specs,
        out_shape=out_shape,
        compiler_params=_params(("parallel",), blocks, temps),
        name="in_projection",
    )(*args)


def _pool_kernel(prev_ref, cur_ref, next_ref, w_ref, sc_ref, o_ref, *, n, ts):
    tiles_per_seq = n // ts
    s = pl.program_id(0) % tiles_per_seq
    cur = cur_ref[...]
    prev = jnp.where(s == 0, 0.0, prev_ref[...])
    nxt = jnp.where(s == tiles_per_seq - 1, 0.0, next_ref[...])
    ext = jnp.concatenate([prev, cur, nxt], axis=0)
    rows = ts + 2 * POOL_HALO
    t = s * ts + lax.broadcasted_iota(jnp.int32, (ts, 1), 0)
    gw = cur.shape[1] // len(POOL_WINDOWS)
    for gi, w in enumerate(POOL_WINDOWS):
        e = ext[:, gi * gw:(gi + 1) * gw]
        width = 1
        while width < w:
            e = e + pltpu.roll(e, rows - width, 0)
            width *= 2
        lead = POOL_HALO - w // 2
        win = (pltpu.roll(e, rows - lead, 0) if lead else e)[:ts]
        lo = jnp.maximum(t - w // 2, 0)
        hi = jnp.minimum(t + w // 2 - 1, n - 1)
        diff = win / (hi - lo + 1).astype(F32) - cur[:, gi * gw:(gi + 1) * gw]
        y = jnp.dot(diff.astype(BF16), w_ref[gi], preferred_element_type=F32)
        o_ref[:, gi * gw:(gi + 1) * gw] = (y * sc_ref[:, gi * gw:(gi + 1) * gw]).astype(o_ref.dtype)


def pool_mix(px, n, w_pool, scale, *, ts=512):
    r, c = px.shape
    ts = min(ts, n)
    hb = ts // POOL_HALO
    last_halo = r // POOL_HALO - 1
    blocks = 2 * _nbytes((ts, c), F32) + _nbytes(w_pool.shape, BF16)
    return pl.pallas_call(
        functools.partial(_pool_kernel, n=n, ts=ts),
        grid=(r // ts,),
        in_specs=[pl.BlockSpec((POOL_HALO, c), lambda i: (jnp.maximum(i * hb - 1, 0), 0)),
                  pl.BlockSpec((ts, c), lambda i: (i, 0)),
                  pl.BlockSpec((POOL_HALO, c), lambda i: (jnp.minimum((i + 1) * hb, last_halo), 0)),
                  pl.BlockSpec(w_pool.shape, lambda i: (0, 0, 0)),
                  pl.BlockSpec((1, c), lambda i: (0, 0))],
        out_specs=pl.BlockSpec((ts, c), lambda i: (i, 0)),
        out_shape=jax.ShapeDtypeStruct((r, c), BF16),
        compiler_params=_params(("parallel",), blocks, 6 * _nbytes((ts + 2 * POOL_HALO, c), F32)),
        name="pool_mix",
    )(px, px, px, w_pool, scale.reshape(1, c))


def _dft_chan_kernel(u_ref, c_ref, s_ref, w_ref, ap_ref, aq_ref, pq_ref, *, norm):
    gw = c_ref.shape[0]

    @pl.when(pl.program_id(0) == 0)
    def _():
        for g in range(FNET_GROUPS):
            w = w_ref[g]
            p = jnp.dot(c_ref[...], w, preferred_element_type=F32, precision=lax.Precision.HIGHEST)
            q = jnp.dot(s_ref[...], w, preferred_element_type=F32, precision=lax.Precision.HIGHEST)
            pq_ref[g] = (jnp.concatenate([p, -q], axis=-1) * norm).astype(BF16)

    for g in range(FNET_GROUPS):
        a = jnp.dot(u_ref[:, g * gw:(g + 1) * gw], pq_ref[g], preferred_element_type=F32)
        ap_ref[:, g * gw:(g + 1) * gw] = a[:, :gw].astype(BF16)
        aq_ref[:, g * gw:(g + 1) * gw] = a[:, gw:].astype(BF16)


def _dft_seq_kernel(c_ref, s_ref, ap_ref, aq_ref, o_ref, acc_ref, *, nn):
    j = pl.program_id(1)

    @pl.when(j == 0)
    def _():
        acc_ref[...] = jnp.zeros_like(acc_ref)

    for b in range(acc_ref.shape[0]):
        acc_ref[b] += (jnp.dot(c_ref[...], ap_ref[b], preferred_element_type=F32)
                       + jnp.dot(s_ref[...], aq_ref[b], preferred_element_type=F32))

    @pl.when(j == nn - 1)
    def _():
        o_ref[...] = acc_ref[...].astype(o_ref.dtype)


def _dft_tables(n):
    k = jnp.arange(n, dtype=jnp.int32)
    ang = ((k[:, None] * k[None, :]) % n).astype(F32) * (2.0 * np.pi / n)
    return jnp.cos(ang), jnp.sin(ang)


def fourier_mix(fx, nb, n, w_fnet, seq_tables, chan_tables, *, tm=512, tk=512, tn=512):
    r, c = fx.shape
    gw = c // FNET_GROUPS
    tm, tk, tn = min(tm, r), min(tk, n), min(tn, n)
    cc, sc = chan_tables
    ap, aq = pl.pallas_call(
        functools.partial(_dft_chan_kernel, norm=float(1.0 / np.sqrt(n * gw))),
        grid=(r // tm,),
        in_specs=[pl.BlockSpec((tm, c), lambda i: (i, 0)),
                  pl.BlockSpec((gw, gw), lambda i: (0, 0)),
                  pl.BlockSpec((gw, gw), lambda i: (0, 0)),
                  pl.BlockSpec(w_fnet.shape, lambda i: (0, 0, 0))],
        out_specs=[pl.BlockSpec((tm, c), lambda i: (i, 0))] * 2,
        out_shape=[jax.ShapeDtypeStruct((r, c), BF16)] * 2,
        scratch_shapes=[pltpu.VMEM((FNET_GROUPS, gw, 2 * gw), BF16)],
        compiler_params=_params(("arbitrary",), 3 * _nbytes((tm, c), BF16) + 6 * _nbytes((gw, gw), F32),
                                4 * _nbytes((tm, c), F32)),
        name="dft_channels",
    )(fx, cc, sc, w_fnet)
    cn, sn = seq_tables
    nn = n // tn
    blocks = 2 * _nbytes((tk, tn), BF16) + 2 * _nbytes((nb, tn, c), BF16) + _nbytes((nb, tk, c), BF16)
    y = pl.pallas_call(
        functools.partial(_dft_seq_kernel, nn=nn),
        grid=(n // tk, nn),
        in_specs=[pl.BlockSpec((tk, tn), lambda i, j: (i, j)),
                  pl.BlockSpec((tk, tn), lambda i, j: (i, j)),
                  pl.BlockSpec((nb, tn, c), lambda i, j: (0, j, 0)),
                  pl.BlockSpec((nb, tn, c), lambda i, j: (0, j, 0))],
        out_specs=pl.BlockSpec((nb, tk, c), lambda i, j: (0, i, 0)),
        out_shape=jax.ShapeDtypeStruct((nb, n, c), BF16),
        scratch_shapes=[pltpu.VMEM((nb, tk, c), F32)],
        compiler_params=_params(("parallel", "arbitrary"), blocks, 3 * _nbytes((nb, tk, c), F32)),
        name="dft_positions",
    )(cn, sn, ap.reshape(nb, n, c), aq.reshape(nb, n, c))
    return y.reshape(r, c)


def _flash_kernel(*refs, n_lat, group, tq):
    if n_lat:
        q_ref, kl_ref, vl_ref, kc_ref, vc_ref, o_ref, qs_ref, m_ref, l_ref, acc_ref = refs
    else:
        q_ref, kc_ref, vc_ref, o_ref, qs_ref, m_ref, l_ref, acc_ref = refs
    ki = pl.program_id(3)

    @pl.when(ki == 0)
    def _():
        for g in range(group):
            qs_ref[g * tq:(g + 1) * tq, :] = q_ref[:, g * HEAD_DIM:(g + 1) * HEAD_DIM]
        m_ref[...] = jnp.full_like(m_ref, MASK_VALUE)
        l_ref[...] = jnp.zeros_like(l_ref)
        acc_ref[...] = jnp.zeros_like(acc_ref)

    def step(k, v):
        s = lax.dot_general(qs_ref[...], k, (((1,), (1,)), ((), ())), preferred_element_type=F32)
        m_old = m_ref[...]
        m_new = jnp.maximum(m_old, jnp.max(s, axis=-1, keepdims=True))
        alpha = jnp.exp(m_old - m_new)
        p = jnp.exp(s - m_new)
        l_ref[...] = alpha * l_ref[...] + jnp.sum(p, axis=-1, keepdims=True)
        acc_ref[...] = alpha * acc_ref[...] + jnp.dot(p.astype(BF16), v, preferred_element_type=F32)
        m_ref[...] = m_new

    if n_lat:
        @pl.when(ki < n_lat)
        def _():
            step(kl_ref[...], vl_ref[...])

    @pl.when(ki == n_lat)
    def _():
        step(kc_ref[...], vc_ref[...])
        o = acc_ref[...] / l_ref[...]
        for g in range(group):
            o_ref[:, g * HEAD_DIM:(g + 1) * HEAD_DIM] = o[g * tq:(g + 1) * tq].astype(o_ref.dtype)


def flash_attention(q, nb, nq, kv_heads, k_ctx, v_ctx, m_ctx, k_lat=None, v_lat=None, *, tq=512, tk=512):
    heads = q.shape[1] // HEAD_DIM
    group = heads // kv_heads
    tq = min(tq, nq)
    nqt = nq // tq
    has_lat = k_lat is not None
    n_lat = nq // tk if has_lat else 0
    gd = group * HEAD_DIM
    in_specs = [pl.BlockSpec((tq, gd), lambda b, h, i, j: (b * nqt + i, h))]
    args = [q]
    if has_lat:
        lat = pl.BlockSpec((tk, HEAD_DIM), lambda b, h, i, j: (b * n_lat + jnp.minimum(j, n_lat - 1), h))
        in_specs += [lat, lat]
        args += [k_lat, v_lat]
    ctx = pl.BlockSpec((m_ctx, HEAD_DIM), lambda b, h, i, j: (b, h))
    in_specs += [ctx, ctx]
    args += [k_ctx, v_ctx]
    rows = group * tq
    blocks = 2 * _nbytes((tq, gd), BF16) + 2 * _nbytes((tk, HEAD_DIM), BF16) + 2 * _nbytes((m_ctx, HEAD_DIM), BF16)
    scratch = [pltpu.VMEM((rows, HEAD_DIM), BF16), pltpu.VMEM((rows, 1), F32), pltpu.VMEM((rows, 1), F32),
               pltpu.VMEM((rows, HEAD_DIM), F32)]
    temps = 4 * _nbytes((rows, HEAD_DIM), F32) + 3 * _nbytes((rows, max(tk, m_ctx)), F32)
    return pl.pallas_call(
        functools.partial(_flash_kernel, n_lat=n_lat, group=group, tq=tq),
        grid=(nb, kv_heads, nqt, n_lat + 1),
        in_specs=in_specs,
        out_specs=pl.BlockSpec((tq, gd), lambda b, h, i, j: (b * nqt + i, h)),
        out_shape=jax.ShapeDtypeStruct(q.shape, BF16),
        scratch_shapes=scratch,
        compiler_params=_params(("parallel", "parallel", "parallel", "arbitrary"), blocks, temps),
        name="flash_attention",
    )(*args)


def _na_plan(rows, rq):
    wr = min(NA_WIN_R, rows)
    band = min(rq + wr - 1, rows)
    nblk = rows // rq
    starts, deltas = [], []
    for j in range(nblk):
        st = int(np.clip(rq * j - wr // 2, 0, rows - band))
        starts.append(st)
        deltas.append(st - rq * j)
    uniq = sorted(set(deltas))
    var = [uniq.index(dl) for dl in deltas]
    e = np.arange(rq)[:, None, None, None]
    qc = np.arange(GRID_W)[None, :, None, None]
    a = np.arange(band)[None, None, :, None]
    kc = np.arange(GRID_W)[None, None, None, :]
    dr_l, dc_l, ok_l = [], [], []
    for dl in uniq:
        j = deltas.index(dl)
        qr = rq * j + e
        kr = starts[j] + a
        rs = np.clip(qr - wr // 2, 0, rows - wr)
        cs = np.clip(qc - NA_WIN_C // 2, 0, GRID_W - NA_WIN_C)
        ok = (kr >= rs) & (kr < rs + wr) & (kc >= cs) & (kc < cs + NA_WIN_C)
        dr = np.broadcast_to(kr - qr + (NA_WIN_R - 1), ok.shape)
        dc = np.broadcast_to(kc - qc + (NA_WIN_C - 1), ok.shape)
        shape2 = (rq * GRID_W, band * GRID_W)
        ok_l.append(ok.reshape(shape2))
        dr_l.append(np.where(ok, dr, 0).reshape(shape2))
        dc_l.append(np.where(ok, dc, 0).reshape(shape2))
    return (np.asarray(starts, np.int32), np.asarray(var, np.int32), band,
            np.stack(dr_l), np.stack(dc_l), np.stack(ok_l))


def _na_kernel(start_ref, var_ref, q_ref, k_ref, v_ref, kc_ref, vc_ref, bias_ref, o_ref, *, band_keys):
    del var_ref
    j = pl.program_id(1)
    st = pl.multiple_of(start_ref[j] * GRID_W, GRID_W)
    dn = (((1,), (1,)), ((), ()))
    for h in range(NA_HEADS):
        hs = slice(h * HEAD_DIM, (h + 1) * HEAD_DIM)
        q = q_ref[:, hs]
        kw = k_ref[pl.ds(st, band_keys), hs]
        vw = v_ref[pl.ds(st, band_keys), hs]
        s_loc = lax.dot_general(q, kw, dn, preferred_element_type=F32) + bias_ref[h]
        s_ctx = lax.dot_general(q, kc_ref[:, hs], dn, preferred_element_type=F32)
        m = jnp.maximum(jnp.max(s_loc, axis=-1, keepdims=True), jnp.max(s_ctx, axis=-1, keepdims=True))
        p_loc = jnp.exp(s_loc - m)
        p_ctx = jnp.exp(s_ctx - m)
        l = jnp.sum(p_loc, axis=-1, keepdims=True) + jnp.sum(p_ctx, axis=-1, keepdims=True)
        o = (jnp.dot(p_loc.astype(BF16), vw, preferred_element_type=F32)
             + jnp.dot(p_ctx.astype(BF16), vc_ref[:, hs], preferred_element_type=F32))
        o_ref[:, hs] = (o / l).astype(o_ref.dtype)


def neighbourhood_attention(q, k, v, k_ctx, v_ctx, nb, n, m_ctx, rpb, *, rq=4):
    rows = n // GRID_W
    rq = min(rq, rows)
    starts, var, band, dr, dc, ok = _na_plan(rows, rq)
    bias = jnp.where(ok[:, None], rpb[:, dr, dc].transpose(1, 0, 2, 3), MASK_VALUE).astype(F32)
    tq, band_keys = rq * GRID_W, band * GRID_W
    nblk = rows // rq
    c = q.shape[1]
    blocks = (_nbytes((tq, c), BF16) * 2 + 2 * _nbytes((n, c), BF16) + 2 * _nbytes((m_ctx, c), BF16)
              + _nbytes((NA_HEADS, tq, band_keys), F32))
    grid_spec = pltpu.PrefetchScalarGridSpec(
        num_scalar_prefetch=2,
        grid=(nb, nblk),
        in_specs=[pl.BlockSpec((tq, c), lambda b, j, st, vr: (b * nblk + j, 0)),
                  pl.BlockSpec((n, c), lambda b, j, st, vr: (b, 0)),
                  pl.BlockSpec((n, c), lambda b, j, st, vr: (b, 0)),
                  pl.BlockSpec((m_ctx, c), lambda b, j, st, vr: (b, 0)),
                  pl.BlockSpec((m_ctx, c), lambda b, j, st, vr: (b, 0)),
                  pl.BlockSpec((None, NA_HEADS, tq, band_keys), lambda b, j, st, vr: (vr[j], 0, 0, 0))],
        out_specs=pl.BlockSpec((tq, c), lambda b, j, st, vr: (b * nblk + j, 0)),
    )
    return pl.pallas_call(
        functools.partial(_na_kernel, band_keys=band_keys),
        grid_spec=grid_spec,
        out_shape=jax.ShapeDtypeStruct(q.shape, BF16),
        compiler_params=_params(("parallel", "arbitrary"), blocks, 6 * _nbytes((tq, band_keys + m_ctx), F32)),
        name="neighbourhood_attention",
    )(jnp.asarray(starts), jnp.asarray(var), q, k, v, k_ctx, v_ctx, bias)


def _outproj_kernel(x_ref, gt_ref, y0_ref, y1_ref, y2_ref, y3_ref, w_ref, o_ref):
    y = jnp.concatenate([y0_ref[...], y1_ref[...], y2_ref[...], y3_ref[...]], axis=-1)
    o_ref[...] = x_ref[...] + gt_ref[...] * jnp.dot(y, w_ref[...], preferred_element_type=F32)


def out_projection(x, mod3, k_gate, row_of_tile, ys, w_out, *, tm=512):
    r, d = x.shape
    tm = min(tm, r)
    row = lambda i: (i, 0)
    blocks = 2 * _nbytes((tm, d), F32) + _nbytes((tm, d), BF16) + _nbytes(w_out.shape, BF16) // 2
    return pl.pallas_call(
        _outproj_kernel,
        grid=(r // tm,),
        in_specs=[pl.BlockSpec((tm, d), row), _mod_spec(d, row_of_tile, k_gate)]
                 + [pl.BlockSpec((tm, MIX_GROUP), row)] * 4
                 + [pl.BlockSpec(w_out.shape, lambda i: (0, 0), pipeline_mode=pl.Buffered(1))],
        out_specs=pl.BlockSpec((tm, d), row),
        out_shape=jax.ShapeDtypeStruct((r, d), F32),
        compiler_params=_params(("parallel",), blocks, 3 * _nbytes((tm, d), F32)),
        name="out_projection",
    )(x, mod3, *ys, w_out)


def _rope_tables(n):
    t = jnp.arange(n, dtype=jnp.int32)
    row = (t // GRID_W).astype(F32)
    col = (t % GRID_W).astype(F32)
    n_freq = HEAD_DIM // 4
    inv = 1.0 / (ROPE_THETA ** (jnp.arange(n_freq, dtype=F32) / n_freq))
    ang = jnp.concatenate([row[:, None] * inv, col[:, None] * inv], axis=-1)
    cos = jnp.repeat(jnp.cos(ang), 2, axis=-1)
    sin = jnp.repeat(jnp.sin(ang), 2, axis=-1)
    sign = jnp.tile(jnp.asarray([-1.0, 1.0], F32), HEAD_DIM // 2)
    return cos, sin * sign


def kernel(x, c, ctx, c_ctx, w_mod, b_mod, ffn1_norm, ffn1_gate, ffn1_up, ffn1_down, mix_norm, w_in, w_out,
           pool_w, pool_scale, q_norm, k_norm, fnet_w, na_rpb, ffn2_norm, ffn2_gate, ffn2_up, ffn2_down, final_norm):
    nb, n, d = x.shape
    m = ctx.shape[1]
    depth = w_mod.shape[0]
    assert nb + 1 <= MOD_ROWS

    cvec = jnp.zeros((MOD_ROWS, d), F32).at[:nb].set(c).at[nb].set(c_ctx)
    mod = modulation(cvec, w_mod, b_mod)

    rope = _rope_tables(n)
    seq_tables = tuple(t.astype(BF16) for t in _dft_tables(n))
    ctx_tables = tuple(t.astype(BF16) for t in _dft_tables(m))
    chan_tables = _dft_tables(MIX_GROUP // FNET_GROUPS)

    bf = lambda w: w.astype(BF16)
    xl = x.reshape(nb * n, d)
    xc = ctx.reshape(nb * m, d)
    ctx_row = lambda i: nb

    for l in range(depth):
        last = l == depth - 1
        mod3 = mod[l].reshape(MOD_ROWS, 1, N_MOD * d)
        f1 = (bf(ffn1_gate[l]), bf(ffn1_up[l]), bf(ffn1_down[l]))
        f2 = (bf(ffn2_gate[l]), bf(ffn2_up[l]), bf(ffn2_down[l]))
        w_in_l, w_out_l, pool_w_l = bf(w_in[l]), bf(w_out[l]), bf(pool_w[l])

        def lat_row(tm):
            return lambda i: (i * tm) // n

        xl = ffn(xl, mod3, 0, lat_row(512), ffn1_norm[l], *f1)
        xc = ffn(xc, mod3, 0, ctx_row, ffn1_norm[l], *f1)

        px, gq, gk, gv, fx, nq, nk, nv = in_projection(
            xl, mod3, 3, lat_row(512), mix_norm[l], w_in_l, q_norm[l], k_norm[l], rope)
        pc, gqc, gkc, gvc, fc, nqc, nkc, nvc = in_projection(
            xc, mod3, 3, ctx_row, mix_norm[l], w_in_l, q_norm[l], k_norm[l])

        ys = (pool_mix(px, n, pool_w_l, pool_scale[l]),
              flash_attention(gq, nb, n, GQA_KV_HEADS, gkc, gvc, m, gk, gv),
              fourier_mix(fx, nb, n, fnet_w[l], seq_tables, chan_tables),
              neighbourhood_attention(nq, nk, nv, nkc, nvc, nb, n, m, na_rpb[l]))
        xl = out_projection(xl, mod3, 5, lat_row(512), ys, w_out_l)

        if not last:
            ycs = (pool_mix(pc, m, pool_w_l, pool_scale[l]),
                   flash_attention(gqc, nb, m, GQA_KV_HEADS, gkc, gvc, m),
                   fourier_mix(fc, nb, m, fnet_w[l], ctx_tables, chan_tables),
                   flash_attention(nqc, nb, m, NA_HEADS, nkc, nvc, m))
            xc = out_projection(xc, mod3, 5, ctx_row, ycs, w_out_l)
            xc = ffn(xc, mod3, 6, ctx_row, ffn2_norm[l], *f2)

        xl = ffn(xl, mod3, 6, lat_row(512), ffn2_norm[l], *f2, final_g=final_norm if last else None)
    return xl.reshape(nb, n, d)
```

```python
import functools

import numpy as np
import jax
import jax.numpy as jnp
from jax import lax
from jax.experimental import pallas as pl
from jax.experimental.pallas import tpu as pltpu

F32 = jnp.float32
BF16 = jnp.bfloat16

GRID_W = 64
HEAD_DIM = 128
POOL_WINDOWS = (2, 4, 8, 16)
POOL_HALO = 8
GQA_Q_HEADS = 4
GQA_KV_HEADS = 2
FNET_GROUPS = 4
NA_HEADS = 4
NA_WIN_R = 8
NA_WIN_C = 16
ROPE_THETA = 10000.0
IN_SPLITS = (512, 512, 256, 256, 512, 512, 512, 512)
MIX_GROUP = 512
EPS = 1e-6
N_MOD = 9
MOD_ROWS = 8
MASK_VALUE = -1e30
LOG2E = float(np.log2(np.e))

V7X_VMEM_BYTES = 64 * 1024 * 1024
VMEM_CAP = V7X_VMEM_BYTES - 8 * 1024 * 1024


def _params(semantics, block_bytes, scratch_bytes=0):
    need = 2 * block_bytes + scratch_bytes
    limit = min(max(need + need // 2, 32 * 1024 * 1024), VMEM_CAP)
    return pltpu.CompilerParams(dimension_semantics=semantics, vmem_limit_bytes=limit)


def _nbytes(shape, dtype):
    return int(np.prod(shape)) * jnp.dtype(dtype).itemsize


def _norm_modulate(x, gain, shift, scale):
    y = x * lax.rsqrt(jnp.mean(x * x, axis=-1, keepdims=True) + EPS) * gain
    return y * (1.0 + scale) + shift


def _mod_spec(d, row_of_tile, k):
    return pl.BlockSpec((None, 1, d), lambda i, *_: (row_of_tile(i), 0, k))


def _mod_kernel(c_ref, w_ref, b_ref, o_ref):
    c = c_ref[...]
    s = c * jax.nn.sigmoid(c)
    o_ref[...] = jnp.dot(s.astype(BF16), w_ref[...].astype(BF16), preferred_element_type=F32) + b_ref[...]


def modulation(cvec, w_mod, b_mod, *, tn=512):
    nl, d, nd = w_mod.shape
    blocks = _nbytes((d, tn), F32) + _nbytes((MOD_ROWS, d), F32) + 2 * _nbytes((MOD_ROWS, tn), F32)
    return pl.pallas_call(
        _mod_kernel,
        grid=(nl, nd // tn),
        in_specs=[pl.BlockSpec((MOD_ROWS, d), lambda l, j: (0, 0)),
                  pl.BlockSpec((None, d, tn), lambda l, j: (l, 0, j)),
                  pl.BlockSpec((None, 1, tn), lambda l, j: (l, 0, j))],
        out_specs=pl.BlockSpec((None, MOD_ROWS, tn), lambda l, j: (l, 0, j)),
        out_shape=jax.ShapeDtypeStruct((nl, MOD_ROWS, nd), F32),
        compiler_params=_params(("parallel", "parallel"), blocks, _nbytes((d, tn), BF16)),
        name="modulation",
    )(cvec, w_mod, b_mod.reshape(nl, 1, nd))


def _ffn_kernel(*refs, nj, final):
    if final:
        x_ref, sh_ref, sc_ref, gt_ref, ng_ref, wg_ref, wu_ref, wd_ref, fg_ref, o_ref, h_ref = refs
    else:
        x_ref, sh_ref, sc_ref, gt_ref, ng_ref, wg_ref, wu_ref, wd_ref, o_ref, h_ref = refs
    j = pl.program_id(1)

    @pl.when(j == 0)
    def _():
        h_ref[...] = _norm_modulate(x_ref[...], ng_ref[...], sh_ref[...], sc_ref[...]).astype(BF16)
        o_ref[...] = jnp.zeros_like(o_ref)

    h = h_ref[...]
    g = jnp.dot(h, wg_ref[...], preferred_element_type=F32)
    u = jnp.dot(h, wu_ref[...], preferred_element_type=F32)
    a = (g * jax.nn.sigmoid(g) * u).astype(BF16)
    o_ref[...] += jnp.dot(a, wd_ref[...], preferred_element_type=F32)

    @pl.when(j == nj - 1)
    def _():
        r = x_ref[...] + 0.5 * gt_ref[...] * o_ref[...]
        if final:
            r = r * lax.rsqrt(jnp.mean(r * r, axis=-1, keepdims=True) + EPS) * fg_ref[...]
        o_ref[...] = r


def ffn(x, mod3, k0, row_of_tile, norm_g, wg, wu, wd, final_g=None, *, tm=512, tf=512):
    r, d = x.shape
    f = wg.shape[1]
    tm = min(tm, r)
    nj = f // tf
    final = final_g is not None
    row = lambda i, j: (i, 0)
    in_specs = [pl.BlockSpec((tm, d), row),
                _mod_spec(d, row_of_tile, k0), _mod_spec(d, row_of_tile, k0 + 1), _mod_spec(d, row_of_tile, k0 + 2),
                pl.BlockSpec((1, d), lambda i, j: (0, 0)),
                pl.BlockSpec((d, tf), lambda i, j: (0, j)),
                pl.BlockSpec((d, tf), lambda i, j: (0, j)),
                pl.BlockSpec((tf, d), lambda i, j: (j, 0))]
    args = [x, mod3, mod3, mod3, norm_g.reshape(1, d), wg, wu, wd]
    if final:
        in_specs.append(pl.BlockSpec((1, d), lambda i, j: (0, 0)))
        args.append(final_g.reshape(1, d))
    blocks = 2 * _nbytes((tm, d), F32) + 3 * _nbytes((d, tf), BF16) + 5 * _nbytes((1, d), F32)
    temps = _nbytes((tm, d), BF16) + 4 * _nbytes((tm, tf), F32) + _nbytes((tm, d), F32)
    return pl.pallas_call(
        functools.partial(_ffn_kernel, nj=nj, final=final),
        grid=(r // tm, nj),
        in_specs=in_specs,
        out_specs=pl.BlockSpec((tm, d), row),
        out_shape=jax.ShapeDtypeStruct((r, d), F32),
        scratch_shapes=[pltpu.VMEM((tm, d), BF16)],
        compiler_params=_params(("parallel", "arbitrary"), blocks, temps),
        name="ffn",
    )(*args)


def _rms_heads(z, gain, nheads):
    outs = []
    for hd in range(nheads):
        zh = z[:, hd * HEAD_DIM:(hd + 1) * HEAD_DIM]
        outs.append(zh * lax.rsqrt(jnp.mean(zh * zh, axis=-1, keepdims=True) + EPS) * gain)
    return outs


def _rope(zh, cos, sin_signed):
    lane = lax.broadcasted_iota(jnp.int32, zh.shape, 1)
    partner = jnp.where(lane % 2 == 0, pltpu.roll(zh, HEAD_DIM - 1, 1), pltpu.roll(zh, 1, 1))
    return zh * cos + partner * sin_signed


def _inproj_kernel(*refs, rope):
    if rope:
        (x_ref, sh_ref, sc_ref, ng_ref, w_ref, qn_ref, kn_ref, cos_ref, sin_ref,
         px_ref, gq_ref, gk_ref, gv_ref, fx_ref, nq_ref, nk_ref, nv_ref) = refs
    else:
        (x_ref, sh_ref, sc_ref, ng_ref, w_ref, qn_ref, kn_ref,
         px_ref, gq_ref, gk_ref, gv_ref, fx_ref, nq_ref, nk_ref, nv_ref) = refs
    h = _norm_modulate(x_ref[...], ng_ref[...], sh_ref[...], sc_ref[...]).astype(BF16)
    offs = np.cumsum((0,) + IN_SPLITS)

    def proj(s):
        return jnp.dot(h, w_ref[:, offs[s]:offs[s + 1]], preferred_element_type=F32)

    sm_scale = HEAD_DIM ** -0.5 * LOG2E
    px_ref[...] = proj(0)
    q_heads = _rms_heads(proj(1), qn_ref[...], GQA_Q_HEADS)
    k_heads = _rms_heads(proj(2), kn_ref[...], GQA_KV_HEADS)
    if rope:
        cos, sin = cos_ref[...], sin_ref[...]
        q_heads = [_rope(zh, cos, sin) for zh in q_heads]
        k_heads = [_rope(zh, cos, sin) for zh in k_heads]
    gq_ref[...] = jnp.concatenate([zh * sm_scale for zh in q_heads], axis=-1).astype(BF16)
    gk_ref[...] = jnp.concatenate(k_heads, axis=-1).astype(BF16)
    gv_ref[...] = proj(3).astype(BF16)
    fx_ref[...] = proj(4).astype(BF16)
    nq_ref[...] = (proj(5) * sm_scale).astype(BF16)
    nk_ref[...] = proj(6).astype(BF16)
    nv_ref[...] = proj(7).astype(BF16)


def in_projection(x, mod3, k0, row_of_tile, norm_g, w_in, q_norm, k_norm, rope_tables=None, *, tm=512):
    r, d = x.shape
    tm = min(tm, r)
    rope = rope_tables is not None
    row = lambda i: (i, 0)
    const = lambda i: (0, 0)
    in_specs = [pl.BlockSpec((tm, d), row),
                _mod_spec(d, row_of_tile, k0), _mod_spec(d, row_of_tile, k0 + 1),
                pl.BlockSpec((1, d), const),
                pl.BlockSpec(w_in.shape, const, pipeline_mode=pl.Buffered(1)),
                pl.BlockSpec((1, HEAD_DIM), const), pl.BlockSpec((1, HEAD_DIM), const)]
    args = [x, mod3, mod3, norm_g.reshape(1, d), w_in, q_norm.reshape(1, HEAD_DIM), k_norm.reshape(1, HEAD_DIM)]
    if rope:
        tiles_per_seq = rope_tables[0].shape[0] // tm
        tab = pl.BlockSpec((tm, HEAD_DIM), lambda i: (i % tiles_per_seq, 0))
        in_specs += [tab, tab]
        args += list(rope_tables)
    widths = (512, 512, 256, 256, 512, 512, 512, 512)
    dtypes = (F32,) + (BF16,) * 7
    out_specs = [pl.BlockSpec((tm, w), row) for w in widths]
    out_shape = [jax.ShapeDtypeStruct((r, w), dt) for w, dt in zip(widths, dtypes)]
    blocks = _nbytes((tm, d), F32) + _nbytes(w_in.shape, BF16) // 2 + _nbytes((tm, 4096), F32)
    temps = _nbytes((tm, d), BF16) + 2 * _nbytes((tm, d), F32) + 6 * _nbytes((tm, 512), F32)
    return pl.pallas_call(
        functools.partial(_inproj_kernel, rope=rope),
        grid=(r // tm,),
        in_specs=in_specs,
        out_specs=out_specs,
        out_shape=out_shape,
        compiler_params=_params(("parallel",), blocks, temps),
        name="in_projection",
    )(*args)


def _pool_kernel(prev_ref, cur_ref, next_ref, w_ref, sc_ref, o_ref, *, n, ts):
    tiles_per_seq = n // ts
    s = pl.program_id(0) % tiles_per_seq
    cur = cur_ref[...]
    prev = jnp.where(s == 0, 0.0, prev_ref[...])
    nxt = jnp.where(s == tiles_per_seq - 1, 0.0, next_ref[...])
    ext = jnp.concatenate([prev, cur, nxt], axis=0)
    rows = ts + 2 * POOL_HALO
    t = s * ts + lax.broadcasted_iota(jnp.int32, (ts, 1), 0)
    gw = cur.shape[1] // len(POOL_WINDOWS)
    for gi, w in enumerate(POOL_WINDOWS):
        e = ext[:, gi * gw:(gi + 1) * gw]
        width = 1
        while width < w:
            e = e + pltpu.roll(e, rows - width, 0)
            width *= 2
        lead = POOL_HALO - w // 2
        win = (pltpu.roll(e, rows - lead, 0) if lead else e)[:ts]
        lo = jnp.maximum(t - w // 2, 0)
        hi = jnp.minimum(t + w // 2 - 1, n - 1)
        diff = win / (hi - lo + 1).astype(F32) - cur[:, gi * gw:(gi + 1) * gw]
        y = jnp.dot(diff.astype(BF16), w_ref[gi], preferred_element_type=F32)
        o_ref[:, gi * gw:(gi + 1) * gw] = (y * sc_ref[:, gi * gw:(gi + 1) * gw]).astype(o_ref.dtype)


def pool_mix(px, n, w_pool, scale, *, ts=512):
    r, c = px.shape
    ts = min(ts, n)
    hb = ts // POOL_HALO
    last_halo = r // POOL_HALO - 1
    blocks = 2 * _nbytes((ts, c), F32) + _nbytes(w_pool.shape, BF16)
    return pl.pallas_call(
        functools.partial(_pool_kernel, n=n, ts=ts),
        grid=(r // ts,),
        in_specs=[pl.BlockSpec((POOL_HALO, c), lambda i: (jnp.maximum(i * hb - 1, 0), 0)),
                  pl.BlockSpec((ts, c), lambda i: (i, 0)),
                  pl.BlockSpec((POOL_HALO, c), lambda i: (jnp.minimum((i + 1) * hb, last_halo), 0)),
                  pl.BlockSpec(w_pool.shape, lambda i: (0, 0, 0)),
                  pl.BlockSpec((1, c), lambda i: (0, 0))],
        out_specs=pl.BlockSpec((ts, c), lambda i: (i, 0)),
        out_shape=jax.ShapeDtypeStruct((r, c), BF16),
        compiler_params=_params(("parallel",), blocks, 6 * _nbytes((ts + 2 * POOL_HALO, c), F32)),
        name="pool_mix",
    )(px, px, px, w_pool, scale.reshape(1, c))


def _dft_chan_kernel(u_ref, c_ref, s_ref, w_ref, ap_ref, aq_ref, pq_ref, *, norm):
    gw = c_ref.shape[0]

    @pl.when(pl.program_id(0) == 0)
    def _():
        for g in range(FNET_GROUPS):
            w = w_ref[g]
            p = jnp.dot(c_ref[...], w, preferred_element_type=F32, precision=lax.Precision.HIGHEST)
            q = jnp.dot(s_ref[...], w, preferred_element_type=F32, precision=lax.Precision.HIGHEST)
            pq_ref[g] = (jnp.concatenate([p, -q], axis=-1) * norm).astype(BF16)

    for g in range(FNET_GROUPS):
        a = jnp.dot(u_ref[:, g * gw:(g + 1) * gw], pq_ref[g], preferred_element_type=F32)
        ap_ref[:, g * gw:(g + 1) * gw] = a[:, :gw].astype(BF16)
        aq_ref[:, g * gw:(g + 1) * gw] = a[:, gw:].astype(BF16)


def _dft_seq_kernel(c_ref, s_ref, ap_ref, aq_ref, o_ref, acc_ref, *, nn):
    j = pl.program_id(1)

    @pl.when(j == 0)
    def _():
        acc_ref[...] = jnp.zeros_like(acc_ref)

    for b in range(acc_ref.shape[0]):
        acc_ref[b] += (jnp.dot(c_ref[...], ap_ref[b], preferred_element_type=F32)
                       + jnp.dot(s_ref[...], aq_ref[b], preferred_element_type=F32))

    @pl.when(j == nn - 1)
    def _():
        o_ref[...] = acc_ref[...].astype(o_ref.dtype)


def _dft_tables(n):
    k = jnp.arange(n, dtype=jnp.int32)
    ang = ((k[:, None] * k[None, :]) % n).astype(F32) * (2.0 * np.pi / n)
    return jnp.cos(ang), jnp.sin(ang)


def fourier_mix(fx, nb, n, w_fnet, seq_tables, chan_tables, *, tm=512, tk=512, tn=512):
    r, c = fx.shape
    gw = c // FNET_GROUPS
    tm, tk, tn = min(tm, r), min(tk, n), min(tn, n)
    cc, sc = chan_tables
    ap, aq = pl.pallas_call(
        functools.partial(_dft_chan_kernel, norm=float(1.0 / np.sqrt(n * gw))),
        grid=(r // tm,),
        in_specs=[pl.BlockSpec((tm, c), lambda i: (i, 0)),
                  pl.BlockSpec((gw, gw), lambda i: (0, 0)),
                  pl.BlockSpec((gw, gw), lambda i: (0, 0)),
                  pl.BlockSpec(w_fnet.shape, lambda i: (0, 0, 0))],
        out_specs=[pl.BlockSpec((tm, c), lambda i: (i, 0))] * 2,
        out_shape=[jax.ShapeDtypeStruct((r, c), BF16)] * 2,
        scratch_shapes=[pltpu.VMEM((FNET_GROUPS, gw, 2 * gw), BF16)],
        compiler_params=_params(("arbitrary",), 3 * _nbytes((tm, c), BF16) + 6 * _nbytes((gw, gw), F32),
                                4 * _nbytes((tm, c), F32)),
        name="dft_channels",
    )(fx, cc, sc, w_fnet)
    cn, sn = seq_tables
    nn = n // tn
    blocks = 2 * _nbytes((tk, tn), BF16) + 2 * _nbytes((nb, tn, c), BF16) + _nbytes((nb, tk, c), BF16)
    y = pl.pallas_call(
        functools.partial(_dft_seq_kernel, nn=nn),
        grid=(n // tk, nn),
        in_specs=[pl.BlockSpec((tk, tn), lambda i, j: (i, j)),
                  pl.BlockSpec((tk, tn), lambda i, j: (i, j)),
                  pl.BlockSpec((nb, tn, c), lambda i, j: (0, j, 0)),
                  pl.BlockSpec((nb, tn, c), lambda i, j: (0, j, 0))],
        out_specs=pl.BlockSpec((nb, tk, c), lambda i, j: (0, i, 0)),
        out_shape=jax.ShapeDtypeStruct((nb, n, c), BF16),
        scratch_shapes=[pltpu.VMEM((nb, tk, c), F32)],
        compiler_params=_params(("parallel", "arbitrary"), blocks, 3 * _nbytes((nb, tk, c), F32)),
        name="dft_positions",
    )(cn, sn, ap.reshape(nb, n, c), aq.reshape(nb, n, c))
    return y.reshape(r, c)


def _attend(q, segments):
    dn = (((1,), (1,)), ((), ()))
    scores = []
    for k, _, bias in segments:
        s = lax.dot_general(q, k, dn, preferred_element_type=F32)
        scores.append(s if bias is None else s + bias)
    m = functools.reduce(jnp.maximum, [jnp.max(s, axis=-1, keepdims=True) for s in scores])
    probs = [jnp.exp2(s - m) for s in scores]
    l = sum(jnp.sum(p, axis=-1, keepdims=True) for p in probs)
    o = sum(jnp.dot(p.astype(BF16), v, preferred_element_type=F32) for p, (_, v, _) in zip(probs, segments))
    return o / l


def _dense_attn_kernel(*refs, has_lat, group, tq):
    if has_lat:
        q_ref, kl_ref, vl_ref, kc_ref, vc_ref, o_ref = refs
        segments = [(kl_ref[...], vl_ref[...], None), (kc_ref[...], vc_ref[...], None)]
    else:
        q_ref, kc_ref, vc_ref, o_ref = refs
        segments = [(kc_ref[...], vc_ref[...], None)]
    q = jnp.concatenate([q_ref[:, g * HEAD_DIM:(g + 1) * HEAD_DIM] for g in range(group)], axis=0)
    o = _attend(q, segments)
    for g in range(group):
        o_ref[:, g * HEAD_DIM:(g + 1) * HEAD_DIM] = o[g * tq:(g + 1) * tq].astype(o_ref.dtype)


def dense_attention(q, nb, nq, kv_heads, k_ctx, v_ctx, m_ctx, k_lat=None, v_lat=None, *, tq=128):
    heads = q.shape[1] // HEAD_DIM
    group = heads // kv_heads
    tq = min(tq, nq)
    nqt = nq // tq
    has_lat = k_lat is not None
    gd = group * HEAD_DIM
    in_specs = [pl.BlockSpec((tq, gd), lambda b, h, i: (b * nqt + i, h))]
    args = [q]
    blocks = 2 * _nbytes((tq, gd), BF16) + 2 * _nbytes((m_ctx, HEAD_DIM), BF16)
    keys = m_ctx
    if has_lat:
        lat = pl.BlockSpec((nq, HEAD_DIM), lambda b, h, i: (b, h))
        in_specs += [lat, lat]
        args += [k_lat, v_lat]
        blocks += 2 * _nbytes((nq, HEAD_DIM), BF16)
        keys += nq
    ctx = pl.BlockSpec((m_ctx, HEAD_DIM), lambda b, h, i: (b, h))
    in_specs += [ctx, ctx]
    args += [k_ctx, v_ctx]
    return pl.pallas_call(
        functools.partial(_dense_attn_kernel, has_lat=has_lat, group=group, tq=tq),
        grid=(nb, kv_heads, nqt),
        in_specs=in_specs,
        out_specs=pl.BlockSpec((tq, gd), lambda b, h, i: (b * nqt + i, h)),
        out_shape=jax.ShapeDtypeStruct(q.shape, BF16),
        compiler_params=_params(("parallel", "parallel", "parallel"), blocks, 3 * _nbytes((group * tq, keys), F32)),
        name="dense_attention",
    )(*args)


def _na_plan(rows, rq):
    wr = min(NA_WIN_R, rows)
    band = min(rq + wr - 1, rows)
    nblk = rows // rq
    starts = [int(np.clip(rq * j - wr // 2, 0, rows - band)) for j in range(nblk)]
    deltas = [starts[j] - rq * j for j in range(nblk)]
    uniq = sorted(set(deltas))
    var = [uniq.index(dl) for dl in deltas]
    reps = [deltas.index(dl) for dl in uniq]
    return starts, var, band, reps


def _na_bias_kernel(rpb_ref, o_ref, *, rows, rq, band, starts, reps):
    h = pl.program_id(0)
    wr = min(NA_WIN_R, rows)
    qc = lax.broadcasted_iota(jnp.int32, (GRID_W, GRID_W), 0)
    kc = lax.broadcasted_iota(jnp.int32, (GRID_W, GRID_W), 1)
    dc = kc - qc + (NA_WIN_C - 1)
    cs = jnp.clip(qc - NA_WIN_C // 2, 0, GRID_W - NA_WIN_C)
    col_ok = (kc >= cs) & (kc < cs + NA_WIN_C)
    masked = jnp.full((GRID_W, GRID_W), MASK_VALUE, F32)
    by_dr = []
    for dr in range(2 * NA_WIN_R - 1):
        t = masked
        for d in range(2 * NA_WIN_C - 1):
            t = jnp.where(dc == d, rpb_ref[h, dr, d] * LOG2E, t)
        by_dr.append(jnp.where(col_ok, t, MASK_VALUE))
    for v, j in enumerate(reps):
        for e in range(rq):
            qr = rq * j + e
            rs = int(np.clip(qr - wr // 2, 0, rows - wr))
            for a in range(band):
                kr = starts[j] + a
                blk = by_dr[kr - qr + NA_WIN_R - 1] if rs <= kr < rs + wr else masked
                o_ref[v, e * GRID_W:(e + 1) * GRID_W, a * GRID_W:(a + 1) * GRID_W] = blk


def _na_bias(rpb, rows, rq, band, starts, reps):
    nv = len(reps)
    tq, bk = rq * GRID_W, band * GRID_W
    return pl.pallas_call(
        functools.partial(_na_bias_kernel, rows=rows, rq=rq, band=band, starts=starts, reps=reps),
        grid=(NA_HEADS,),
        in_specs=[pl.BlockSpec(memory_space=pltpu.SMEM)],
        out_specs=pl.BlockSpec((nv, None, tq, bk), lambda h: (0, h, 0, 0)),
        out_shape=jax.ShapeDtypeStruct((nv, NA_HEADS, tq, bk), F32),
        compiler_params=_params(("parallel",), _nbytes((nv, tq, bk), F32)),
        name="na_bias",
    )(rpb)


def _na_kernel(start_ref, var_ref, q_ref, k_ref, v_ref, kc_ref, vc_ref, bias_ref, o_ref, *, band_keys):
    del var_ref
    j = pl.program_id(1)
    st = pl.multiple_of(start_ref[j] * GRID_W, GRID_W)
    for h in range(NA_HEADS):
        hs = slice(h * HEAD_DIM, (h + 1) * HEAD_DIM)
        segments = [(k_ref[pl.ds(st, band_keys), hs], v_ref[pl.ds(st, band_keys), hs], bias_ref[h]),
                    (kc_ref[:, hs], vc_ref[:, hs], None)]
        o_ref[:, hs] = _attend(q_ref[:, hs], segments).astype(o_ref.dtype)


def neighbourhood_attention(q, k, v, k_ctx, v_ctx, nb, n, m_ctx, rpb, *, rq=4):
    rows = n // GRID_W
    rq = min(rq, rows)
    starts, var, band, reps = _na_plan(rows, rq)
    bias = _na_bias(rpb, rows, rq, band, starts, reps)
    tq, band_keys = rq * GRID_W, band * GRID_W
    nblk = rows // rq
    c = q.shape[1]
    blocks = (_nbytes((tq, c), BF16) * 2 + 2 * _nbytes((n, c), BF16) + 2 * _nbytes((m_ctx, c), BF16)
              + _nbytes((NA_HEADS, tq, band_keys), F32))
    grid_spec = pltpu.PrefetchScalarGridSpec(
        num_scalar_prefetch=2,
        grid=(nb, nblk),
        in_specs=[pl.BlockSpec((tq, c), lambda b, j, st, vr: (b * nblk + j, 0)),
                  pl.BlockSpec((n, c), lambda b, j, st, vr: (b, 0)),
                  pl.BlockSpec((n, c), lambda b, j, st, vr: (b, 0)),
                  pl.BlockSpec((m_ctx, c), lambda b, j, st, vr: (b, 0)),
                  pl.BlockSpec((m_ctx, c), lambda b, j, st, vr: (b, 0)),
                  pl.BlockSpec((None, NA_HEADS, tq, band_keys), lambda b, j, st, vr: (vr[j], 0, 0, 0))],
        out_specs=pl.BlockSpec((tq, c), lambda b, j, st, vr: (b * nblk + j, 0)),
    )
    return pl.pallas_call(
        functools.partial(_na_kernel, band_keys=band_keys),
        grid_spec=grid_spec,
        out_shape=jax.ShapeDtypeStruct(q.shape, BF16),
        compiler_params=_params(("parallel", "arbitrary"), blocks, 6 * _nbytes((tq, band_keys + m_ctx), F32)),
        name="neighbourhood_attention",
    )(jnp.asarray(starts, jnp.int32), jnp.asarray(var, jnp.int32), q, k, v, k_ctx, v_ctx, bias)


def _outproj_kernel(x_ref, gt_ref, y0_ref, y1_ref, y2_ref, y3_ref, w_ref, o_ref):
    y = jnp.concatenate([y0_ref[...], y1_ref[...], y2_ref[...], y3_ref[...]], axis=-1)
    o_ref[...] = x_ref[...] + gt_ref[...] * jnp.dot(y, w_ref[...], preferred_element_type=F32)


def out_projection(x, mod3, k_gate, row_of_tile, ys, w_out, *, tm=512):
    r, d = x.shape
    tm = min(tm, r)
    row = lambda i: (i, 0)
    blocks = 2 * _nbytes((tm, d), F32) + _nbytes((tm, d), BF16) + _nbytes(w_out.shape, BF16) // 2
    return pl.pallas_call(
        _outproj_kernel,
        grid=(r // tm,),
        in_specs=[pl.BlockSpec((tm, d), row), _mod_spec(d, row_of_tile, k_gate)]
                 + [pl.BlockSpec((tm, MIX_GROUP), row)] * 4
                 + [pl.BlockSpec(w_out.shape, lambda i: (0, 0), pipeline_mode=pl.Buffered(1))],
        out_specs=pl.BlockSpec((tm, d), row),
        out_shape=jax.ShapeDtypeStruct((r, d), F32),
        compiler_params=_params(("parallel",), blocks, 3 * _nbytes((tm, d), F32)),
        name="out_projection",
    )(x, mod3, *ys, w_out)


def _rope_tables(n):
    t = jnp.arange(n, dtype=jnp.int32)
    row = (t // GRID_W).astype(F32)
    col = (t % GRID_W).astype(F32)
    n_freq = HEAD_DIM // 4
    inv = 1.0 / (ROPE_THETA ** (jnp.arange(n_freq, dtype=F32) / n_freq))
    ang = jnp.concatenate([row[:, None] * inv, col[:, None] * inv], axis=-1)
    cos = jnp.repeat(jnp.cos(ang), 2, axis=-1)
    sin = jnp.repeat(jnp.sin(ang), 2, axis=-1)
    sign = jnp.tile(jnp.asarray([-1.0, 1.0], F32), HEAD_DIM // 2)
    return cos, sin * sign


def kernel(x, c, ctx, c_ctx, w_mod, b_mod, ffn1_norm, ffn1_gate, ffn1_up, ffn1_down, mix_norm, w_in, w_out,
           pool_w, pool_scale, q_norm, k_norm, fnet_w, na_rpb, ffn2_norm, ffn2_gate, ffn2_up, ffn2_down, final_norm):
    nb, n, d = x.shape
    m = ctx.shape[1]
    depth = w_mod.shape[0]
    assert nb + 1 <= MOD_ROWS

    cvec = jnp.zeros((MOD_ROWS, d), F32).at[:nb].set(c).at[nb].set(c_ctx)
    mod = modulation(cvec, w_mod, b_mod)

    rope = _rope_tables(n)
    seq_tables = tuple(t.astype(BF16) for t in _dft_tables(n))
    ctx_tables = tuple(t.astype(BF16) for t in _dft_tables(m))
    chan_tables = _dft_tables(MIX_GROUP // FNET_GROUPS)

    bf = lambda w: w.astype(BF16)
    xl = x.reshape(nb * n, d)
    xc = ctx.reshape(nb * m, d)
    ctx_row = lambda i: nb

    for l in range(depth):
        last = l == depth - 1
        mod3 = mod[l].reshape(MOD_ROWS, 1, N_MOD * d)
        f1 = (bf(ffn1_gate[l]), bf(ffn1_up[l]), bf(ffn1_down[l]))
        f2 = (bf(ffn2_gate[l]), bf(ffn2_up[l]), bf(ffn2_down[l]))
        w_in_l, w_out_l, pool_w_l = bf(w_in[l]), bf(w_out[l]), bf(pool_w[l])

        def lat_row(tm):
            return lambda i: (i * tm) // n

        xl = ffn(xl, mod3, 0, lat_row(512), ffn1_norm[l], *f1)
        xc = ffn(xc, mod3, 0, ctx_row, ffn1_norm[l], *f1)

        px, gq, gk, gv, fx, nq, nk, nv = in_projection(
            xl, mod3, 3, lat_row(512), mix_norm[l], w_in_l, q_norm[l], k_norm[l], rope)
        pc, gqc, gkc, gvc, fc, nqc, nkc, nvc = in_projection(
            xc, mod3, 3, ctx_row, mix_norm[l], w_in_l, q_norm[l], k_norm[l])

        ys = (pool_mix(px, n, pool_w_l, pool_scale[l]),
              dense_attention(gq, nb, n, GQA_KV_HEADS, gkc, gvc, m, gk, gv),
              fourier_mix(fx, nb, n, fnet_w[l], seq_tables, chan_tables),
              neighbourhood_attention(nq, nk, nv, nkc, nvc, nb, n, m, na_rpb[l]))
        xl = out_projection(xl, mod3, 5, lat_row(512), ys, w_out_l)

        if not last:
            ycs = (pool_mix(pc, m, pool_w_l, pool_scale[l]),
                   dense_attention(gqc, nb, m, GQA_KV_HEADS, gkc, gvc, m),
                   fourier_mix(fc, nb, m, fnet_w[l], ctx_tables, chan_tables),
                   dense_attention(nqc, nb, m, NA_HEADS, nkc, nvc, m))
            xc = out_projection(xc, mod3, 5, ctx_row, ycs, w_out_l)
            xc = ffn(xc, mod3, 6, ctx_row, ffn2_norm[l], *f2)

        xl = ffn(xl, mod3, 6, lat_row(512), ffn2_norm[l], *f2, final_g=final_norm if last else None)
    return xl.reshape(nb, n, d)
```

```python
import functools

import numpy as np
import jax
import jax.numpy as jnp
from jax import lax
from jax.experimental import pallas as pl
from jax.experimental.pallas import tpu as pltpu

F32 = jnp.float32
BF16 = jnp.bfloat16

GRID_W = 64
HEAD_DIM = 128
POOL_WINDOWS = (2, 4, 8, 16)
POOL_HALO = 8
GQA_Q_HEADS = 4
GQA_KV_HEADS = 2
FNET_GROUPS = 4
NA_HEADS = 4
NA_WIN_R = 8
NA_WIN_C = 16
ROPE_THETA = 10000.0
IN_SPLITS = (512, 512, 256, 256, 512, 512, 512, 512)
MIX_GROUP = 512
EPS = 1e-6
N_MOD = 9
MOD_ROWS = 8
MASK_VALUE = -1e30
LOG2E = float(np.log2(np.e))

V7X_VMEM_BYTES = 64 * 1024 * 1024
VMEM_CAP = V7X_VMEM_BYTES - 4 * 1024 * 1024
FFN_TM = 1024


def _params(semantics, block_bytes, scratch_bytes=0):
    need = 2 * block_bytes + scratch_bytes
    limit = min(max(need + need // 2, 32 * 1024 * 1024), VMEM_CAP)
    return pltpu.CompilerParams(dimension_semantics=semantics, vmem_limit_bytes=limit)


def _nbytes(shape, dtype):
    return int(np.prod(shape)) * jnp.dtype(dtype).itemsize


def _norm_modulate(x, gain, shift, scale):
    inv = lax.rsqrt(jnp.mean(x * x, axis=-1, keepdims=True) + EPS)
    return (x * inv) * (gain * (1.0 + scale)) + shift


def _mod_spec(d, row_of_tile, k):
    return pl.BlockSpec((None, 1, d), lambda i, *_: (row_of_tile(i), 0, k))


def _mod_kernel(c_ref, w_ref, b_ref, o_ref):
    c = c_ref[...]
    s = c * jax.nn.sigmoid(c)
    o_ref[...] = jnp.dot(s.astype(BF16), w_ref[...].astype(BF16), preferred_element_type=F32) + b_ref[...]


def modulation(cvec, w_mod, b_mod, *, tn=512):
    nl, d, nd = w_mod.shape
    blocks = _nbytes((d, tn), F32) + _nbytes((MOD_ROWS, d), F32) + 2 * _nbytes((MOD_ROWS, tn), F32)
    return pl.pallas_call(
        _mod_kernel,
        grid=(nl, nd // tn),
        in_specs=[pl.BlockSpec((MOD_ROWS, d), lambda l, j: (0, 0)),
                  pl.BlockSpec((None, d, tn), lambda l, j: (l, 0, j)),
                  pl.BlockSpec((None, 1, tn), lambda l, j: (l, 0, j))],
        out_specs=pl.BlockSpec((None, MOD_ROWS, tn), lambda l, j: (l, 0, j)),
        out_shape=jax.ShapeDtypeStruct((nl, MOD_ROWS, nd), F32),
        compiler_params=_params(("parallel", "parallel"), blocks, _nbytes((d, tn), BF16)),
        name="modulation",
    )(cvec, w_mod, b_mod.reshape(nl, 1, nd))


def _ffn_kernel(*refs, nj, final):
    if final:
        x_ref, sh_ref, sc_ref, gt_ref, ng_ref, wg_ref, wu_ref, wd_ref, fg_ref, o_ref, h_ref = refs
    else:
        x_ref, sh_ref, sc_ref, gt_ref, ng_ref, wg_ref, wu_ref, wd_ref, o_ref, h_ref = refs
    j = pl.program_id(1)

    def partial_out():
        h = h_ref[...]
        g = jnp.dot(h, wg_ref[...], preferred_element_type=F32)
        u = jnp.dot(h, wu_ref[...], preferred_element_type=F32)
        a = (g * jax.nn.sigmoid(g) * u).astype(BF16)
        return jnp.dot(a, wd_ref[...], preferred_element_type=F32)

    @pl.when(j == 0)
    def _():
        h_ref[...] = _norm_modulate(x_ref[...], ng_ref[...], sh_ref[...], sc_ref[...]).astype(BF16)
        o_ref[...] = partial_out()

    @pl.when(j > 0)
    def _():
        o_ref[...] += partial_out()

    @pl.when(j == nj - 1)
    def _():
        r = x_ref[...] + 0.5 * gt_ref[...] * o_ref[...]
        if final:
            r = r * lax.rsqrt(jnp.mean(r * r, axis=-1, keepdims=True) + EPS) * fg_ref[...]
        o_ref[...] = r


def ffn(x, mod3, k0, row_of_tile, norm_g, layer, wg, wu, wd, final_g=None, *, tm=512, tf=512):
    r, d = x.shape
    f = wg.shape[2]
    tm = min(tm, r)
    nj = f // tf
    final = final_g is not None
    row = lambda i, j: (i, 0)
    in_specs = [pl.BlockSpec((tm, d), row),
                _mod_spec(d, row_of_tile, k0), _mod_spec(d, row_of_tile, k0 + 1), _mod_spec(d, row_of_tile, k0 + 2),
                pl.BlockSpec((1, d), lambda i, j: (0, 0)),
                pl.BlockSpec((None, d, tf), lambda i, j: (layer, 0, j)),
                pl.BlockSpec((None, d, tf), lambda i, j: (layer, 0, j)),
                pl.BlockSpec((None, tf, d), lambda i, j: (layer, j, 0))]
    args = [x, mod3, mod3, mod3, norm_g.reshape(1, d), wg, wu, wd]
    if final:
        in_specs.append(pl.BlockSpec((1, d), lambda i, j: (0, 0)))
        args.append(final_g.reshape(1, d))
    blocks = 2 * _nbytes((tm, d), F32) + 3 * _nbytes((d, tf), BF16) + 5 * _nbytes((1, d), F32)
    temps = _nbytes((tm, d), BF16) + 4 * _nbytes((tm, tf), F32) + _nbytes((tm, d), F32)
    return pl.pallas_call(
        functools.partial(_ffn_kernel, nj=nj, final=final),
        grid=(r // tm, nj),
        in_specs=in_specs,
        out_specs=pl.BlockSpec((tm, d), row),
        out_shape=jax.ShapeDtypeStruct((r, d), F32),
        scratch_shapes=[pltpu.VMEM((tm, d), BF16)],
        compiler_params=_params(("parallel", "arbitrary"), blocks, temps),
        name="ffn",
    )(*args)


def _rms_heads(z, gain, nheads):
    outs = []
    for hd in range(nheads):
        zh = z[:, hd * HEAD_DIM:(hd + 1) * HEAD_DIM]
        outs.append(zh * lax.rsqrt(jnp.mean(zh * zh, axis=-1, keepdims=True) + EPS) * gain)
    return outs


def _rope(zh, cos, sin_signed):
    lane = lax.broadcasted_iota(jnp.int32, zh.shape, 1)
    partner = jnp.where(lane % 2 == 0, pltpu.roll(zh, HEAD_DIM - 1, 1), pltpu.roll(zh, 1, 1))
    return zh * cos + partner * sin_signed


def _inproj_kernel(*refs, rope):
    if rope:
        (x_ref, sh_ref, sc_ref, ng_ref, w_ref, qn_ref, kn_ref, cos_ref, sin_ref,
         px_ref, gq_ref, gk_ref, gv_ref, fx_ref, nq_ref, nk_ref, nv_ref) = refs
    else:
        (x_ref, sh_ref, sc_ref, ng_ref, w_ref, qn_ref, kn_ref,
         px_ref, gq_ref, gk_ref, gv_ref, fx_ref, nq_ref, nk_ref, nv_ref) = refs
    h = _norm_modulate(x_ref[...], ng_ref[...], sh_ref[...], sc_ref[...]).astype(BF16)
    offs = np.cumsum((0,) + IN_SPLITS)

    def proj(s):
        return jnp.dot(h, w_ref[:, offs[s]:offs[s + 1]], preferred_element_type=F32)

    sm_scale = HEAD_DIM ** -0.5 * LOG2E
    px_ref[...] = proj(0)
    q_heads = _rms_heads(proj(1), qn_ref[...], GQA_Q_HEADS)
    k_heads = _rms_heads(proj(2), kn_ref[...], GQA_KV_HEADS)
    if rope:
        cos, sin = cos_ref[...], sin_ref[...]
        q_heads = [_rope(zh, cos, sin) for zh in q_heads]
        k_heads = [_rope(zh, cos, sin) for zh in k_heads]
    gq_ref[...] = jnp.concatenate([zh * sm_scale for zh in q_heads], axis=-1).astype(BF16)
    gk_ref[...] = jnp.concatenate(k_heads, axis=-1).astype(BF16)
    gv_ref[...] = proj(3).astype(BF16)
    fx_ref[...] = proj(4).astype(BF16)
    nq_ref[...] = (proj(5) * sm_scale).astype(BF16)
    nk_ref[...] = proj(6).astype(BF16)
    nv_ref[...] = proj(7).astype(BF16)


def in_projection(x, mod3, k0, row_of_tile, norm_g, layer, w_in, q_norm, k_norm, rope_tables=None, *, tm=512):
    r, d = x.shape
    tm = min(tm, r)
    rope = rope_tables is not None
    row = lambda i: (i, 0)
    const = lambda i: (0, 0)
    in_specs = [pl.BlockSpec((tm, d), row),
                _mod_spec(d, row_of_tile, k0), _mod_spec(d, row_of_tile, k0 + 1),
                pl.BlockSpec((1, d), const),
                pl.BlockSpec((None,) + w_in.shape[1:], lambda i: (layer, 0, 0), pipeline_mode=pl.Buffered(1)),
                pl.BlockSpec((1, HEAD_DIM), const), pl.BlockSpec((1, HEAD_DIM), const)]
    args = [x, mod3, mod3, norm_g.reshape(1, d), w_in, q_norm.reshape(1, HEAD_DIM), k_norm.reshape(1, HEAD_DIM)]
    if rope:
        tiles_per_seq = rope_tables[0].shape[0] // tm
        tab = pl.BlockSpec((tm, HEAD_DIM), lambda i: (i % tiles_per_seq, 0))
        in_specs += [tab, tab]
        args += list(rope_tables)
    widths = (512, 512, 256, 256, 512, 512, 512, 512)
    dtypes = (F32,) + (BF16,) * 7
    out_specs = [pl.BlockSpec((tm, w), row) for w in widths]
    out_shape = [jax.ShapeDtypeStruct((r, w), dt) for w, dt in zip(widths, dtypes)]
    blocks = _nbytes((tm, d), F32) + _nbytes(w_in.shape[1:], BF16) // 2 + _nbytes((tm, 4096), F32)
    temps = _nbytes((tm, d), BF16) + 2 * _nbytes((tm, d), F32) + 6 * _nbytes((tm, 512), F32)
    return pl.pallas_call(
        functools.partial(_inproj_kernel, rope=rope),
        grid=(r // tm,),
        in_specs=in_specs,
        out_specs=out_specs,
        out_shape=out_shape,
        compiler_params=_params(("parallel",), blocks, temps),
        name="in_projection",
    )(*args)


def _pool_kernel(prev_ref, cur_ref, next_ref, w_ref, sc_ref, o_ref, *, n, ts):
    tiles_per_seq = n // ts
    s = pl.program_id(0) % tiles_per_seq
    cur = cur_ref[...]
    prev = jnp.where(s == 0, 0.0, prev_ref[...])
    nxt = jnp.where(s == tiles_per_seq - 1, 0.0, next_ref[...])
    ext = jnp.concatenate([prev, cur, nxt], axis=0)
    rows = ts + 2 * POOL_HALO
    t = s * ts + lax.broadcasted_iota(jnp.int32, (ts, 1), 0)
    gw = cur.shape[1] // len(POOL_WINDOWS)
    for gi, w in enumerate(POOL_WINDOWS):
        e = ext[:, gi * gw:(gi + 1) * gw]
        width = 1
        while width < w:
            e = e + pltpu.roll(e, rows - width, 0)
            width *= 2
        lead = POOL_HALO - w // 2
        win = (pltpu.roll(e, rows - lead, 0) if lead else e)[:ts]
        lo = jnp.maximum(t - w // 2, 0)
        hi = jnp.minimum(t + w // 2 - 1, n - 1)
        diff = win / (hi - lo + 1).astype(F32) - cur[:, gi * gw:(gi + 1) * gw]
        y = jnp.dot(diff.astype(BF16), w_ref[gi], preferred_element_type=F32)
        o_ref[:, gi * gw:(gi + 1) * gw] = (y * sc_ref[:, gi * gw:(gi + 1) * gw]).astype(o_ref.dtype)


def pool_mix(px, n, w_pool, scale, *, ts=512):
    r, c = px.shape
    ts = min(ts, n)
    hb = ts // POOL_HALO
    last_halo = r // POOL_HALO - 1
    blocks = 2 * _nbytes((ts, c), F32) + _nbytes(w_pool.shape, BF16)
    return pl.pallas_call(
        functools.partial(_pool_kernel, n=n, ts=ts),
        grid=(r // ts,),
        in_specs=[pl.BlockSpec((POOL_HALO, c), lambda i: (jnp.maximum(i * hb - 1, 0), 0)),
                  pl.BlockSpec((ts, c), lambda i: (i, 0)),
                  pl.BlockSpec((POOL_HALO, c), lambda i: (jnp.minimum((i + 1) * hb, last_halo), 0)),
                  pl.BlockSpec(w_pool.shape, lambda i: (0, 0, 0)),
                  pl.BlockSpec((1, c), lambda i: (0, 0))],
        out_specs=pl.BlockSpec((ts, c), lambda i: (i, 0)),
        out_shape=jax.ShapeDtypeStruct((r, c), BF16),
        compiler_params=_params(("parallel",), blocks, 6 * _nbytes((ts + 2 * POOL_HALO, c), F32)),
        name="pool_mix",
    )(px, px, px, w_pool, scale.reshape(1, c))


def _dft_chan_kernel(u_ref, c_ref, s_ref, w_ref, ap_ref, aq_ref, pq_ref, *, norm):
    gw = c_ref.shape[0]

    @pl.when(pl.program_id(0) == 0)
    def _():
        for g in range(FNET_GROUPS):
            w = w_ref[g]
            p = jnp.dot(c_ref[...], w, preferred_element_type=F32, precision=lax.Precision.HIGHEST)
            q = jnp.dot(s_ref[...], w, preferred_element_type=F32, precision=lax.Precision.HIGHEST)
            pq_ref[g] = (jnp.concatenate([p, -q], axis=-1) * norm).astype(BF16)

    for g in range(FNET_GROUPS):
        a = jnp.dot(u_ref[:, g * gw:(g + 1) * gw], pq_ref[g], preferred_element_type=F32)
        ap_ref[:, g * gw:(g + 1) * gw] = a[:, :gw].astype(BF16)
        aq_ref[:, g * gw:(g + 1) * gw] = a[:, gw:].astype(BF16)


def _dft_seq_kernel(c_ref, s_ref, ap_ref, aq_ref, o_ref, acc_ref, *, nn):
    j = pl.program_id(1)

    @pl.when(j == 0)
    def _():
        acc_ref[...] = jnp.zeros_like(acc_ref)

    for b in range(acc_ref.shape[0]):
        acc_ref[b] += (jnp.dot(c_ref[...], ap_ref[b], preferred_element_type=F32)
                       + jnp.dot(s_ref[...], aq_ref[b], preferred_element_type=F32))

    @pl.when(j == nn - 1)
    def _():
        o_ref[...] = acc_ref[...].astype(o_ref.dtype)


def _dft_tables(n, dtype, tile=512):
    def cos_sin(cols):
        k = jnp.arange(n, dtype=jnp.int32)[:, None]
        ang = ((k * cols[None, :]) % n).astype(F32) * (2.0 * np.pi / n)
        return jnp.cos(ang), jnp.sin(ang)

    if n <= tile:
        c, s = cos_sin(jnp.arange(n, dtype=jnp.int32))
        return c.astype(dtype), s.astype(dtype)
    c1, s1 = cos_sin(jnp.arange(tile, dtype=jnp.int32))
    c0, s0 = cos_sin(jnp.arange(n // tile, dtype=jnp.int32) * tile)
    c = c0[:, :, None] * c1[:, None, :] - s0[:, :, None] * s1[:, None, :]
    s = s0[:, :, None] * c1[:, None, :] + c0[:, :, None] * s1[:, None, :]
    return c.reshape(n, n).astype(dtype), s.reshape(n, n).astype(dtype)


def fourier_mix(fx, nb, n, w_fnet, seq_tables, chan_tables, *, tm=512, tk=512, tn=512):
    r, c = fx.shape
    gw = c // FNET_GROUPS
    tm, tk, tn = min(tm, r), min(tk, n), min(tn, n)
    cc, sc = chan_tables
    ap, aq = pl.pallas_call(
        functools.partial(_dft_chan_kernel, norm=float(1.0 / np.sqrt(n * gw))),
        grid=(r // tm,),
        in_specs=[pl.BlockSpec((tm, c), lambda i: (i, 0)),
                  pl.BlockSpec((gw, gw), lambda i: (0, 0)),
                  pl.BlockSpec((gw, gw), lambda i: (0, 0)),
                  pl.BlockSpec(w_fnet.shape, lambda i: (0, 0, 0))],
        out_specs=[pl.BlockSpec((tm, c), lambda i: (i, 0))] * 2,
        out_shape=[jax.ShapeDtypeStruct((r, c), BF16)] * 2,
        scratch_shapes=[pltpu.VMEM((FNET_GROUPS, gw, 2 * gw), BF16)],
        compiler_params=_params(("arbitrary",), 3 * _nbytes((tm, c), BF16) + 6 * _nbytes((gw, gw), F32),
                                4 * _nbytes((tm, c), F32)),
        name="dft_channels",
    )(fx, cc, sc, w_fnet)
    cn, sn = seq_tables
    nn = n // tn
    blocks = 2 * _nbytes((tk, tn), BF16) + 2 * _nbytes((nb, tn, c), BF16) + _nbytes((nb, tk, c), BF16)
    y = pl.pallas_call(
        functools.partial(_dft_seq_kernel, nn=nn),
        grid=(n // tk, nn),
        in_specs=[pl.BlockSpec((tk, tn), lambda i, j: (i, j)),
                  pl.BlockSpec((tk, tn), lambda i, j: (i, j)),
                  pl.BlockSpec((nb, tn, c), lambda i, j: (0, j, 0)),
                  pl.BlockSpec((nb, tn, c), lambda i, j: (0, j, 0))],
        out_specs=pl.BlockSpec((nb, tk, c), lambda i, j: (0, i, 0)),
        out_shape=jax.ShapeDtypeStruct((nb, n, c), BF16),
        scratch_shapes=[pltpu.VMEM((nb, tk, c), F32)],
        compiler_params=_params(("parallel", "arbitrary"), blocks, 3 * _nbytes((nb, tk, c), F32)),
        name="dft_positions",
    )(cn, sn, ap.reshape(nb, n, c), aq.reshape(nb, n, c))
    return y.reshape(r, c)


def _attend(q, segments):
    dn = (((1,), (1,)), ((), ()))
    scores = []
    for k, _, bias in segments:
        s = lax.dot_general(q, k, dn, preferred_element_type=F32)
        scores.append(s if bias is None else s + bias)
    m = functools.reduce(jnp.maximum, [jnp.max(s, axis=-1, keepdims=True) for s in scores])
    probs = [jnp.exp2(s - m) for s in scores]
    l = sum(jnp.sum(p, axis=-1, keepdims=True) for p in probs)
    o = sum(jnp.dot(p.astype(BF16), v, preferred_element_type=F32) for p, (_, v, _) in zip(probs, segments))
    return o / l


def _ctx_attn_kernel(q_ref, kc_ref, vc_ref, o_ref, *, group):
    segments = [(kc_ref[...], vc_ref[...], None)]
    for g in range(group):
        hs = slice(g * HEAD_DIM, (g + 1) * HEAD_DIM)
        o_ref[:, hs] = _attend(q_ref[:, hs], segments).astype(o_ref.dtype)


def context_attention(q, nb, nq, kv_heads, k_ctx, v_ctx, m_ctx):
    group = q.shape[1] // HEAD_DIM // kv_heads
    gd = group * HEAD_DIM
    ctx = pl.BlockSpec((m_ctx, HEAD_DIM), lambda b, h: (b, h))
    blocks = 2 * _nbytes((nq, gd), BF16) + 2 * _nbytes((m_ctx, HEAD_DIM), BF16)
    return pl.pallas_call(
        functools.partial(_ctx_attn_kernel, group=group),
        grid=(nb, kv_heads),
        in_specs=[pl.BlockSpec((nq, gd), lambda b, h: (b, h)), ctx, ctx],
        out_specs=pl.BlockSpec((nq, gd), lambda b, h: (b, h)),
        out_shape=jax.ShapeDtypeStruct(q.shape, BF16),
        compiler_params=_params(("parallel", "parallel"), blocks, 6 * _nbytes((nq, m_ctx), F32)),
        name="context_attention",
    )(q, k_ctx, v_ctx)


def _dense_attn_kernel(q_ref, kl_ref, vl_ref, kc_ref, vc_ref, o_ref, *, group):
    segments = [(kl_ref[...], vl_ref[...], None), (kc_ref[...], vc_ref[...], None)]
    for g in range(group):
        hs = slice(g * HEAD_DIM, (g + 1) * HEAD_DIM)
        o_ref[:, hs] = _attend(q_ref[:, hs], segments).astype(o_ref.dtype)


def dense_attention(q, nb, nq, kv_heads, k_ctx, v_ctx, m_ctx, k_lat, v_lat, *, tq=256):
    heads = q.shape[1] // HEAD_DIM
    group = heads // kv_heads
    nqt = nq // tq
    gd = group * HEAD_DIM
    rows, keys = group * tq, nq + m_ctx
    qmap = lambda b, h, i: (b * nqt + i, h)
    lat = pl.BlockSpec((nq, HEAD_DIM), lambda b, h, i: (b, h))
    ctx = pl.BlockSpec((m_ctx, HEAD_DIM), lambda b, h, i: (b, h))
    blocks = 2 * _nbytes((tq, gd), BF16) + 2 * _nbytes((keys, HEAD_DIM), BF16)
    return pl.pallas_call(
        functools.partial(_dense_attn_kernel, group=group),
        grid=(nb, kv_heads, nqt),
        in_specs=[pl.BlockSpec((tq, gd), qmap), lat, lat, ctx, ctx],
        out_specs=pl.BlockSpec((tq, gd), qmap),
        out_shape=jax.ShapeDtypeStruct(q.shape, BF16),
        compiler_params=_params(("parallel", "parallel", "parallel"), blocks, 3 * _nbytes((rows, keys), F32)),
        name="dense_attention",
    )(q, k_lat, v_lat, k_ctx, v_ctx)


def _na_plan(rows, rq):
    wr = min(NA_WIN_R, rows)
    band = min(rq + wr - 1, rows)
    nblk = rows // rq
    starts = [int(np.clip(rq * j - wr // 2, 0, rows - band)) for j in range(nblk)]
    deltas = [starts[j] - rq * j for j in range(nblk)]
    uniq = sorted(set(deltas))
    var = [uniq.index(dl) for dl in deltas]
    reps = [deltas.index(dl) for dl in uniq]
    return starts, var, band, reps


def _na_bias_kernel(rpb_ref, o_ref, *, rows, rq, band, starts, reps):
    h = pl.program_id(0)
    wr = min(NA_WIN_R, rows)
    qc = lax.broadcasted_iota(jnp.int32, (GRID_W, GRID_W), 0)
    kc = lax.broadcasted_iota(jnp.int32, (GRID_W, GRID_W), 1)
    dc = kc - qc + (NA_WIN_C - 1)
    cs = jnp.clip(qc - NA_WIN_C // 2, 0, GRID_W - NA_WIN_C)
    col_ok = (kc >= cs) & (kc < cs + NA_WIN_C)
    masked = jnp.full((GRID_W, GRID_W), MASK_VALUE, F32)
    by_dr = []
    for dr in range(2 * NA_WIN_R - 1):
        t = masked
        for d in range(2 * NA_WIN_C - 1):
            t = jnp.where(dc == d, rpb_ref[h, dr, d] * LOG2E, t)
        by_dr.append(jnp.where(col_ok, t, MASK_VALUE))
    for v, j in enumerate(reps):
        for e in range(rq):
            qr = rq * j + e
            rs = int(np.clip(qr - wr // 2, 0, rows - wr))
            for a in range(band):
                kr = starts[j] + a
                blk = by_dr[kr - qr + NA_WIN_R - 1] if rs <= kr < rs + wr else masked
                o_ref[v, e * GRID_W:(e + 1) * GRID_W, a * GRID_W:(a + 1) * GRID_W] = blk


def _na_bias(rpb, rows, rq, band, starts, reps):
    nv = len(reps)
    tq, bk = rq * GRID_W, band * GRID_W
    return pl.pallas_call(
        functools.partial(_na_bias_kernel, rows=rows, rq=rq, band=band, starts=starts, reps=reps),
        grid=(NA_HEADS,),
        in_specs=[pl.BlockSpec(memory_space=pltpu.SMEM)],
        out_specs=pl.BlockSpec((nv, None, tq, bk), lambda h: (0, h, 0, 0)),
        out_shape=jax.ShapeDtypeStruct((nv, NA_HEADS, tq, bk), F32),
        compiler_params=_params(("parallel",), _nbytes((nv, tq, bk), F32)),
        name="na_bias",
    )(rpb)


def _na_kernel(start_ref, var_ref, q_ref, k_ref, v_ref, kc_ref, vc_ref, bias_ref, o_ref, *, band_keys):
    del var_ref
    j = pl.program_id(1)
    st = pl.multiple_of(start_ref[j] * GRID_W, GRID_W)
    for h in range(NA_HEADS):
        hs = slice(h * HEAD_DIM, (h + 1) * HEAD_DIM)
        segments = [(k_ref[pl.ds(st, band_keys), hs], v_ref[pl.ds(st, band_keys), hs], bias_ref[h]),
                    (kc_ref[:, hs], vc_ref[:, hs], None)]
        o_ref[:, hs] = _attend(q_ref[:, hs], segments).astype(o_ref.dtype)


def neighbourhood_attention(q, k, v, k_ctx, v_ctx, nb, n, m_ctx, rpb, *, rq=4):
    rows = n // GRID_W
    rq = min(rq, rows)
    starts, var, band, reps = _na_plan(rows, rq)
    bias = _na_bias(rpb, rows, rq, band, starts, reps)
    tq, band_keys = rq * GRID_W, band * GRID_W
    nblk = rows // rq
    c = q.shape[1]
    blocks = (_nbytes((tq, c), BF16) * 2 + 2 * _nbytes((n, c), BF16) + 2 * _nbytes((m_ctx, c), BF16)
              + _nbytes((NA_HEADS, tq, band_keys), F32))
    grid_spec = pltpu.PrefetchScalarGridSpec(
        num_scalar_prefetch=2,
        grid=(nb, nblk),
        in_specs=[pl.BlockSpec((tq, c), lambda b, j, st, vr: (b * nblk + j, 0)),
                  pl.BlockSpec((n, c), lambda b, j, st, vr: (b, 0)),
                  pl.BlockSpec((n, c), lambda b, j, st, vr: (b, 0)),
                  pl.BlockSpec((m_ctx, c), lambda b, j, st, vr: (b, 0)),
                  pl.BlockSpec((m_ctx, c), lambda b, j, st, vr: (b, 0)),
                  pl.BlockSpec((None, NA_HEADS, tq, band_keys), lambda b, j, st, vr: (vr[j], 0, 0, 0))],
        out_specs=pl.BlockSpec((tq, c), lambda b, j, st, vr: (b * nblk + j, 0)),
    )
    return pl.pallas_call(
        functools.partial(_na_kernel, band_keys=band_keys),
        grid_spec=grid_spec,
        out_shape=jax.ShapeDtypeStruct(q.shape, BF16),
        compiler_params=_params(("parallel", "arbitrary"), blocks, 6 * _nbytes((tq, band_keys + m_ctx), F32)),
        name="neighbourhood_attention",
    )(jnp.asarray(starts, jnp.int32), jnp.asarray(var, jnp.int32), q, k, v, k_ctx, v_ctx, bias)


def _outproj_kernel(x_ref, gt_ref, y0_ref, y1_ref, y2_ref, y3_ref, w_ref, o_ref):
    y = jnp.concatenate([y0_ref[...], y1_ref[...], y2_ref[...], y3_ref[...]], axis=-1)
    o_ref[...] = x_ref[...] + gt_ref[...] * jnp.dot(y, w_ref[...], preferred_element_type=F32)


def out_projection(x, mod3, k_gate, row_of_tile, ys, layer, w_out, *, tm=512):
    r, d = x.shape
    tm = min(tm, r)
    row = lambda i: (i, 0)
    blocks = 2 * _nbytes((tm, d), F32) + _nbytes((tm, d), BF16) + _nbytes(w_out.shape[1:], BF16) // 2
    return pl.pallas_call(
        _outproj_kernel,
        grid=(r // tm,),
        in_specs=[pl.BlockSpec((tm, d), row), _mod_spec(d, row_of_tile, k_gate)]
                 + [pl.BlockSpec((tm, MIX_GROUP), row)] * 4
                 + [pl.BlockSpec((None,) + w_out.shape[1:], lambda i: (layer, 0, 0), pipeline_mode=pl.Buffered(1))],
        out_specs=pl.BlockSpec((tm, d), row),
        out_shape=jax.ShapeDtypeStruct((r, d), F32),
        compiler_params=_params(("parallel",), blocks, 3 * _nbytes((tm, d), F32)),
        name="out_projection",
    )(x, mod3, *ys, w_out)


def _rope_tables(n):
    t = jnp.arange(n, dtype=jnp.int32)
    row = (t // GRID_W).astype(F32)
    col = (t % GRID_W).astype(F32)
    n_freq = HEAD_DIM // 4
    inv = 1.0 / (ROPE_THETA ** (jnp.arange(n_freq, dtype=F32) / n_freq))
    ang = jnp.concatenate([row[:, None] * inv, col[:, None] * inv], axis=-1)
    cos = jnp.repeat(jnp.cos(ang), 2, axis=-1)
    sin = jnp.repeat(jnp.sin(ang), 2, axis=-1)
    sign = jnp.tile(jnp.asarray([-1.0, 1.0], F32), HEAD_DIM // 2)
    return cos, sin * sign


def kernel(x, c, ctx, c_ctx, w_mod, b_mod, ffn1_norm, ffn1_gate, ffn1_up, ffn1_down, mix_norm, w_in, w_out,
           pool_w, pool_scale, q_norm, k_norm, fnet_w, na_rpb, ffn2_norm, ffn2_gate, ffn2_up, ffn2_down, final_norm):
    nb, n, d = x.shape
    m = ctx.shape[1]
    depth = w_mod.shape[0]
    assert nb + 1 <= MOD_ROWS

    cvec = jnp.zeros((MOD_ROWS, d), F32).at[:nb].set(c).at[nb].set(c_ctx)
    mod = modulation(cvec, w_mod, b_mod)

    rope = _rope_tables(n)
    seq_tables = _dft_tables(n, BF16)
    ctx_tables = _dft_tables(m, BF16)
    chan_tables = _dft_tables(MIX_GROUP // FNET_GROUPS, F32)

    bf = lambda w: w.astype(BF16)
    f1 = (bf(ffn1_gate), bf(ffn1_up), bf(ffn1_down))
    f2 = (bf(ffn2_gate), bf(ffn2_up), bf(ffn2_down))
    w_in_b, w_out_b, pool_w_b = bf(w_in), bf(w_out), bf(pool_w)
    xl = x.reshape(nb * n, d)
    xc = ctx.reshape(nb * m, d)
    ctx_row = lambda i: nb

    def lat_row(tm):
        return lambda i: (i * tm) // n

    for l in range(depth):
        last = l == depth - 1
        mod3 = mod[l].reshape(MOD_ROWS, 1, N_MOD * d)

        xl = ffn(xl, mod3, 0, lat_row(FFN_TM), ffn1_norm[l], l, *f1, tm=FFN_TM)
        xc = ffn(xc, mod3, 0, ctx_row, ffn1_norm[l], l, *f1)

        px, gq, gk, gv, fx, nq, nk, nv = in_projection(
            xl, mod3, 3, lat_row(512), mix_norm[l], l, w_in_b, q_norm[l], k_norm[l], rope)
        pc, gqc, gkc, gvc, fc, nqc, nkc, nvc = in_projection(
            xc, mod3, 3, ctx_row, mix_norm[l], l, w_in_b, q_norm[l], k_norm[l])

        ys = (pool_mix(px, n, pool_w_b[l], pool_scale[l]),
              dense_attention(gq, nb, n, GQA_KV_HEADS, gkc, gvc, m, gk, gv),
              fourier_mix(fx, nb, n, fnet_w[l], seq_tables, chan_tables),
              neighbourhood_attention(nq, nk, nv, nkc, nvc, nb, n, m, na_rpb[l]))
        xl = out_projection(xl, mod3, 5, lat_row(512), ys, l, w_out_b)

        if not last:
            ycs = (pool_mix(pc, m, pool_w_b[l], pool_scale[l]),
                   context_attention(gqc, nb, m, GQA_KV_HEADS, gkc, gvc, m),
                   fourier_mix(fc, nb, m, fnet_w[l], ctx_tables, chan_tables),
                   context_attention(nqc, nb, m, NA_HEADS, nkc, nvc, m))
            xc = out_projection(xc, mod3, 5, ctx_row, ycs, l, w_out_b)
            xc = ffn(xc, mod3, 6, ctx_row, ffn2_norm[l], l, *f2)

        xl = ffn(xl, mod3, 6, lat_row(FFN_TM), ffn2_norm[l], l, *f2, final_g=final_norm if last else None, tm=FFN_TM)
    return xl.reshape(nb, n, d)
```

```python
import functools

import numpy as np
import jax
import jax.numpy as jnp
from jax import lax
from jax.experimental import pallas as pl
from jax.experimental.pallas import tpu as pltpu

F32 = jnp.float32
BF16 = jnp.bfloat16

GRID_W = 64
HEAD_DIM = 128
POOL_WINDOWS = (2, 4, 8, 16)
POOL_HALO = 8
GQA_Q_HEADS = 4
GQA_KV_HEADS = 2
FNET_GROUPS = 4
NA_HEADS = 4
NA_WIN_R = 8
NA_WIN_C = 16
ROPE_THETA = 10000.0
IN_SPLITS = (512, 512, 256, 256, 512, 512, 512, 512)
MIX_GROUP = 512
EPS = 1e-6
N_MOD = 9
MOD_ROWS = 8
MASK_VALUE = -1e30
LOG2E = float(np.log2(np.e))

V7X_VMEM_BYTES = 64 * 1024 * 1024
VMEM_CAP = V7X_VMEM_BYTES - 4 * 1024 * 1024
FFN_TM = 1024
CAST_VIEW_ROWS = 2048
BF16_SUBLANES = 16


def _params(semantics, block_bytes, scratch_bytes=0):
    need = 2 * block_bytes + scratch_bytes
    limit = min(max(need + need // 2, 32 * 1024 * 1024), VMEM_CAP)
    return pltpu.CompilerParams(dimension_semantics=semantics, vmem_limit_bytes=limit)


def _nbytes(shape, dtype):
    return int(np.prod(shape)) * jnp.dtype(dtype).itemsize


def _norm_modulate(x, gain, shift, scale):
    inv = lax.rsqrt(jnp.mean(x * x, axis=-1, keepdims=True) + EPS)
    return (x * inv) * (gain * (1.0 + scale)) + shift


def _mod_spec(d, row_of_tile, k):
    return pl.BlockSpec((None, 1, d), lambda i, *_: (row_of_tile(i), 0, k))


def _mod_kernel(c_ref, w_ref, b_ref, o_ref):
    c = c_ref[...]
    s = c * jax.nn.sigmoid(c)
    o_ref[...] = jnp.dot(s.astype(BF16), w_ref[...].astype(BF16), preferred_element_type=F32) + b_ref[...]


def modulation(cvec, w_mod, b_mod, *, tn=512):
    nl, d, nd = w_mod.shape
    blocks = _nbytes((d, tn), F32) + _nbytes((MOD_ROWS, d), F32) + 2 * _nbytes((MOD_ROWS, tn), F32)
    return pl.pallas_call(
        _mod_kernel,
        grid=(nl, nd // tn),
        in_specs=[pl.BlockSpec((MOD_ROWS, d), lambda l, j: (0, 0)),
                  pl.BlockSpec((None, d, tn), lambda l, j: (l, 0, j)),
                  pl.BlockSpec((None, 1, tn), lambda l, j: (l, 0, j))],
        out_specs=pl.BlockSpec((None, MOD_ROWS, tn), lambda l, j: (l, 0, j)),
        out_shape=jax.ShapeDtypeStruct((nl, MOD_ROWS, nd), F32),
        compiler_params=_params(("parallel", "parallel"), blocks, _nbytes((d, tn), BF16)),
        name="modulation",
    )(cvec, w_mod, b_mod.reshape(nl, 1, nd))


def _ffn_kernel(*refs, nj, final):
    if final:
        x_ref, sh_ref, sc_ref, gt_ref, ng_ref, wg_ref, wu_ref, wd_ref, fg_ref, o_ref, h_ref = refs
    else:
        x_ref, sh_ref, sc_ref, gt_ref, ng_ref, wg_ref, wu_ref, wd_ref, o_ref, h_ref = refs
    j = pl.program_id(1)

    def partial_out():
        h = h_ref[...]
        g = jnp.dot(h, wg_ref[...], preferred_element_type=F32)
        u = jnp.dot(h, wu_ref[...], preferred_element_type=F32)
        a = (g * jax.nn.sigmoid(g) * u).astype(BF16)
        return jnp.dot(a, wd_ref[...], preferred_element_type=F32)

    @pl.when(j == 0)
    def _():
        h_ref[...] = _norm_modulate(x_ref[...], ng_ref[...], sh_ref[...], sc_ref[...]).astype(BF16)
        o_ref[...] = partial_out()

    @pl.when(j > 0)
    def _():
        o_ref[...] += partial_out()

    @pl.when(j == nj - 1)
    def _():
        r = x_ref[...] + 0.5 * gt_ref[...] * o_ref[...]
        if final:
            r = r * lax.rsqrt(jnp.mean(r * r, axis=-1, keepdims=True) + EPS) * fg_ref[...]
        o_ref[...] = r


def ffn(x, mod3, k0, row_of_tile, norm_g, layer, wg, wu, wd, final_g=None, *, tm=512, tf=512):
    r, d = x.shape
    f = wg.shape[2]
    tm = min(tm, r)
    nj = f // tf
    final = final_g is not None
    row = lambda i, j: (i, 0)
    in_specs = [pl.BlockSpec((tm, d), row),
                _mod_spec(d, row_of_tile, k0), _mod_spec(d, row_of_tile, k0 + 1), _mod_spec(d, row_of_tile, k0 + 2),
                pl.BlockSpec((1, d), lambda i, j: (0, 0)),
                pl.BlockSpec((None, d, tf), lambda i, j: (layer, 0, j)),
                pl.BlockSpec((None, d, tf), lambda i, j: (layer, 0, j)),
                pl.BlockSpec((None, tf, d), lambda i, j: (layer, j, 0))]
    args = [x, mod3, mod3, mod3, norm_g.reshape(1, d), wg, wu, wd]
    if final:
        in_specs.append(pl.BlockSpec((1, d), lambda i, j: (0, 0)))
        args.append(final_g.reshape(1, d))
    blocks = 2 * _nbytes((tm, d), F32) + 3 * _nbytes((d, tf), BF16) + 5 * _nbytes((1, d), F32)
    temps = _nbytes((tm, d), BF16) + 4 * _nbytes((tm, tf), F32) + _nbytes((tm, d), F32)
    return pl.pallas_call(
        functools.partial(_ffn_kernel, nj=nj, final=final),
        grid=(r // tm, nj),
        in_specs=in_specs,
        out_specs=pl.BlockSpec((tm, d), row),
        out_shape=jax.ShapeDtypeStruct((r, d), F32),
        scratch_shapes=[pltpu.VMEM((tm, d), BF16)],
        compiler_params=_params(("parallel", "arbitrary"), blocks, temps),
        name="ffn",
    )(*args)


def _rms_heads(z, gain, nheads):
    outs = []
    for hd in range(nheads):
        zh = z[:, hd * HEAD_DIM:(hd + 1) * HEAD_DIM]
        outs.append(zh * lax.rsqrt(jnp.mean(zh * zh, axis=-1, keepdims=True) + EPS) * gain)
    return outs


def _rope(zh, cos, sin_signed):
    lane = lax.broadcasted_iota(jnp.int32, zh.shape, 1)
    partner = jnp.where(lane % 2 == 0, pltpu.roll(zh, HEAD_DIM - 1, 1), pltpu.roll(zh, 1, 1))
    return zh * cos + partner * sin_signed


def _inproj_kernel(*refs, rope):
    if rope:
        (x_ref, sh_ref, sc_ref, ng_ref, w_ref, qn_ref, kn_ref, cos_ref, sin_ref,
         px_ref, gq_ref, gk_ref, gv_ref, fx_ref, nq_ref, nk_ref, nv_ref) = refs
    else:
        (x_ref, sh_ref, sc_ref, ng_ref, w_ref, qn_ref, kn_ref,
         px_ref, gq_ref, gk_ref, gv_ref, fx_ref, nq_ref, nk_ref, nv_ref) = refs
    h = _norm_modulate(x_ref[...], ng_ref[...], sh_ref[...], sc_ref[...]).astype(BF16)
    offs = np.cumsum((0,) + IN_SPLITS)

    def proj(s):
        return jnp.dot(h, w_ref[:, offs[s]:offs[s + 1]], preferred_element_type=F32)

    sm_scale = HEAD_DIM ** -0.5 * LOG2E
    px_ref[...] = proj(0)
    q_heads = _rms_heads(proj(1), qn_ref[...], GQA_Q_HEADS)
    k_heads = _rms_heads(proj(2), kn_ref[...], GQA_KV_HEADS)
    if rope:
        cos, sin = cos_ref[...], sin_ref[...]
        q_heads = [_rope(zh, cos, sin) for zh in q_heads]
        k_heads = [_rope(zh, cos, sin) for zh in k_heads]
    gq_ref[...] = jnp.concatenate([zh * sm_scale for zh in q_heads], axis=-1).astype(BF16)
    gk_ref[...] = jnp.concatenate(k_heads, axis=-1).astype(BF16)
    gv_ref[...] = proj(3).astype(BF16)
    fx_ref[...] = proj(4).astype(BF16)
    nq_ref[...] = (proj(5) * sm_scale).astype(BF16)
    nk_ref[...] = proj(6).astype(BF16)
    nv_ref[...] = proj(7).astype(BF16)


def in_projection(x, mod3, k0, row_of_tile, norm_g, layer, w_in, q_norm, k_norm, rope_tables=None, *, tm=512):
    r, d = x.shape
    tm = min(tm, r)
    rope = rope_tables is not None
    row = lambda i: (i, 0)
    const = lambda i: (0, 0)
    in_specs = [pl.BlockSpec((tm, d), row),
                _mod_spec(d, row_of_tile, k0), _mod_spec(d, row_of_tile, k0 + 1),
                pl.BlockSpec((1, d), const),
                pl.BlockSpec((None,) + w_in.shape[1:], lambda i: (layer, 0, 0), pipeline_mode=pl.Buffered(1)),
                pl.BlockSpec((1, HEAD_DIM), const), pl.BlockSpec((1, HEAD_DIM), const)]
    args = [x, mod3, mod3, norm_g.reshape(1, d), w_in, q_norm.reshape(1, HEAD_DIM), k_norm.reshape(1, HEAD_DIM)]
    if rope:
        tiles_per_seq = rope_tables[0].shape[0] // tm
        tab = pl.BlockSpec((tm, HEAD_DIM), lambda i: (i % tiles_per_seq, 0))
        in_specs += [tab, tab]
        args += list(rope_tables)
    widths = (512, 512, 256, 256, 512, 512, 512, 512)
    dtypes = (F32,) + (BF16,) * 7
    out_specs = [pl.BlockSpec((tm, w), row) for w in widths]
    out_shape = [jax.ShapeDtypeStruct((r, w), dt) for w, dt in zip(widths, dtypes)]
    blocks = _nbytes((tm, d), F32) + _nbytes(w_in.shape[1:], BF16) // 2 + _nbytes((tm, 4096), F32)
    temps = _nbytes((tm, d), BF16) + 2 * _nbytes((tm, d), F32) + 6 * _nbytes((tm, 512), F32)
    return pl.pallas_call(
        functools.partial(_inproj_kernel, rope=rope),
        grid=(r // tm,),
        in_specs=in_specs,
        out_specs=out_specs,
        out_shape=out_shape,
        compiler_params=_params(("parallel",), blocks, temps),
        name="in_projection",
    )(*args)


def _pool_kernel(prev_ref, cur_ref, next_ref, w_ref, sc_ref, o_ref, *, n, ts):
    tiles_per_seq = n // ts
    s = pl.program_id(0) % tiles_per_seq
    cur = cur_ref[...]
    prev = jnp.where(s == 0, 0.0, prev_ref[...])
    nxt = jnp.where(s == tiles_per_seq - 1, 0.0, next_ref[...])
    ext = jnp.concatenate([prev, cur, nxt], axis=0)
    rows = ts + 2 * POOL_HALO
    t = s * ts + lax.broadcasted_iota(jnp.int32, (ts, 1), 0)
    gw = cur.shape[1] // len(POOL_WINDOWS)
    for gi, w in enumerate(POOL_WINDOWS):
        e = ext[:, gi * gw:(gi + 1) * gw]
        width = 1
        while width < w:
            e = e + pltpu.roll(e, rows - width, 0)
            width *= 2
        lead = POOL_HALO - w // 2
        win = (pltpu.roll(e, rows - lead, 0) if lead else e)[:ts]
        lo = jnp.maximum(t - w // 2, 0)
        hi = jnp.minimum(t + w // 2 - 1, n - 1)
        diff = win / (hi - lo + 1).astype(F32) - cur[:, gi * gw:(gi + 1) * gw]
        y = jnp.dot(diff.astype(BF16), w_ref[gi], preferred_element_type=F32)
        o_ref[:, gi * gw:(gi + 1) * gw] = (y * sc_ref[:, gi * gw:(gi + 1) * gw]).astype(o_ref.dtype)


def pool_mix(px, n, w_pool, scale, *, ts=512):
    r, c = px.shape
    ts = min(ts, n)
    hb = ts // POOL_HALO
    last_halo = r // POOL_HALO - 1
    blocks = 2 * _nbytes((ts, c), F32) + _nbytes(w_pool.shape, BF16)
    return pl.pallas_call(
        functools.partial(_pool_kernel, n=n, ts=ts),
        grid=(r // ts,),
        in_specs=[pl.BlockSpec((POOL_HALO, c), lambda i: (jnp.maximum(i * hb - 1, 0), 0)),
                  pl.BlockSpec((ts, c), lambda i: (i, 0)),
                  pl.BlockSpec((POOL_HALO, c), lambda i: (jnp.minimum((i + 1) * hb, last_halo), 0)),
                  pl.BlockSpec(w_pool.shape, lambda i: (0, 0, 0)),
                  pl.BlockSpec((1, c), lambda i: (0, 0))],
        out_specs=pl.BlockSpec((ts, c), lambda i: (i, 0)),
        out_shape=jax.ShapeDtypeStruct((r, c), BF16),
        compiler_params=_params(("parallel",), blocks, 6 * _nbytes((ts + 2 * POOL_HALO, c), F32)),
        name="pool_mix",
    )(px, px, px, w_pool, scale.reshape(1, c))


def _dft_chan_kernel(u_ref, c_ref, s_ref, w_ref, ap_ref, aq_ref, pq_ref, *, norm):
    gw = c_ref.shape[0]

    @pl.when(pl.program_id(0) == 0)
    def _():
        for g in range(FNET_GROUPS):
            w = w_ref[g]
            p = jnp.dot(c_ref[...], w, preferred_element_type=F32, precision=lax.Precision.HIGHEST)
            q = jnp.dot(s_ref[...], w, preferred_element_type=F32, precision=lax.Precision.HIGHEST)
            pq_ref[g] = (jnp.concatenate([p, -q], axis=-1) * norm).astype(BF16)

    for g in range(FNET_GROUPS):
        a = jnp.dot(u_ref[:, g * gw:(g + 1) * gw], pq_ref[g], preferred_element_type=F32)
        ap_ref[:, g * gw:(g + 1) * gw] = a[:, :gw].astype(BF16)
        aq_ref[:, g * gw:(g + 1) * gw] = a[:, gw:].astype(BF16)


def _dft_seq_kernel(c_ref, s_ref, ap_ref, aq_ref, o_ref, acc_ref, *, nn):
    j = pl.program_id(1)

    @pl.when(j == 0)
    def _():
        acc_ref[...] = jnp.zeros_like(acc_ref)

    for b in range(acc_ref.shape[0]):
        acc_ref[b] += (jnp.dot(c_ref[...], ap_ref[b], preferred_element_type=F32)
                       + jnp.dot(s_ref[...], aq_ref[b], preferred_element_type=F32))

    @pl.when(j == nn - 1)
    def _():
        o_ref[...] = acc_ref[...].astype(o_ref.dtype)


def _dft_tables(n, dtype, tile=512):
    def cos_sin(cols):
        k = jnp.arange(n, dtype=jnp.int32)[:, None]
        ang = ((k * cols[None, :]) % n).astype(F32) * (2.0 * np.pi / n)
        return jnp.cos(ang), jnp.sin(ang)

    if n <= tile:
        c, s = cos_sin(jnp.arange(n, dtype=jnp.int32))
        return c.astype(dtype), s.astype(dtype)
    c1, s1 = cos_sin(jnp.arange(tile, dtype=jnp.int32))
    c0, s0 = cos_sin(jnp.arange(n // tile, dtype=jnp.int32) * tile)
    c = c0[:, :, None] * c1[:, None, :] - s0[:, :, None] * s1[:, None, :]
    s = s0[:, :, None] * c1[:, None, :] + c0[:, :, None] * s1[:, None, :]
    return c.reshape(n, n).astype(dtype), s.reshape(n, n).astype(dtype)


def fourier_mix(fx, nb, n, w_fnet, seq_tables, chan_tables, *, tm=512, tk=512, tn=512):
    r, c = fx.shape
    gw = c // FNET_GROUPS
    tm, tk, tn = min(tm, r), min(tk, n), min(tn, n)
    cc, sc = chan_tables
    ap, aq = pl.pallas_call(
        functools.partial(_dft_chan_kernel, norm=float(1.0 / np.sqrt(n * gw))),
        grid=(r // tm,),
        in_specs=[pl.BlockSpec((tm, c), lambda i: (i, 0)),
                  pl.BlockSpec((gw, gw), lambda i: (0, 0)),
                  pl.BlockSpec((gw, gw), lambda i: (0, 0)),
                  pl.BlockSpec(w_fnet.shape, lambda i: (0, 0, 0))],
        out_specs=[pl.BlockSpec((tm, c), lambda i: (i, 0))] * 2,
        out_shape=[jax.ShapeDtypeStruct((r, c), BF16)] * 2,
        scratch_shapes=[pltpu.VMEM((FNET_GROUPS, gw, 2 * gw), BF16)],
        compiler_params=_params(("arbitrary",), 3 * _nbytes((tm, c), BF16) + 6 * _nbytes((gw, gw), F32),
                                4 * _nbytes((tm, c), F32)),
        name="dft_channels",
    )(fx, cc, sc, w_fnet)
    cn, sn = seq_tables
    nn = n // tn
    blocks = 2 * _nbytes((tk, tn), BF16) + 2 * _nbytes((nb, tn, c), BF16) + _nbytes((nb, tk, c), BF16)
    y = pl.pallas_call(
        functools.partial(_dft_seq_kernel, nn=nn),
        grid=(n // tk, nn),
        in_specs=[pl.BlockSpec((tk, tn), lambda i, j: (i, j)),
                  pl.BlockSpec((tk, tn), lambda i, j: (i, j)),
                  pl.BlockSpec((nb, tn, c), lambda i, j: (0, j, 0)),
                  pl.BlockSpec((nb, tn, c), lambda i, j: (0, j, 0))],
        out_specs=pl.BlockSpec((nb, tk, c), lambda i, j: (0, i, 0)),
        out_shape=jax.ShapeDtypeStruct((nb, n, c), BF16),
        scratch_shapes=[pltpu.VMEM((nb, tk, c), F32)],
        compiler_params=_params(("parallel", "arbitrary"), blocks, 3 * _nbytes((nb, tk, c), F32)),
        name="dft_positions",
    )(cn, sn, ap.reshape(nb, n, c), aq.reshape(nb, n, c))
    return y.reshape(r, c)


def _attend(q, segments):
    dn = (((1,), (1,)), ((), ()))
    scores = []
    for k, _, bias in segments:
        s = lax.dot_general(q, k, dn, preferred_element_type=F32)
        scores.append(s if bias is None else s + bias)
    m = functools.reduce(jnp.maximum, [jnp.max(s, axis=-1, keepdims=True) for s in scores])
    probs = [jnp.exp2(s - m) for s in scores]
    l = sum(jnp.sum(p, axis=-1, keepdims=True) for p in probs)
    o = sum(jnp.dot(p.astype(BF16), v, preferred_element_type=F32) for p, (_, v, _) in zip(probs, segments))
    return o / l


def _ctx_attn_kernel(q_ref, kc_ref, vc_ref, o_ref, *, group):
    segments = [(kc_ref[...], vc_ref[...], None)]
    for g in range(group):
        hs = slice(g * HEAD_DIM, (g + 1) * HEAD_DIM)
        o_ref[:, hs] = _attend(q_ref[:, hs], segments).astype(o_ref.dtype)


def context_attention(q, nb, nq, kv_heads, k_ctx, v_ctx, m_ctx):
    group = q.shape[1] // HEAD_DIM // kv_heads
    gd = group * HEAD_DIM
    ctx = pl.BlockSpec((m_ctx, HEAD_DIM), lambda b, h: (b, h))
    blocks = 2 * _nbytes((nq, gd), BF16) + 2 * _nbytes((m_ctx, HEAD_DIM), BF16)
    return pl.pallas_call(
        functools.partial(_ctx_attn_kernel, group=group),
        grid=(nb, kv_heads),
        in_specs=[pl.BlockSpec((nq, gd), lambda b, h: (b, h)), ctx, ctx],
        out_specs=pl.BlockSpec((nq, gd), lambda b, h: (b, h)),
        out_shape=jax.ShapeDtypeStruct(q.shape, BF16),
        compiler_params=_params(("parallel", "parallel"), blocks, 6 * _nbytes((nq, m_ctx), F32)),
        name="context_attention",
    )(q, k_ctx, v_ctx)


def _dense_attn_kernel(*refs, group, n_casts):
    q_ref, kl_ref, vl_ref, kc_ref, vc_ref = refs[:5]
    cast_src = refs[5:5 + n_casts]
    o_ref = refs[5 + n_casts]
    cast_dst = refs[6 + n_casts:]
    segments = [(kl_ref[...], vl_ref[...], None), (kc_ref[...], vc_ref[...], None)]
    for g in range(group):
        hs = slice(g * HEAD_DIM, (g + 1) * HEAD_DIM)
        o_ref[:, hs] = _attend(q_ref[:, hs], segments).astype(o_ref.dtype)
    for src, dst in zip(cast_src, cast_dst):
        dst[...] = src[...].astype(dst.dtype)


def dense_attention(q, nb, nq, kv_heads, k_ctx, v_ctx, m_ctx, k_lat, v_lat, casts=(), *, tq=256):
    heads = q.shape[1] // HEAD_DIM
    group = heads // kv_heads
    nqt = nq // tq
    gd = group * HEAD_DIM
    rows, keys = group * tq, nq + m_ctx
    qmap = lambda b, h, i: (b * nqt + i, h)
    lat = pl.BlockSpec((nq, HEAD_DIM), lambda b, h, i: (b, h))
    ctx = pl.BlockSpec((m_ctx, HEAD_DIM), lambda b, h, i: (b, h))
    blocks = 2 * _nbytes((tq, gd), BF16) + 2 * _nbytes((keys, HEAD_DIM), BF16)
    steps = nb * kv_heads * nqt
    slab = CAST_VIEW_ROWS // steps
    assert slab * steps == CAST_VIEW_ROWS and slab % BF16_SUBLANES == 0
    cast_in, cast_specs_in, cast_specs_out, cast_shapes = [], [], [], []
    for w, layer in casts:
        cols = w.shape[1] * w.shape[2] // CAST_VIEW_ROWS
        cast_in.append(w.reshape(w.shape[0] * CAST_VIEW_ROWS, cols))
        cast_specs_in.append(pl.BlockSpec(
            (slab, cols), lambda b, h, i, layer=layer: (layer * steps + (b * kv_heads + h) * nqt + i, 0)))
        cast_specs_out.append(pl.BlockSpec((slab, cols), lambda b, h, i: ((b * kv_heads + h) * nqt + i, 0)))
        cast_shapes.append(jax.ShapeDtypeStruct((CAST_VIEW_ROWS, cols), BF16))
        blocks += _nbytes((slab, cols), F32) + _nbytes((slab, cols), BF16)
    outs = pl.pallas_call(
        functools.partial(_dense_attn_kernel, group=group, n_casts=len(casts)),
        grid=(nb, kv_heads, nqt),
        in_specs=[pl.BlockSpec((tq, gd), qmap), lat, lat, ctx, ctx] + cast_specs_in,
        out_specs=[pl.BlockSpec((tq, gd), qmap)] + cast_specs_out,
        out_shape=[jax.ShapeDtypeStruct(q.shape, BF16)] + cast_shapes,
        compiler_params=_params(("parallel", "parallel", "parallel"), blocks, 3 * _nbytes((rows, keys), F32)),
        name="dense_attention",
    )(q, k_lat, v_lat, k_ctx, v_ctx, *cast_in)
    return outs[0], [o.reshape((1,) + w.shape[1:]) for o, (w, _) in zip(outs[1:], casts)]


def _na_plan(rows, rq):
    wr = min(NA_WIN_R, rows)
    band = min(rq + wr - 1, rows)
    nblk = rows // rq
    starts = [int(np.clip(rq * j - wr // 2, 0, rows - band)) for j in range(nblk)]
    deltas = [starts[j] - rq * j for j in range(nblk)]
    uniq = sorted(set(deltas))
    var = [uniq.index(dl) for dl in deltas]
    reps = [deltas.index(dl) for dl in uniq]
    return starts, var, band, reps


def _na_bias_kernel(rpb_ref, o_ref, *, rows, rq, band, starts, reps):
    h = pl.program_id(0)
    wr = min(NA_WIN_R, rows)
    qc = lax.broadcasted_iota(jnp.int32, (GRID_W, GRID_W), 0)
    kc = lax.broadcasted_iota(jnp.int32, (GRID_W, GRID_W), 1)
    dc = kc - qc + (NA_WIN_C - 1)
    cs = jnp.clip(qc - NA_WIN_C // 2, 0, GRID_W - NA_WIN_C)
    col_ok = (kc >= cs) & (kc < cs + NA_WIN_C)
    masked = jnp.full((GRID_W, GRID_W), MASK_VALUE, F32)
    by_dr = []
    for dr in range(2 * NA_WIN_R - 1):
        t = masked
        for d in range(2 * NA_WIN_C - 1):
            t = jnp.where(dc == d, rpb_ref[h, dr, d] * LOG2E, t)
        by_dr.append(jnp.where(col_ok, t, MASK_VALUE))
    for v, j in enumerate(reps):
        for e in range(rq):
            qr = rq * j + e
            rs = int(np.clip(qr - wr // 2, 0, rows - wr))
            for a in range(band):
                kr = starts[j] + a
                blk = by_dr[kr - qr + NA_WIN_R - 1] if rs <= kr < rs + wr else masked
                o_ref[v, e * GRID_W:(e + 1) * GRID_W, a * GRID_W:(a + 1) * GRID_W] = blk


def _na_bias(rpb, rows, rq, band, starts, reps):
    nv = len(reps)
    tq, bk = rq * GRID_W, band * GRID_W
    return pl.pallas_call(
        functools.partial(_na_bias_kernel, rows=rows, rq=rq, band=band, starts=starts, reps=reps),
        grid=(NA_HEADS,),
        in_specs=[pl.BlockSpec(memory_space=pltpu.SMEM)],
        out_specs=pl.BlockSpec((nv, None, tq, bk), lambda h: (0, h, 0, 0)),
        out_shape=jax.ShapeDtypeStruct((nv, NA_HEADS, tq, bk), F32),
        compiler_params=_params(("parallel",), _nbytes((nv, tq, bk), F32)),
        name="na_bias",
    )(rpb)


def _na_kernel(start_ref, var_ref, q_ref, k_ref, v_ref, kc_ref, vc_ref, bias_ref, o_ref, *, band_keys):
    del var_ref
    j = pl.program_id(1)
    st = pl.multiple_of(start_ref[j] * GRID_W, GRID_W)
    for h in range(NA_HEADS):
        hs = slice(h * HEAD_DIM, (h + 1) * HEAD_DIM)
        segments = [(k_ref[pl.ds(st, band_keys), hs], v_ref[pl.ds(st, band_keys), hs], bias_ref[h]),
                    (kc_ref[:, hs], vc_ref[:, hs], None)]
        o_ref[:, hs] = _attend(q_ref[:, hs], segments).astype(o_ref.dtype)


def neighbourhood_attention(q, k, v, k_ctx, v_ctx, nb, n, m_ctx, rpb, *, rq=4):
    rows = n // GRID_W
    rq = min(rq, rows)
    starts, var, band, reps = _na_plan(rows, rq)
    bias = _na_bias(rpb, rows, rq, band, starts, reps)
    tq, band_keys = rq * GRID_W, band * GRID_W
    nblk = rows // rq
    c = q.shape[1]
    blocks = (_nbytes((tq, c), BF16) * 2 + 2 * _nbytes((n, c), BF16) + 2 * _nbytes((m_ctx, c), BF16)
              + _nbytes((NA_HEADS, tq, band_keys), F32))
    grid_spec = pltpu.PrefetchScalarGridSpec(
        num_scalar_prefetch=2,
        grid=(nb, nblk),
        in_specs=[pl.BlockSpec((tq, c), lambda b, j, st, vr: (b * nblk + j, 0)),
                  pl.BlockSpec((n, c), lambda b, j, st, vr: (b, 0)),
                  pl.BlockSpec((n, c), lambda b, j, st, vr: (b, 0)),
                  pl.BlockSpec((m_ctx, c), lambda b, j, st, vr: (b, 0)),
                  pl.BlockSpec((m_ctx, c), lambda b, j, st, vr: (b, 0)),
                  pl.BlockSpec((None, NA_HEADS, tq, band_keys), lambda b, j, st, vr: (vr[j], 0, 0, 0))],
        out_specs=pl.BlockSpec((tq, c), lambda b, j, st, vr: (b * nblk + j, 0)),
    )
    return pl.pallas_call(
        functools.partial(_na_kernel, band_keys=band_keys),
        grid_spec=grid_spec,
        out_shape=jax.ShapeDtypeStruct(q.shape, BF16),
        compiler_params=_params(("parallel", "arbitrary"), blocks, 6 * _nbytes((tq, band_keys + m_ctx), F32)),
        name="neighbourhood_attention",
    )(jnp.asarray(starts, jnp.int32), jnp.asarray(var, jnp.int32), q, k, v, k_ctx, v_ctx, bias)


def _outproj_kernel(x_ref, gt_ref, y0_ref, y1_ref, y2_ref, y3_ref, w_ref, o_ref):
    y = jnp.concatenate([y0_ref[...], y1_ref[...], y2_ref[...], y3_ref[...]], axis=-1)
    o_ref[...] = x_ref[...] + gt_ref[...] * jnp.dot(y, w_ref[...], preferred_element_type=F32)


def out_projection(x, mod3, k_gate, row_of_tile, ys, layer, w_out, *, tm=512):
    r, d = x.shape
    tm = min(tm, r)
    row = lambda i: (i, 0)
    blocks = 2 * _nbytes((tm, d), F32) + _nbytes((tm, d), BF16) + _nbytes(w_out.shape[1:], BF16) // 2
    return pl.pallas_call(
        _outproj_kernel,
        grid=(r // tm,),
        in_specs=[pl.BlockSpec((tm, d), row), _mod_spec(d, row_of_tile, k_gate)]
                 + [pl.BlockSpec((tm, MIX_GROUP), row)] * 4
                 + [pl.BlockSpec((None,) + w_out.shape[1:], lambda i: (layer, 0, 0), pipeline_mode=pl.Buffered(1))],
        out_specs=pl.BlockSpec((tm, d), row),
        out_shape=jax.ShapeDtypeStruct((r, d), F32),
        compiler_params=_params(("parallel",), blocks, 3 * _nbytes((tm, d), F32)),
        name="out_projection",
    )(x, mod3, *ys, w_out)


def _rope_tables(n):
    t = jnp.arange(n, dtype=jnp.int32)
    row = (t // GRID_W).astype(F32)
    col = (t % GRID_W).astype(F32)
    n_freq = HEAD_DIM // 4
    inv = 1.0 / (ROPE_THETA ** (jnp.arange(n_freq, dtype=F32) / n_freq))
    ang = jnp.concatenate([row[:, None] * inv, col[:, None] * inv], axis=-1)
    cos = jnp.repeat(jnp.cos(ang), 2, axis=-1)
    sin = jnp.repeat(jnp.sin(ang), 2, axis=-1)
    sign = jnp.tile(jnp.asarray([-1.0, 1.0], F32), HEAD_DIM // 2)
    return cos, sin * sign


def kernel(x, c, ctx, c_ctx, w_mod, b_mod, ffn1_norm, ffn1_gate, ffn1_up, ffn1_down, mix_norm, w_in, w_out,
           pool_w, pool_scale, q_norm, k_norm, fnet_w, na_rpb, ffn2_norm, ffn2_gate, ffn2_up, ffn2_down, final_norm):
    nb, n, d = x.shape
    m = ctx.shape[1]
    depth = w_mod.shape[0]
    assert nb + 1 <= MOD_ROWS

    cvec = jnp.zeros((MOD_ROWS, d), F32).at[:nb].set(c).at[nb].set(c_ctx)
    mod = modulation(cvec, w_mod, b_mod)

    rope = _rope_tables(n)
    seq_tables = _dft_tables(n, BF16)
    ctx_tables = _dft_tables(m, BF16)
    chan_tables = _dft_tables(MIX_GROUP // FNET_GROUPS, F32)

    first_layer = lambda w: w[:1].astype(BF16)
    f1 = tuple(first_layer(w) for w in (ffn1_gate, ffn1_up, ffn1_down))
    w_in_b = first_layer(w_in)
    pool_w_b = pool_w.astype(BF16)
    xl = x.reshape(nb * n, d)
    xc = ctx.reshape(nb * m, d)
    ctx_row = lambda i: nb

    def lat_row(tm):
        return lambda i: (i * tm) // n

    for l in range(depth):
        last = l == depth - 1
        mod3 = mod[l].reshape(MOD_ROWS, 1, N_MOD * d)

        xl = ffn(xl, mod3, 0, lat_row(FFN_TM), ffn1_norm[l], 0, *f1, tm=FFN_TM)
        xc = ffn(xc, mod3, 0, ctx_row, ffn1_norm[l], 0, *f1, tm=FFN_TM)

        px, gq, gk, gv, fx, nq, nk, nv = in_projection(
            xl, mod3, 3, lat_row(512), mix_norm[l], 0, w_in_b, q_norm[l], k_norm[l], rope)
        pc, gqc, gkc, gvc, fc, nqc, nkc, nvc = in_projection(
            xc, mod3, 3, ctx_row, mix_norm[l], 0, w_in_b, q_norm[l], k_norm[l])

        casts = [(w, l) for w in (w_out, ffn2_gate, ffn2_up, ffn2_down)]
        if not last:
            casts += [(w, l + 1) for w in (ffn1_gate, ffn1_up, ffn1_down, w_in)]
        y_gqa, cast_w = dense_attention(gq, nb, n, GQA_KV_HEADS, gkc, gvc, m, gk, gv, casts)
        w_out_b, f2 = cast_w[0], cast_w[1:4]
        if not last:
            f1, w_in_b = cast_w[4:7], cast_w[7]

        ys = (pool_mix(px, n, pool_w_b[l], pool_scale[l]),
              y_gqa,
              fourier_mix(fx, nb, n, fnet_w[l], seq_tables, chan_tables),
              neighbourhood_attention(nq, nk, nv, nkc, nvc, nb, n, m, na_rpb[l]))
        xl = out_projection(xl, mod3, 5, lat_row(512), ys, 0, w_out_b)

        if not last:
            ycs = (pool_mix(pc, m, pool_w_b[l], pool_scale[l]),
                   context_attention(gqc, nb, m, GQA_KV_HEADS, gkc, gvc, m),
                   fourier_mix(fc, nb, m, fnet_w[l], ctx_tables, chan_tables),
                   context_attention(nqc, nb, m, NA_HEADS, nkc, nvc, m))
            xc = out_projection(xc, mod3, 5, ctx_row, ycs, 0, w_out_b)
            xc = ffn(xc, mod3, 6, ctx_row, ffn2_norm[l], 0, *f2, tm=FFN_TM)

        xl = ffn(xl, mod3, 6, lat_row(FFN_TM), ffn2_norm[l], 0, *f2, final_g=final_norm if last else None, tm=FFN_TM)
    return xl.reshape(nb, n, d)
```

```python
import functools

import numpy as np
import jax
import jax.numpy as jnp
from jax import lax
from jax.experimental import pallas as pl
from jax.experimental.pallas import tpu as pltpu

F32 = jnp.float32
BF16 = jnp.bfloat16

GRID_W = 64
HEAD_DIM = 128
POOL_WINDOWS = (2, 4, 8, 16)
POOL_HALO = 8
GQA_Q_HEADS = 4
GQA_KV_HEADS = 2
FNET_GROUPS = 4
NA_HEADS = 4
NA_WIN_R = 8
NA_WIN_C = 16
ROPE_THETA = 10000.0
IN_SPLITS = (512, 512, 256, 256, 512, 512, 512, 512)
MIX_GROUP = 512
EPS = 1e-6
N_MOD = 9
MOD_ROWS = 8
MASK_VALUE = -1e30
LOG2E = float(np.log2(np.e))

V7X_VMEM_BYTES = 64 * 1024 * 1024
VMEM_CAP = V7X_VMEM_BYTES - 4 * 1024 * 1024
FFN_TM = 1024
BF16_SUBLANES = 16
LANES = 128


def _params(semantics, block_bytes, scratch_bytes=0):
    need = 2 * block_bytes + scratch_bytes
    limit = min(max(need + need // 2, 32 * 1024 * 1024), VMEM_CAP)
    return pltpu.CompilerParams(dimension_semantics=semantics, vmem_limit_bytes=limit)


def _nbytes(shape, dtype):
    return int(np.prod(shape)) * jnp.dtype(dtype).itemsize


def _norm_modulate(x, gain, shift, scale):
    inv = lax.rsqrt(jnp.mean(x * x, axis=-1, keepdims=True) + EPS)
    return (x * inv) * (gain * (1.0 + scale)) + shift


def _mod_spec(d, row_of_tile, k):
    return pl.BlockSpec((None, 1, d), lambda i, *_: (row_of_tile(i), 0, k))


def _mod_kernel(c_ref, w_ref, b_ref, o_ref):
    c = c_ref[...]
    s = c * jax.nn.sigmoid(c)
    o_ref[...] = jnp.dot(s.astype(BF16), w_ref[...].astype(BF16), preferred_element_type=F32) + b_ref[...]


def modulation(cvec, w_mod, b_mod, *, tn=512):
    nl, d, nd = w_mod.shape
    blocks = _nbytes((d, tn), F32) + _nbytes((MOD_ROWS, d), F32) + 2 * _nbytes((MOD_ROWS, tn), F32)
    return pl.pallas_call(
        _mod_kernel,
        grid=(nl, nd // tn),
        in_specs=[pl.BlockSpec((MOD_ROWS, d), lambda l, j: (0, 0)),
                  pl.BlockSpec((None, d, tn), lambda l, j: (l, 0, j)),
                  pl.BlockSpec((None, 1, tn), lambda l, j: (l, 0, j))],
        out_specs=pl.BlockSpec((None, MOD_ROWS, tn), lambda l, j: (l, 0, j)),
        out_shape=jax.ShapeDtypeStruct((nl, MOD_ROWS, nd), F32),
        compiler_params=_params(("parallel", "parallel"), blocks, _nbytes((d, tn), BF16)),
        name="modulation",
    )(cvec, w_mod, b_mod.reshape(nl, 1, nd))


def _cast_kernel(src_ref, dst_ref):
    dst_ref[...] = src_ref[...].astype(dst_ref.dtype)


def cast_layer(w, layer, *, rows=256):
    _, a, b = w.shape
    rows = min(rows, a)
    return pl.pallas_call(
        _cast_kernel,
        grid=(a // rows,),
        in_specs=[pl.BlockSpec((None, rows, b), lambda i: (layer, i, 0))],
        out_specs=pl.BlockSpec((None, rows, b), lambda i: (0, i, 0)),
        out_shape=jax.ShapeDtypeStruct((1, a, b), BF16),
        compiler_params=_params(("parallel",), _nbytes((rows, b), F32) + _nbytes((rows, b), BF16)),
        name="cast_layer",
    )(w)


def _ffn_kernel(*refs, nj, final):
    if final:
        x_ref, sh_ref, sc_ref, gt_ref, ng_ref, wg_ref, wu_ref, wd_ref, fg_ref, o_ref, h_ref = refs
    else:
        x_ref, sh_ref, sc_ref, gt_ref, ng_ref, wg_ref, wu_ref, wd_ref, o_ref, h_ref = refs
    j = pl.program_id(1)

    def partial_out():
        h = h_ref[...]
        g = jnp.dot(h, wg_ref[...], preferred_element_type=F32)
        u = jnp.dot(h, wu_ref[...], preferred_element_type=F32)
        a = (g * jax.nn.sigmoid(g) * u).astype(BF16)
        return jnp.dot(a, wd_ref[...], preferred_element_type=F32)

    @pl.when(j == 0)
    def _():
        h_ref[...] = _norm_modulate(x_ref[...], ng_ref[...], sh_ref[...], sc_ref[...]).astype(BF16)
        o_ref[...] = partial_out()

    @pl.when(j > 0)
    def _():
        o_ref[...] += partial_out()

    @pl.when(j == nj - 1)
    def _():
        r = x_ref[...] + 0.5 * gt_ref[...] * o_ref[...]
        if final:
            r = r * lax.rsqrt(jnp.mean(r * r, axis=-1, keepdims=True) + EPS) * fg_ref[...]
        o_ref[...] = r


def ffn(x, mod3, k0, row_of_tile, norm_g, layer, wg, wu, wd, final_g=None, *, tm=512, tf=512):
    r, d = x.shape
    f = wg.shape[2]
    tm = min(tm, r)
    nj = f // tf
    final = final_g is not None
    row = lambda i, j: (i, 0)
    in_specs = [pl.BlockSpec((tm, d), row),
                _mod_spec(d, row_of_tile, k0), _mod_spec(d, row_of_tile, k0 + 1), _mod_spec(d, row_of_tile, k0 + 2),
                pl.BlockSpec((1, d), lambda i, j: (0, 0)),
                pl.BlockSpec((None, d, tf), lambda i, j: (layer, 0, j)),
                pl.BlockSpec((None, d, tf), lambda i, j: (layer, 0, j)),
                pl.BlockSpec((None, tf, d), lambda i, j: (layer, j, 0))]
    args = [x, mod3, mod3, mod3, norm_g.reshape(1, d), wg, wu, wd]
    if final:
        in_specs.append(pl.BlockSpec((1, d), lambda i, j: (0, 0)))
        args.append(final_g.reshape(1, d))
    blocks = 2 * _nbytes((tm, d), F32) + 3 * _nbytes((d, tf), BF16) + 5 * _nbytes((1, d), F32)
    temps = _nbytes((tm, d), BF16) + 4 * _nbytes((tm, tf), F32) + _nbytes((tm, d), F32)
    return pl.pallas_call(
        functools.partial(_ffn_kernel, nj=nj, final=final),
        grid=(r // tm, nj),
        in_specs=in_specs,
        out_specs=pl.BlockSpec((tm, d), row),
        out_shape=jax.ShapeDtypeStruct((r, d), F32),
        scratch_shapes=[pltpu.VMEM((tm, d), BF16)],
        compiler_params=_params(("parallel", "arbitrary"), blocks, temps),
        name="ffn",
    )(*args)


def _rms_heads(z, gain, nheads):
    outs = []
    for hd in range(nheads):
        zh = z[:, hd * HEAD_DIM:(hd + 1) * HEAD_DIM]
        outs.append(zh * lax.rsqrt(jnp.mean(zh * zh, axis=-1, keepdims=True) + EPS) * gain)
    return outs


def _rope(zh, cos, sin_signed):
    lane = lax.broadcasted_iota(jnp.int32, zh.shape, 1)
    partner = jnp.where(lane % 2 == 0, pltpu.roll(zh, HEAD_DIM - 1, 1), pltpu.roll(zh, 1, 1))
    return zh * cos + partner * sin_signed


def _inproj_kernel(*refs, rope):
    if rope:
        (x_ref, sh_ref, sc_ref, ng_ref, w_ref, qn_ref, kn_ref, cos_ref, sin_ref,
         px_ref, gq_ref, gk_ref, gv_ref, fx_ref, nq_ref, nk_ref, nv_ref) = refs
    else:
        (x_ref, sh_ref, sc_ref, ng_ref, w_ref, qn_ref, kn_ref,
         px_ref, gq_ref, gk_ref, gv_ref, fx_ref, nq_ref, nk_ref, nv_ref) = refs
    h = _norm_modulate(x_ref[...], ng_ref[...], sh_ref[...], sc_ref[...]).astype(BF16)
    offs = np.cumsum((0,) + IN_SPLITS)

    def proj(s):
        return jnp.dot(h, w_ref[:, offs[s]:offs[s + 1]], preferred_element_type=F32)

    sm_scale = HEAD_DIM ** -0.5 * LOG2E
    px_ref[...] = proj(0)
    q_heads = _rms_heads(proj(1), qn_ref[...], GQA_Q_HEADS)
    k_heads = _rms_heads(proj(2), kn_ref[...], GQA_KV_HEADS)
    if rope:
        cos, sin = cos_ref[...], sin_ref[...]
        q_heads = [_rope(zh, cos, sin) for zh in q_heads]
        k_heads = [_rope(zh, cos, sin) for zh in k_heads]
    gq_ref[...] = jnp.concatenate([zh * sm_scale for zh in q_heads], axis=-1).astype(BF16)
    gk_ref[...] = jnp.concatenate(k_heads, axis=-1).astype(BF16)
    gv_ref[...] = proj(3).astype(BF16)
    fx_ref[...] = proj(4).astype(BF16)
    nq_ref[...] = (proj(5) * sm_scale).astype(BF16)
    nk_ref[...] = proj(6).astype(BF16)
    nv_ref[...] = proj(7).astype(BF16)


def in_projection(x, mod3, k0, row_of_tile, norm_g, layer, w_in, q_norm, k_norm, rope_tables=None, *, tm=512):
    r, d = x.shape
    tm = min(tm, r)
    rope = rope_tables is not None
    row = lambda i: (i, 0)
    const = lambda i: (0, 0)
    in_specs = [pl.BlockSpec((tm, d), row),
                _mod_spec(d, row_of_tile, k0), _mod_spec(d, row_of_tile, k0 + 1),
                pl.BlockSpec((1, d), const),
                pl.BlockSpec((None,) + w_in.shape[1:], lambda i: (layer, 0, 0), pipeline_mode=pl.Buffered(1)),
                pl.BlockSpec((1, HEAD_DIM), const), pl.BlockSpec((1, HEAD_DIM), const)]
    args = [x, mod3, mod3, norm_g.reshape(1, d), w_in, q_norm.reshape(1, HEAD_DIM), k_norm.reshape(1, HEAD_DIM)]
    if rope:
        tiles_per_seq = rope_tables[0].shape[0] // tm
        tab = pl.BlockSpec((tm, HEAD_DIM), lambda i: (i % tiles_per_seq, 0))
        in_specs += [tab, tab]
        args += list(rope_tables)
    widths = (512, 512, 256, 256, 512, 512, 512, 512)
    dtypes = (F32,) + (BF16,) * 7
    out_specs = [pl.BlockSpec((tm, w), row) for w in widths]
    out_shape = [jax.ShapeDtypeStruct((r, w), dt) for w, dt in zip(widths, dtypes)]
    blocks = _nbytes((tm, d), F32) + _nbytes(w_in.shape[1:], BF16) // 2 + _nbytes((tm, 4096), F32)
    temps = _nbytes((tm, d), BF16) + 2 * _nbytes((tm, d), F32) + 6 * _nbytes((tm, 512), F32)
    return pl.pallas_call(
        functools.partial(_inproj_kernel, rope=rope),
        grid=(r // tm,),
        in_specs=in_specs,
        out_specs=out_specs,
        out_shape=out_shape,
        compiler_params=_params(("parallel",), blocks, temps),
        name="in_projection",
    )(*args)


def _pool_kernel(prev_ref, cur_ref, next_ref, w_ref, sc_ref, o_ref, *, n, ts):
    tiles_per_seq = n // ts
    s = pl.program_id(0) % tiles_per_seq
    cur = cur_ref[...]
    prev = jnp.where(s == 0, 0.0, prev_ref[...])
    nxt = jnp.where(s == tiles_per_seq - 1, 0.0, next_ref[...])
    ext = jnp.concatenate([prev, cur, nxt], axis=0)
    rows = ts + 2 * POOL_HALO
    t = s * ts + lax.broadcasted_iota(jnp.int32, (ts, 1), 0)
    gw = cur.shape[1] // len(POOL_WINDOWS)
    for gi, w in enumerate(POOL_WINDOWS):
        e = ext[:, gi * gw:(gi + 1) * gw]
        width = 1
        while width < w:
            e = e + pltpu.roll(e, rows - width, 0)
            width *= 2
        lead = POOL_HALO - w // 2
        win = (pltpu.roll(e, rows - lead, 0) if lead else e)[:ts]
        lo = jnp.maximum(t - w // 2, 0)
        hi = jnp.minimum(t + w // 2 - 1, n - 1)
        diff = win / (hi - lo + 1).astype(F32) - cur[:, gi * gw:(gi + 1) * gw]
        y = jnp.dot(diff.astype(BF16), w_ref[gi], preferred_element_type=F32)
        o_ref[:, gi * gw:(gi + 1) * gw] = (y * sc_ref[:, gi * gw:(gi + 1) * gw]).astype(o_ref.dtype)


def pool_mix(px, n, w_pool, scale, *, ts=512):
    r, c = px.shape
    ts = min(ts, n)
    hb = ts // POOL_HALO
    last_halo = r // POOL_HALO - 1
    blocks = 2 * _nbytes((ts, c), F32) + _nbytes(w_pool.shape, BF16)
    return pl.pallas_call(
        functools.partial(_pool_kernel, n=n, ts=ts),
        grid=(r // ts,),
        in_specs=[pl.BlockSpec((POOL_HALO, c), lambda i: (jnp.maximum(i * hb - 1, 0), 0)),
                  pl.BlockSpec((ts, c), lambda i: (i, 0)),
                  pl.BlockSpec((POOL_HALO, c), lambda i: (jnp.minimum((i + 1) * hb, last_halo), 0)),
                  pl.BlockSpec(w_pool.shape, lambda i: (0, 0, 0)),
                  pl.BlockSpec((1, c), lambda i: (0, 0))],
        out_specs=pl.BlockSpec((ts, c), lambda i: (i, 0)),
        out_shape=jax.ShapeDtypeStruct((r, c), BF16),
        compiler_params=_params(("parallel",), blocks, 6 * _nbytes((ts + 2 * POOL_HALO, c), F32)),
        name="pool_mix",
    )(px, px, px, w_pool, scale.reshape(1, c))


def _dft_chan_kernel(u_ref, c_ref, s_ref, w_ref, ap_ref, aq_ref, pq_ref, *, norm):
    gw = c_ref.shape[0]

    @pl.when(pl.program_id(0) == 0)
    def _():
        for g in range(FNET_GROUPS):
            w = w_ref[g]
            p = jnp.dot(c_ref[...], w, preferred_element_type=F32, precision=lax.Precision.HIGHEST)
            q = jnp.dot(s_ref[...], w, preferred_element_type=F32, precision=lax.Precision.HIGHEST)
            pq_ref[g] = (jnp.concatenate([p, -q], axis=-1) * norm).astype(BF16)

    for g in range(FNET_GROUPS):
        a = jnp.dot(u_ref[:, g * gw:(g + 1) * gw], pq_ref[g], preferred_element_type=F32)
        ap_ref[:, g * gw:(g + 1) * gw] = a[:, :gw].astype(BF16)
        aq_ref[:, g * gw:(g + 1) * gw] = a[:, gw:].astype(BF16)


def _dft_seq_kernel(c_ref, s_ref, ap_ref, aq_ref, o_ref, acc_ref, *, nn):
    j = pl.program_id(1)

    @pl.when(j == 0)
    def _():
        acc_ref[...] = jnp.zeros_like(acc_ref)

    for b in range(acc_ref.shape[0]):
        acc_ref[b] += (jnp.dot(c_ref[...], ap_ref[b], preferred_element_type=F32)
                       + jnp.dot(s_ref[...], aq_ref[b], preferred_element_type=F32))

    @pl.when(j == nn - 1)
    def _():
        o_ref[...] = acc_ref[...].astype(o_ref.dtype)


def _dft_tables(n, dtype, tile=512):
    def cos_sin(cols):
        k = jnp.arange(n, dtype=jnp.int32)[:, None]
        ang = ((k * cols[None, :]) % n).astype(F32) * (2.0 * np.pi / n)
        return jnp.cos(ang), jnp.sin(ang)

    if n <= tile:
        c, s = cos_sin(jnp.arange(n, dtype=jnp.int32))
        return c.astype(dtype), s.astype(dtype)
    c1, s1 = cos_sin(jnp.arange(tile, dtype=jnp.int32))
    c0, s0 = cos_sin(jnp.arange(n // tile, dtype=jnp.int32) * tile)
    c = c0[:, :, None] * c1[:, None, :] - s0[:, :, None] * s1[:, None, :]
    s = s0[:, :, None] * c1[:, None, :] + c0[:, :, None] * s1[:, None, :]
    return c.reshape(n, n).astype(dtype), s.reshape(n, n).astype(dtype)


def fourier_mix(fx, nb, n, w_fnet, seq_tables, chan_tables, *, tm=512, tk=512, tn=512):
    r, c = fx.shape
    gw = c // FNET_GROUPS
    tm, tk, tn = min(tm, r), min(tk, n), min(tn, n)
    cc, sc = chan_tables
    ap, aq = pl.pallas_call(
        functools.partial(_dft_chan_kernel, norm=float(1.0 / np.sqrt(n * gw))),
        grid=(r // tm,),
        in_specs=[pl.BlockSpec((tm, c), lambda i: (i, 0)),
                  pl.BlockSpec((gw, gw), lambda i: (0, 0)),
                  pl.BlockSpec((gw, gw), lambda i: (0, 0)),
                  pl.BlockSpec(w_fnet.shape, lambda i: (0, 0, 0))],
        out_specs=[pl.BlockSpec((tm, c), lambda i: (i, 0))] * 2,
        out_shape=[jax.ShapeDtypeStruct((r, c), BF16)] * 2,
        scratch_shapes=[pltpu.VMEM((FNET_GROUPS, gw, 2 * gw), BF16)],
        compiler_params=_params(("arbitrary",), 3 * _nbytes((tm, c), BF16) + 6 * _nbytes((gw, gw), F32),
                                4 * _nbytes((tm, c), F32)),
        name="dft_channels",
    )(fx, cc, sc, w_fnet)
    cn, sn = seq_tables
    nn = n // tn
    blocks = 2 * _nbytes((tk, tn), BF16) + 2 * _nbytes((nb, tn, c), BF16) + _nbytes((nb, tk, c), BF16)
    y = pl.pallas_call(
        functools.partial(_dft_seq_kernel, nn=nn),
        grid=(n // tk, nn),
        in_specs=[pl.BlockSpec((tk, tn), lambda i, j: (i, j)),
                  pl.BlockSpec((tk, tn), lambda i, j: (i, j)),
                  pl.BlockSpec((nb, tn, c), lambda i, j: (0, j, 0)),
                  pl.BlockSpec((nb, tn, c), lambda i, j: (0, j, 0))],
        out_specs=pl.BlockSpec((nb, tk, c), lambda i, j: (0, i, 0)),
        out_shape=jax.ShapeDtypeStruct((nb, n, c), BF16),
        scratch_shapes=[pltpu.VMEM((nb, tk, c), F32)],
        compiler_params=_params(("parallel", "arbitrary"), blocks, 3 * _nbytes((nb, tk, c), F32)),
        name="dft_positions",
    )(cn, sn, ap.reshape(nb, n, c), aq.reshape(nb, n, c))
    return y.reshape(r, c)


def _attend(q, segments):
    dn = (((1,), (1,)), ((), ()))
    scores = []
    for k, _, bias in segments:
        s = lax.dot_general(q, k, dn, preferred_element_type=F32)
        scores.append(s if bias is None else s + bias)
    m = functools.reduce(jnp.maximum, [jnp.max(s, axis=-1, keepdims=True) for s in scores])
    probs = [jnp.exp2(s - m) for s in scores]
    l = sum(jnp.sum(p, axis=-1, keepdims=True) for p in probs)
    o = sum(jnp.dot(p.astype(BF16), v, preferred_element_type=F32) for p, (_, v, _) in zip(probs, segments))
    return o / l


def _ctx_attn_kernel(q_ref, kc_ref, vc_ref, o_ref, *, group):
    segments = [(kc_ref[...], vc_ref[...], None)]
    for g in range(group):
        hs = slice(g * HEAD_DIM, (g + 1) * HEAD_DIM)
        o_ref[:, hs] = _attend(q_ref[:, hs], segments).astype(o_ref.dtype)


def context_attention(q, nb, nq, kv_heads, k_ctx, v_ctx, m_ctx):
    group = q.shape[1] // HEAD_DIM // kv_heads
    gd = group * HEAD_DIM
    ctx = pl.BlockSpec((m_ctx, HEAD_DIM), lambda b, h: (b, h))
    blocks = 2 * _nbytes((nq, gd), BF16) + 2 * _nbytes((m_ctx, HEAD_DIM), BF16)
    return pl.pallas_call(
        functools.partial(_ctx_attn_kernel, group=group),
        grid=(nb, kv_heads),
        in_specs=[pl.BlockSpec((nq, gd), lambda b, h: (b, h)), ctx, ctx],
        out_specs=pl.BlockSpec((nq, gd), lambda b, h: (b, h)),
        out_shape=jax.ShapeDtypeStruct(q.shape, BF16),
        compiler_params=_params(("parallel", "parallel"), blocks, 6 * _nbytes((nq, m_ctx), F32)),
        name="context_attention",
    )(q, k_ctx, v_ctx)


def _dense_attn_kernel(*refs, group, n_casts):
    q_ref, kl_ref, vl_ref, kc_ref, vc_ref = refs[:5]
    cast_src = refs[5:5 + n_casts]
    o_ref = refs[5 + n_casts]
    cast_dst = refs[6 + n_casts:]
    segments = [(kl_ref[...], vl_ref[...], None), (kc_ref[...], vc_ref[...], None)]
    for g in range(group):
        hs = slice(g * HEAD_DIM, (g + 1) * HEAD_DIM)
        o_ref[:, hs] = _attend(q_ref[:, hs], segments).astype(o_ref.dtype)
    for src, dst in zip(cast_src, cast_dst):
        dst[...] = src[...].astype(dst.dtype)


def dense_attention(q, nb, nq, kv_heads, k_ctx, v_ctx, m_ctx, k_lat, v_lat, casts=(), *, tq=256):
    heads = q.shape[1] // HEAD_DIM
    group = heads // kv_heads
    nqt = nq // tq
    gd = group * HEAD_DIM
    rows, keys = group * tq, nq + m_ctx
    qmap = lambda b, h, i: (b * nqt + i, h)
    lat = pl.BlockSpec((nq, HEAD_DIM), lambda b, h, i: (b, h))
    ctx = pl.BlockSpec((m_ctx, HEAD_DIM), lambda b, h, i: (b, h))
    blocks = 2 * _nbytes((tq, gd), BF16) + 2 * _nbytes((keys, HEAD_DIM), BF16)
    steps = nb * kv_heads * nqt
    cast_specs_in, cast_specs_out, cast_shapes = [], [], []
    for w, layer in casts:
        _, rows_w, cols_w = w.shape
        col_blocks = next(c for c in (1, 2, 4, 8) if steps % c == 0 and rows_w % (steps // c) == 0
                          and (rows_w // (steps // c)) % BF16_SUBLANES == 0 and cols_w % (c * LANES) == 0)
        slab = (rows_w // (steps // col_blocks), cols_w // col_blocks)

        def slab_index(b, h, i, col_blocks=col_blocks):
            step = (b * kv_heads + h) * nqt + i
            return step // col_blocks, step % col_blocks

        cast_specs_in.append(pl.BlockSpec((None,) + slab, lambda b, h, i, f=slab_index, layer=layer: (layer,) + f(b, h, i)))
        cast_specs_out.append(pl.BlockSpec((None,) + slab, lambda b, h, i, f=slab_index: (0,) + f(b, h, i)))
        cast_shapes.append(jax.ShapeDtypeStruct((1, rows_w, cols_w), BF16))
        blocks += _nbytes(slab, F32) + _nbytes(slab, BF16)
    outs = pl.pallas_call(
        functools.partial(_dense_attn_kernel, group=group, n_casts=len(casts)),
        grid=(nb, kv_heads, nqt),
        in_specs=[pl.BlockSpec((tq, gd), qmap), lat, lat, ctx, ctx] + cast_specs_in,
        out_specs=[pl.BlockSpec((tq, gd), qmap)] + cast_specs_out,
        out_shape=[jax.ShapeDtypeStruct(q.shape, BF16)] + cast_shapes,
        compiler_params=_params(("parallel", "parallel", "parallel"), blocks, 3 * _nbytes((rows, keys), F32)),
        name="dense_attention",
    )(q, k_lat, v_lat, k_ctx, v_ctx, *[w for w, _ in casts])
    return outs[0], list(outs[1:])


def _na_plan(rows, rq):
    wr = min(NA_WIN_R, rows)
    band = min(rq + wr - 1, rows)
    nblk = rows // rq
    starts = [int(np.clip(rq * j - wr // 2, 0, rows - band)) for j in range(nblk)]
    deltas = [starts[j] - rq * j for j in range(nblk)]
    uniq = sorted(set(deltas))
    var = [uniq.index(dl) for dl in deltas]
    reps = [deltas.index(dl) for dl in uniq]
    return starts, var, band, reps


def _na_bias_kernel(rpb_ref, o_ref, *, rows, rq, band, starts, reps):
    h = pl.program_id(0)
    wr = min(NA_WIN_R, rows)
    qc = lax.broadcasted_iota(jnp.int32, (GRID_W, GRID_W), 0)
    kc = lax.broadcasted_iota(jnp.int32, (GRID_W, GRID_W), 1)
    dc = kc - qc + (NA_WIN_C - 1)
    cs = jnp.clip(qc - NA_WIN_C // 2, 0, GRID_W - NA_WIN_C)
    col_ok = (kc >= cs) & (kc < cs + NA_WIN_C)
    masked = jnp.full((GRID_W, GRID_W), MASK_VALUE, F32)
    by_dr = []
    for dr in range(2 * NA_WIN_R - 1):
        t = masked
        for d in range(2 * NA_WIN_C - 1):
            t = jnp.where(dc == d, rpb_ref[h, dr, d] * LOG2E, t)
        by_dr.append(jnp.where(col_ok, t, MASK_VALUE))
    for v, j in enumerate(reps):
        for e in range(rq):
            qr = rq * j + e
            rs = int(np.clip(qr - wr // 2, 0, rows - wr))
            for a in range(band):
                kr = starts[j] + a
                blk = by_dr[kr - qr + NA_WIN_R - 1] if rs <= kr < rs + wr else masked
                o_ref[v, e * GRID_W:(e + 1) * GRID_W, a * GRID_W:(a + 1) * GRID_W] = blk


def _na_bias(rpb, rows, rq, band, starts, reps):
    nv = len(reps)
    tq, bk = rq * GRID_W, band * GRID_W
    return pl.pallas_call(
        functools.partial(_na_bias_kernel, rows=rows, rq=rq, band=band, starts=starts, reps=reps),
        grid=(NA_HEADS,),
        in_specs=[pl.BlockSpec(memory_space=pltpu.SMEM)],
        out_specs=pl.BlockSpec((nv, None, tq, bk), lambda h: (0, h, 0, 0)),
        out_shape=jax.ShapeDtypeStruct((nv, NA_HEADS, tq, bk), F32),
        compiler_params=_params(("parallel",), _nbytes((nv, tq, bk), F32)),
        name="na_bias",
    )(rpb)


def _na_kernel(start_ref, var_ref, q_ref, k_ref, v_ref, kc_ref, vc_ref, bias_ref, o_ref, *, band_keys):
    del var_ref
    j = pl.program_id(1)
    st = pl.multiple_of(start_ref[j] * GRID_W, GRID_W)
    for h in range(NA_HEADS):
        hs = slice(h * HEAD_DIM, (h + 1) * HEAD_DIM)
        segments = [(k_ref[pl.ds(st, band_keys), hs], v_ref[pl.ds(st, band_keys), hs], bias_ref[h]),
                    (kc_ref[:, hs], vc_ref[:, hs], None)]
        o_ref[:, hs] = _attend(q_ref[:, hs], segments).astype(o_ref.dtype)


def neighbourhood_attention(q, k, v, k_ctx, v_ctx, nb, n, m_ctx, rpb, *, rq=4):
    rows = n // GRID_W
    rq = min(rq, rows)
    starts, var, band, reps = _na_plan(rows, rq)
    bias = _na_bias(rpb, rows, rq, band, starts, reps)
    tq, band_keys = rq * GRID_W, band * GRID_W
    nblk = rows // rq
    c = q.shape[1]
    blocks = (_nbytes((tq, c), BF16) * 2 + 2 * _nbytes((n, c), BF16) + 2 * _nbytes((m_ctx, c), BF16)
              + _nbytes((NA_HEADS, tq, band_keys), F32))
    grid_spec = pltpu.PrefetchScalarGridSpec(
        num_scalar_prefetch=2,
        grid=(nb, nblk),
        in_specs=[pl.BlockSpec((tq, c), lambda b, j, st, vr: (b * nblk + j, 0)),
                  pl.BlockSpec((n, c), lambda b, j, st, vr: (b, 0)),
                  pl.BlockSpec((n, c), lambda b, j, st, vr: (b, 0)),
                  pl.BlockSpec((m_ctx, c), lambda b, j, st, vr: (b, 0)),
                  pl.BlockSpec((m_ctx, c), lambda b, j, st, vr: (b, 0)),
                  pl.BlockSpec((None, NA_HEADS, tq, band_keys), lambda b, j, st, vr: (vr[j], 0, 0, 0))],
        out_specs=pl.BlockSpec((tq, c), lambda b, j, st, vr: (b * nblk + j, 0)),
    )
    return pl.pallas_call(
        functools.partial(_na_kernel, band_keys=band_keys),
        grid_spec=grid_spec,
        out_shape=jax.ShapeDtypeStruct(q.shape, BF16),
        compiler_params=_params(("parallel", "arbitrary"), blocks, 6 * _nbytes((tq, band_keys + m_ctx), F32)),
        name="neighbourhood_attention",
    )(jnp.asarray(starts, jnp.int32), jnp.asarray(var, jnp.int32), q, k, v, k_ctx, v_ctx, bias)


def _outproj_kernel(x_ref, gt_ref, y0_ref, y1_ref, y2_ref, y3_ref, w_ref, o_ref):
    y = jnp.concatenate([y0_ref[...], y1_ref[...], y2_ref[...], y3_ref[...]], axis=-1)
    o_ref[...] = x_ref[...] + gt_ref[...] * jnp.dot(y, w_ref[...], preferred_element_type=F32)


def out_projection(x, mod3, k_gate, row_of_tile, ys, layer, w_out, *, tm=512):
    r, d = x.shape
    tm = min(tm, r)
    row = lambda i: (i, 0)
    blocks = 2 * _nbytes((tm, d), F32) + _nbytes((tm, d), BF16) + _nbytes(w_out.shape[1:], BF16) // 2
    return pl.pallas_call(
        _outproj_kernel,
        grid=(r // tm,),
        in_specs=[pl.BlockSpec((tm, d), row), _mod_spec(d, row_of_tile, k_gate)]
                 + [pl.BlockSpec((tm, MIX_GROUP), row)] * 4
                 + [pl.BlockSpec((None,) + w_out.shape[1:], lambda i: (layer, 0, 0), pipeline_mode=pl.Buffered(1))],
        out_specs=pl.BlockSpec((tm, d), row),
        out_shape=jax.ShapeDtypeStruct((r, d), F32),
        compiler_params=_params(("parallel",), blocks, 3 * _nbytes((tm, d), F32)),
        name="out_projection",
    )(x, mod3, *ys, w_out)


def _rope_tables(n):
    t = jnp.arange(n, dtype=jnp.int32)
    row = (t // GRID_W).astype(F32)
    col = (t % GRID_W).astype(F32)
    n_freq = HEAD_DIM // 4
    inv = 1.0 / (ROPE_THETA ** (jnp.arange(n_freq, dtype=F32) / n_freq))
    ang = jnp.concatenate([row[:, None] * inv, col[:, None] * inv], axis=-1)
    cos = jnp.repeat(jnp.cos(ang), 2, axis=-1)
    sin = jnp.repeat(jnp.sin(ang), 2, axis=-1)
    sign = jnp.tile(jnp.asarray([-1.0, 1.0], F32), HEAD_DIM // 2)
    return cos, sin * sign


def kernel(x, c, ctx, c_ctx, w_mod, b_mod, ffn1_norm, ffn1_gate, ffn1_up, ffn1_down, mix_norm, w_in, w_out,
           pool_w, pool_scale, q_norm, k_norm, fnet_w, na_rpb, ffn2_norm, ffn2_gate, ffn2_up, ffn2_down, final_norm):
    nb, n, d = x.shape
    m = ctx.shape[1]
    depth = w_mod.shape[0]
    assert nb + 1 <= MOD_ROWS

    cvec = jnp.zeros((MOD_ROWS, d), F32).at[:nb].set(c).at[nb].set(c_ctx)
    mod = modulation(cvec, w_mod, b_mod)

    rope = _rope_tables(n)
    seq_tables = _dft_tables(n, BF16)
    ctx_tables = _dft_tables(m, BF16)
    chan_tables = _dft_tables(MIX_GROUP // FNET_GROUPS, F32)

    f1 = tuple(cast_layer(w, 0) for w in (ffn1_gate, ffn1_up, ffn1_down))
    w_in_b = cast_layer(w_in, 0)
    pool_w_b = pool_w.astype(BF16)
    xl = x.reshape(nb * n, d)
    xc = ctx.reshape(nb * m, d)
    ctx_row = lambda i: nb

    def lat_row(tm):
        return lambda i: (i * tm) // n

    for l in range(depth):
        last = l == depth - 1
        mod3 = mod[l].reshape(MOD_ROWS, 1, N_MOD * d)

        xl = ffn(xl, mod3, 0, lat_row(FFN_TM), ffn1_norm[l], 0, *f1, tm=FFN_TM)
        xc = ffn(xc, mod3, 0, ctx_row, ffn1_norm[l], 0, *f1, tm=FFN_TM)

        px, gq, gk, gv, fx, nq, nk, nv = in_projection(
            xl, mod3, 3, lat_row(512), mix_norm[l], 0, w_in_b, q_norm[l], k_norm[l], rope)
        pc, gqc, gkc, gvc, fc, nqc, nkc, nvc = in_projection(
            xc, mod3, 3, ctx_row, mix_norm[l], 0, w_in_b, q_norm[l], k_norm[l])

        casts = [(w, l) for w in (w_out, ffn2_gate, ffn2_up, ffn2_down)]
        if not last:
            casts += [(w, l + 1) for w in (ffn1_gate, ffn1_up, ffn1_down, w_in)]
        y_gqa, cast_w = dense_attention(gq, nb, n, GQA_KV_HEADS, gkc, gvc, m, gk, gv, casts)
        w_out_b, f2 = cast_w[0], cast_w[1:4]
        if not last:
            f1, w_in_b = cast_w[4:7], cast_w[7]

        ys = (pool_mix(px, n, pool_w_b[l], pool_scale[l]),
              y_gqa,
              fourier_mix(fx, nb, n, fnet_w[l], seq_tables, chan_tables),
              neighbourhood_attention(nq, nk, nv, nkc, nvc, nb, n, m, na_rpb[l]))
        xl = out_projection(xl, mod3, 5, lat_row(512), ys, 0, w_out_b)

        if not last:
            ycs = (pool_mix(pc, m, pool_w_b[l], pool_scale[l]),
                   context_attention(gqc, nb, m, GQA_KV_HEADS, gkc, gvc, m),
                   fourier_mix(fc, nb, m, fnet_w[l], ctx_tables, chan_tables),
                   context_attention(nqc, nb, m, NA_HEADS, nkc, nvc, m))
            xc = out_projection(xc, mod3, 5, ctx_row, ycs, 0, w_out_b)
            xc = ffn(xc, mod3, 6, ctx_row, ffn2_norm[l], 0, *f2, tm=FFN_TM)

        xl = ffn(xl, mod3, 6, lat_row(FFN_TM), ffn2_norm[l], 0, *f2, final_g=final_norm if last else None, tm=FFN_TM)
    return xl.reshape(nb, n, d)
```

```python
import functools

import numpy as np
import jax
import jax.numpy as jnp
from jax import lax
from jax.experimental import pallas as pl
from jax.experimental.pallas import tpu as pltpu

F32 = jnp.float32
BF16 = jnp.bfloat16

GRID_W = 64
HEAD_DIM = 128
POOL_WINDOWS = (2, 4, 8, 16)
POOL_HALO = 8
GQA_Q_HEADS = 4
GQA_KV_HEADS = 2
FNET_GROUPS = 4
NA_HEADS = 4
NA_WIN_R = 8
NA_WIN_C = 16
ROPE_THETA = 10000.0
IN_SPLITS = (512, 512, 256, 256, 512, 512, 512, 512)
MIX_GROUP = 512
EPS = 1e-6
N_MOD = 9
MOD_ROWS = 8
MASK_VALUE = -1e30
LOG2E = float(np.log2(np.e))

V7X_VMEM_BYTES = 64 * 1024 * 1024
VMEM_CAP = V7X_VMEM_BYTES - 4 * 1024 * 1024
FFN_TM = 1024
BF16_SUBLANES = 16
LANES = 128


def _params(semantics, block_bytes, scratch_bytes=0):
    need = 2 * block_bytes + scratch_bytes
    limit = min(max(need + need // 2, 32 * 1024 * 1024), VMEM_CAP)
    return pltpu.CompilerParams(dimension_semantics=semantics, vmem_limit_bytes=limit)


def _nbytes(shape, dtype):
    return int(np.prod(shape)) * jnp.dtype(dtype).itemsize


def _norm_modulate(x, gain, shift, scale):
    inv = lax.rsqrt(jnp.mean(x * x, axis=-1, keepdims=True) + EPS)
    return (x * inv) * (gain * (1.0 + scale)) + shift


def _mod_spec(d, row_of_tile, k):
    return pl.BlockSpec((None, 1, d), lambda i, *_: (row_of_tile(i), 0, k))


def _mod_kernel(c_ref, w_ref, b_ref, o_ref):
    c = c_ref[...]
    s = c * jax.nn.sigmoid(c)
    o_ref[...] = jnp.dot(s.astype(BF16), w_ref[...].astype(BF16), preferred_element_type=F32) + b_ref[...]


def modulation(cvec, w_mod, b_mod, *, tn=512):
    nl, d, nd = w_mod.shape
    blocks = _nbytes((d, tn), F32) + _nbytes((MOD_ROWS, d), F32) + 2 * _nbytes((MOD_ROWS, tn), F32)
    return pl.pallas_call(
        _mod_kernel,
        grid=(nl, nd // tn),
        in_specs=[pl.BlockSpec((MOD_ROWS, d), lambda l, j: (0, 0)),
                  pl.BlockSpec((None, d, tn), lambda l, j: (l, 0, j)),
                  pl.BlockSpec((None, 1, tn), lambda l, j: (l, 0, j))],
        out_specs=pl.BlockSpec((None, MOD_ROWS, tn), lambda l, j: (l, 0, j)),
        out_shape=jax.ShapeDtypeStruct((nl, MOD_ROWS, nd), F32),
        compiler_params=_params(("parallel", "parallel"), blocks, _nbytes((d, tn), BF16)),
        name="modulation",
    )(cvec, w_mod, b_mod.reshape(nl, 1, nd))


def _cast_kernel(src_ref, dst_ref):
    dst_ref[...] = src_ref[...].astype(dst_ref.dtype)


def cast_layer(w, layer, *, rows=256):
    _, a, b = w.shape
    rows = min(rows, a)
    return pl.pallas_call(
        _cast_kernel,
        grid=(a // rows,),
        in_specs=[pl.BlockSpec((None, rows, b), lambda i: (layer, i, 0))],
        out_specs=pl.BlockSpec((None, rows, b), lambda i: (0, i, 0)),
        out_shape=jax.ShapeDtypeStruct((1, a, b), BF16),
        compiler_params=_params(("parallel",), _nbytes((rows, b), F32) + _nbytes((rows, b), BF16)),
        name="cast_layer",
    )(w)


def _ffn_kernel(*refs, nj, final):
    if final:
        x_ref, sh_ref, sc_ref, gt_ref, ng_ref, wg_ref, wu_ref, wd_ref, fg_ref, o_ref, h_ref = refs
    else:
        x_ref, sh_ref, sc_ref, gt_ref, ng_ref, wg_ref, wu_ref, wd_ref, o_ref, h_ref = refs
    j = pl.program_id(1)

    def partial_out():
        h = h_ref[...]
        g = jnp.dot(h, wg_ref[...], preferred_element_type=F32)
        u = jnp.dot(h, wu_ref[...], preferred_element_type=F32)
        a = (g * jax.nn.sigmoid(g) * u).astype(BF16)
        return jnp.dot(a, wd_ref[...], preferred_element_type=F32)

    @pl.when(j == 0)
    def _():
        h_ref[...] = _norm_modulate(x_ref[...], ng_ref[...], sh_ref[...], sc_ref[...]).astype(BF16)
        o_ref[...] = partial_out()

    @pl.when(j > 0)
    def _():
        o_ref[...] += partial_out()

    @pl.when(j == nj - 1)
    def _():
        r = x_ref[...] + 0.5 * gt_ref[...] * o_ref[...]
        if final:
            r = r * lax.rsqrt(jnp.mean(r * r, axis=-1, keepdims=True) + EPS) * fg_ref[...]
        o_ref[...] = r


def ffn(x, mod3, k0, row_of_tile, norm_g, layer, wg, wu, wd, final_g=None, *, tm=512, tf=512):
    r, d = x.shape
    f = wg.shape[2]
    tm = min(tm, r)
    nj = f // tf
    final = final_g is not None
    row = lambda i, j: (i, 0)
    in_specs = [pl.BlockSpec((tm, d), row),
                _mod_spec(d, row_of_tile, k0), _mod_spec(d, row_of_tile, k0 + 1), _mod_spec(d, row_of_tile, k0 + 2),
                pl.BlockSpec((1, d), lambda i, j: (0, 0)),
                pl.BlockSpec((None, d, tf), lambda i, j: (layer, 0, j)),
                pl.BlockSpec((None, d, tf), lambda i, j: (layer, 0, j)),
                pl.BlockSpec((None, tf, d), lambda i, j: (layer, j, 0))]
    args = [x, mod3, mod3, mod3, norm_g.reshape(1, d), wg, wu, wd]
    if final:
        in_specs.append(pl.BlockSpec((1, d), lambda i, j: (0, 0)))
        args.append(final_g.reshape(1, d))
    blocks = 2 * _nbytes((tm, d), F32) + 3 * _nbytes((d, tf), BF16) + 5 * _nbytes((1, d), F32)
    temps = _nbytes((tm, d), BF16) + 4 * _nbytes((tm, tf), F32) + _nbytes((tm, d), F32)
    return pl.pallas_call(
        functools.partial(_ffn_kernel, nj=nj, final=final),
        grid=(r // tm, nj),
        in_specs=in_specs,
        out_specs=pl.BlockSpec((tm, d), row),
        out_shape=jax.ShapeDtypeStruct((r, d), F32),
        scratch_shapes=[pltpu.VMEM((tm, d), BF16)],
        compiler_params=_params(("parallel", "arbitrary"), blocks, temps),
        name="ffn",
    )(*args)


def _rms_heads(z, gain, nheads):
    outs = []
    for hd in range(nheads):
        zh = z[:, hd * HEAD_DIM:(hd + 1) * HEAD_DIM]
        outs.append(zh * lax.rsqrt(jnp.mean(zh * zh, axis=-1, keepdims=True) + EPS) * gain)
    return outs


def _rope(zh, cos, sin_signed):
    lane = lax.broadcasted_iota(jnp.int32, zh.shape, 1)
    partner = jnp.where(lane % 2 == 0, pltpu.roll(zh, HEAD_DIM - 1, 1), pltpu.roll(zh, 1, 1))
    return zh * cos + partner * sin_signed


def _inproj_kernel(*refs, rope):
    if rope:
        (x_ref, sh_ref, sc_ref, ng_ref, w_ref, qn_ref, kn_ref, cos_ref, sin_ref,
         px_ref, gq_ref, gk_ref, gv_ref, fx_ref, nq_ref, nk_ref, nv_ref) = refs
    else:
        (x_ref, sh_ref, sc_ref, ng_ref, w_ref, qn_ref, kn_ref,
         px_ref, gq_ref, gk_ref, gv_ref, fx_ref, nq_ref, nk_ref, nv_ref) = refs
    h = _norm_modulate(x_ref[...], ng_ref[...], sh_ref[...], sc_ref[...]).astype(BF16)
    offs = np.cumsum((0,) + IN_SPLITS)

    def proj(s):
        return jnp.dot(h, w_ref[:, offs[s]:offs[s + 1]], preferred_element_type=F32)

    sm_scale = HEAD_DIM ** -0.5 * LOG2E
    px_ref[...] = proj(0)
    q_heads = _rms_heads(proj(1), qn_ref[...], GQA_Q_HEADS)
    k_heads = _rms_heads(proj(2), kn_ref[...], GQA_KV_HEADS)
    if rope:
        cos, sin = cos_ref[...], sin_ref[...]
        q_heads = [_rope(zh, cos, sin) for zh in q_heads]
        k_heads = [_rope(zh, cos, sin) for zh in k_heads]
    gq_ref[...] = jnp.concatenate([zh * sm_scale for zh in q_heads], axis=-1).astype(BF16)
    gk_ref[...] = jnp.concatenate(k_heads, axis=-1).astype(BF16)
    gv_ref[...] = proj(3).astype(BF16)
    fx_ref[...] = proj(4).astype(BF16)
    nq_ref[...] = (proj(5) * sm_scale).astype(BF16)
    nk_ref[...] = proj(6).astype(BF16)
    nv_ref[...] = proj(7).astype(BF16)


def in_projection(x, mod3, k0, row_of_tile, norm_g, layer, w_in, q_norm, k_norm, rope_tables=None, *, tm=512):
    r, d = x.shape
    tm = min(tm, r)
    rope = rope_tables is not None
    row = lambda i: (i, 0)
    const = lambda i: (0, 0)
    in_specs = [pl.BlockSpec((tm, d), row),
                _mod_spec(d, row_of_tile, k0), _mod_spec(d, row_of_tile, k0 + 1),
                pl.BlockSpec((1, d), const),
                pl.BlockSpec((None,) + w_in.shape[1:], lambda i: (layer, 0, 0), pipeline_mode=pl.Buffered(1)),
                pl.BlockSpec((1, HEAD_DIM), const), pl.BlockSpec((1, HEAD_DIM), const)]
    args = [x, mod3, mod3, norm_g.reshape(1, d), w_in, q_norm.reshape(1, HEAD_DIM), k_norm.reshape(1, HEAD_DIM)]
    if rope:
        tiles_per_seq = rope_tables[0].shape[0] // tm
        tab = pl.BlockSpec((tm, HEAD_DIM), lambda i: (i % tiles_per_seq, 0))
        in_specs += [tab, tab]
        args += list(rope_tables)
    widths = (512, 512, 256, 256, 512, 512, 512, 512)
    dtypes = (F32,) + (BF16,) * 7
    out_specs = [pl.BlockSpec((tm, w), row) for w in widths]
    out_shape = [jax.ShapeDtypeStruct((r, w), dt) for w, dt in zip(widths, dtypes)]
    blocks = _nbytes((tm, d), F32) + _nbytes(w_in.shape[1:], BF16) // 2 + _nbytes((tm, 4096), F32)
    temps = _nbytes((tm, d), BF16) + 2 * _nbytes((tm, d), F32) + 6 * _nbytes((tm, 512), F32)
    return pl.pallas_call(
        functools.partial(_inproj_kernel, rope=rope),
        grid=(r // tm,),
        in_specs=in_specs,
        out_specs=out_specs,
        out_shape=out_shape,
        compiler_params=_params(("parallel",), blocks, temps),
        name="in_projection",
    )(*args)


def _pool_kernel(prev_ref, cur_ref, next_ref, w_ref, sc_ref, o_ref, *, n, ts):
    tiles_per_seq = n // ts
    s = pl.program_id(0) % tiles_per_seq
    cur = cur_ref[...]
    prev = jnp.where(s == 0, 0.0, prev_ref[...])
    nxt = jnp.where(s == tiles_per_seq - 1, 0.0, next_ref[...])
    ext = jnp.concatenate([prev, cur, nxt], axis=0)
    rows = ts + 2 * POOL_HALO
    t = s * ts + lax.broadcasted_iota(jnp.int32, (ts, 1), 0)
    gw = cur.shape[1] // len(POOL_WINDOWS)
    for gi, w in enumerate(POOL_WINDOWS):
        e = ext[:, gi * gw:(gi + 1) * gw]
        width = 1
        while width < w:
            e = e + pltpu.roll(e, rows - width, 0)
            width *= 2
        lead = POOL_HALO - w // 2
        win = (pltpu.roll(e, rows - lead, 0) if lead else e)[:ts]
        lo = jnp.maximum(t - w // 2, 0)
        hi = jnp.minimum(t + w // 2 - 1, n - 1)
        diff = win / (hi - lo + 1).astype(F32) - cur[:, gi * gw:(gi + 1) * gw]
        y = jnp.dot(diff.astype(BF16), w_ref[gi], preferred_element_type=F32)
        o_ref[:, gi * gw:(gi + 1) * gw] = (y * sc_ref[:, gi * gw:(gi + 1) * gw]).astype(o_ref.dtype)


def pool_mix(px, n, w_pool, scale, *, ts=512):
    r, c = px.shape
    ts = min(ts, n)
    hb = ts // POOL_HALO
    last_halo = r // POOL_HALO - 1
    blocks = 2 * _nbytes((ts, c), F32) + _nbytes(w_pool.shape, BF16)
    return pl.pallas_call(
        functools.partial(_pool_kernel, n=n, ts=ts),
        grid=(r // ts,),
        in_specs=[pl.BlockSpec((POOL_HALO, c), lambda i: (jnp.maximum(i * hb - 1, 0), 0)),
                  pl.BlockSpec((ts, c), lambda i: (i, 0)),
                  pl.BlockSpec((POOL_HALO, c), lambda i: (jnp.minimum((i + 1) * hb, last_halo), 0)),
                  pl.BlockSpec(w_pool.shape, lambda i: (0, 0, 0)),
                  pl.BlockSpec((1, c), lambda i: (0, 0))],
        out_specs=pl.BlockSpec((ts, c), lambda i: (i, 0)),
        out_shape=jax.ShapeDtypeStruct((r, c), BF16),
        compiler_params=_params(("parallel",), blocks, 6 * _nbytes((ts + 2 * POOL_HALO, c), F32)),
        name="pool_mix",
    )(px, px, px, w_pool, scale.reshape(1, c))


def _dft_chan_kernel(u_ref, c_ref, s_ref, w_ref, ap_ref, aq_ref, alt_ref, pq_ref, *, norm, tiles_per_seq):
    gw = c_ref.shape[0]
    i = pl.program_id(0)

    @pl.when(i == 0)
    def _():
        for g in range(FNET_GROUPS):
            w = w_ref[g]
            p = jnp.dot(c_ref[...], w, preferred_element_type=F32, precision=lax.Precision.HIGHEST)
            q = jnp.dot(s_ref[...], w, preferred_element_type=F32, precision=lax.Precision.HIGHEST)
            pq_ref[g] = (jnp.concatenate([p, -q], axis=-1) * norm).astype(BF16)

    sign = (1 - 2 * (lax.broadcasted_iota(jnp.int32, (u_ref.shape[0], 1), 0) % 2)).astype(F32)
    alt = []
    for g in range(FNET_GROUPS):
        a = jnp.dot(u_ref[:, g * gw:(g + 1) * gw], pq_ref[g], preferred_element_type=F32)
        ap_ref[:, g * gw:(g + 1) * gw] = a[:, :gw].astype(BF16)
        aq_ref[:, g * gw:(g + 1) * gw] = a[:, gw:].astype(BF16)
        alt.append(jnp.sum(a[:, :gw] * sign, axis=0, keepdims=True))

    @pl.when(i % tiles_per_seq == 0)
    def _():
        alt_ref[...] = jnp.zeros_like(alt_ref)

    alt_ref[...] += jnp.concatenate(alt, axis=-1)


def _dft_seq_kernel(c_ref, s_ref, ap_ref, aq_ref, sum_ref, dif_ref, g_ref, h_ref, *, nn):
    j = pl.program_id(1)

    @pl.when(j == 0)
    def _():
        g_ref[...] = jnp.zeros_like(g_ref)
        h_ref[...] = jnp.zeros_like(h_ref)

    for b in range(g_ref.shape[0]):
        g_ref[b] += jnp.dot(c_ref[...], ap_ref[b], preferred_element_type=F32)
        h_ref[b] += jnp.dot(s_ref[...], aq_ref[b], preferred_element_type=F32)

    @pl.when(j == nn - 1)
    def _():
        sum_ref[...] = (g_ref[...] + h_ref[...]).astype(sum_ref.dtype)
        dif_ref[...] = (g_ref[...] - h_ref[...]).astype(dif_ref.dtype)


def _dft_tables(n, dtype, rows=None, tile=512):
    rows = n if rows is None else rows

    def cos_sin(cols):
        k = jnp.arange(rows, dtype=jnp.int32)[:, None]
        ang = ((k * cols[None, :]) % n).astype(F32) * (2.0 * np.pi / n)
        return jnp.cos(ang), jnp.sin(ang)

    if n <= tile:
        c, s = cos_sin(jnp.arange(n, dtype=jnp.int32))
        return c.astype(dtype), s.astype(dtype)
    c1, s1 = cos_sin(jnp.arange(tile, dtype=jnp.int32))
    c0, s0 = cos_sin(jnp.arange(n // tile, dtype=jnp.int32) * tile)
    c = c0[:, :, None] * c1[:, None, :] - s0[:, :, None] * s1[:, None, :]
    s = s0[:, :, None] * c1[:, None, :] + c0[:, :, None] * s1[:, None, :]
    return c.reshape(rows, n).astype(dtype), s.reshape(rows, n).astype(dtype)


def fourier_mix(fx, nb, n, w_fnet, seq_tables, chan_tables, *, tm=512, tk=512, tn=512):
    r, c = fx.shape
    gw = c // FNET_GROUPS
    half = n // 2
    tm, tk, tn = min(tm, n), min(tk, half), min(tn, n)
    cc, sc = chan_tables
    tiles_per_seq = n // tm
    ap, aq, alt = pl.pallas_call(
        functools.partial(_dft_chan_kernel, norm=float(1.0 / np.sqrt(n * gw)), tiles_per_seq=tiles_per_seq),
        grid=(r // tm,),
        in_specs=[pl.BlockSpec((tm, c), lambda i: (i, 0)),
                  pl.BlockSpec((gw, gw), lambda i: (0, 0)),
                  pl.BlockSpec((gw, gw), lambda i: (0, 0)),
                  pl.BlockSpec(w_fnet.shape, lambda i: (0, 0, 0))],
        out_specs=[pl.BlockSpec((tm, c), lambda i: (i, 0))] * 2
                  + [pl.BlockSpec((None, 1, c), lambda i: (i // tiles_per_seq, 0, 0))],
        out_shape=[jax.ShapeDtypeStruct((r, c), BF16)] * 2 + [jax.ShapeDtypeStruct((nb, 1, c), F32)],
        scratch_shapes=[pltpu.VMEM((FNET_GROUPS, gw, 2 * gw), BF16)],
        compiler_params=_params(("arbitrary",), 3 * _nbytes((tm, c), BF16) + 6 * _nbytes((gw, gw), F32),
                                4 * _nbytes((tm, c), F32)),
        name="dft_channels",
    )(fx, cc, sc, w_fnet)
    cn, sn = seq_tables
    nn = n // tn
    blocks = 2 * _nbytes((tk, tn), BF16) + 2 * _nbytes((nb, tn, c), BF16) + 2 * _nbytes((nb, tk, c), BF16)
    y_sum, y_dif = pl.pallas_call(
        functools.partial(_dft_seq_kernel, nn=nn),
        grid=(half // tk, nn),
        in_specs=[pl.BlockSpec((tk, tn), lambda i, j: (i, j)),
                  pl.BlockSpec((tk, tn), lambda i, j: (i, j)),
                  pl.BlockSpec((nb, tn, c), lambda i, j: (0, j, 0)),
                  pl.BlockSpec((nb, tn, c), lambda i, j: (0, j, 0))],
        out_specs=[pl.BlockSpec((nb, tk, c), lambda i, j: (0, i, 0))] * 2,
        out_shape=[jax.ShapeDtypeStruct((nb, half, c), BF16)] * 2,
        scratch_shapes=[pltpu.VMEM((nb, tk, c), F32)] * 2,
        compiler_params=_params(("parallel", "arbitrary"), blocks, 4 * _nbytes((nb, tk, c), F32)),
        name="dft_positions",
    )(cn, sn, ap.reshape(nb, n, c), aq.reshape(nb, n, c))
    y = jnp.concatenate([y_sum, alt.astype(BF16), jnp.flip(y_dif[:, 1:], axis=1)], axis=1)
    return y.reshape(r, c)


def _attend(q, segments):
    dn = (((1,), (1,)), ((), ()))
    scores = []
    for k, _, bias in segments:
        s = lax.dot_general(q, k, dn, preferred_element_type=F32)
        scores.append(s if bias is None else s + bias)
    m = functools.reduce(jnp.maximum, [jnp.max(s, axis=-1, keepdims=True) for s in scores])
    probs = [jnp.exp2(s - m) for s in scores]
    l = sum(jnp.sum(p, axis=-1, keepdims=True) for p in probs)
    o = sum(jnp.dot(p.astype(BF16), v, preferred_element_type=F32) for p, (_, v, _) in zip(probs, segments))
    return o / l


def _ctx_attn_kernel(q_ref, kc_ref, vc_ref, o_ref, *, group):
    segments = [(kc_ref[...], vc_ref[...], None)]
    for g in range(group):
        hs = slice(g * HEAD_DIM, (g + 1) * HEAD_DIM)
        o_ref[:, hs] = _attend(q_ref[:, hs], segments).astype(o_ref.dtype)


def context_attention(q, nb, nq, kv_heads, k_ctx, v_ctx, m_ctx):
    group = q.shape[1] // HEAD_DIM // kv_heads
    gd = group * HEAD_DIM
    ctx = pl.BlockSpec((m_ctx, HEAD_DIM), lambda b, h: (b, h))
    blocks = 2 * _nbytes((nq, gd), BF16) + 2 * _nbytes((m_ctx, HEAD_DIM), BF16)
    return pl.pallas_call(
        functools.partial(_ctx_attn_kernel, group=group),
        grid=(nb, kv_heads),
        in_specs=[pl.BlockSpec((nq, gd), lambda b, h: (b, h)), ctx, ctx],
        out_specs=pl.BlockSpec((nq, gd), lambda b, h: (b, h)),
        out_shape=jax.ShapeDtypeStruct(q.shape, BF16),
        compiler_params=_params(("parallel", "parallel"), blocks, 6 * _nbytes((nq, m_ctx), F32)),
        name="context_attention",
    )(q, k_ctx, v_ctx)


def _dense_attn_kernel(*refs, group, n_casts):
    q_ref, kl_ref, vl_ref, kc_ref, vc_ref = refs[:5]
    cast_src = refs[5:5 + n_casts]
    o_ref = refs[5 + n_casts]
    cast_dst = refs[6 + n_casts:]
    n_lat = kl_ref.shape[0]
    segments = [(kc_ref[...], vc_ref[...], None)]
    for c in range(0, n_lat, ATTN_KEY_CHUNK):
        segments.append((kl_ref[c:c + ATTN_KEY_CHUNK, :], vl_ref[c:c + ATTN_KEY_CHUNK, :], None))
    for g in range(group):
        hs = slice(g * HEAD_DIM, (g + 1) * HEAD_DIM)
        for r in range(0, q_ref.shape[0], ATTN_CHAIN_ROWS):
            rs = slice(r, r + ATTN_CHAIN_ROWS)
            o_ref[rs, hs] = _attend(q_ref[rs, hs], segments).astype(o_ref.dtype)
    for src, dst in zip(cast_src, cast_dst):
        dst[...] = src[...].astype(dst.dtype)


ATTN_CHAIN_ROWS = 256
ATTN_KEY_CHUNK = 1024


def dense_attention(q, nb, nq, kv_heads, k_ctx, v_ctx, m_ctx, k_lat, v_lat, casts=(), *, tq=512):
    heads = q.shape[1] // HEAD_DIM
    group = heads // kv_heads
    nqt = nq // tq
    gd = group * HEAD_DIM
    rows, keys = group * tq, nq + m_ctx
    qmap = lambda b, h, i: (b * nqt + i, h)
    lat = pl.BlockSpec((nq, HEAD_DIM), lambda b, h, i: (b, h))
    ctx = pl.BlockSpec((m_ctx, HEAD_DIM), lambda b, h, i: (b, h))
    blocks = 2 * _nbytes((tq, gd), BF16) + 2 * _nbytes((keys, HEAD_DIM), BF16)
    steps = nb * kv_heads * nqt
    cast_specs_in, cast_specs_out, cast_shapes = [], [], []
    for w, layer in casts:
        _, rows_w, cols_w = w.shape
        col_blocks = next(c for c in (1, 2, 4, 8) if steps % c == 0 and rows_w % (steps // c) == 0
                          and (rows_w // (steps // c)) % BF16_SUBLANES == 0 and cols_w % (c * LANES) == 0)
        slab = (rows_w // (steps // col_blocks), cols_w // col_blocks)

        def slab_index(b, h, i, col_blocks=col_blocks):
            step = (b * kv_heads + h) * nqt + i
            return step // col_blocks, step % col_blocks

        cast_specs_in.append(pl.BlockSpec((None,) + slab, lambda b, h, i, f=slab_index, layer=layer: (layer,) + f(b, h, i)))
        cast_specs_out.append(pl.BlockSpec((None,) + slab, lambda b, h, i, f=slab_index: (0,) + f(b, h, i)))
        cast_shapes.append(jax.ShapeDtypeStruct((1, rows_w, cols_w), BF16))
        blocks += _nbytes(slab, F32) + _nbytes(slab, BF16)
    outs = pl.pallas_call(
        functools.partial(_dense_attn_kernel, group=group, n_casts=len(casts)),
        grid=(nb, kv_heads, nqt),
        in_specs=[pl.BlockSpec((tq, gd), qmap), lat, lat, ctx, ctx] + cast_specs_in,
        out_specs=[pl.BlockSpec((tq, gd), qmap)] + cast_specs_out,
        out_shape=[jax.ShapeDtypeStruct(q.shape, BF16)] + cast_shapes,
        compiler_params=_params(("parallel", "parallel", "parallel"), blocks, 3 * _nbytes((rows, keys), F32)),
        name="dense_attention",
    )(q, k_lat, v_lat, k_ctx, v_ctx, *[w for w, _ in casts])
    return outs[0], list(outs[1:])


def _na_plan(rows, rq):
    wr = min(NA_WIN_R, rows)
    band = min(rq + wr - 1, rows)
    nblk = rows // rq
    starts = [int(np.clip(rq * j - wr // 2, 0, rows - band)) for j in range(nblk)]
    deltas = [starts[j] - rq * j for j in range(nblk)]
    uniq = sorted(set(deltas))
    var = [uniq.index(dl) for dl in deltas]
    reps = [deltas.index(dl) for dl in uniq]
    return starts, var, band, reps


def _na_bias_kernel(rpb_ref, o_ref, *, rows, rq, band, starts, reps):
    h = pl.program_id(0)
    wr = min(NA_WIN_R, rows)
    qc = lax.broadcasted_iota(jnp.int32, (GRID_W, GRID_W), 0)
    kc = lax.broadcasted_iota(jnp.int32, (GRID_W, GRID_W), 1)
    dc = kc - qc + (NA_WIN_C - 1)
    cs = jnp.clip(qc - NA_WIN_C // 2, 0, GRID_W - NA_WIN_C)
    col_ok = (kc >= cs) & (kc < cs + NA_WIN_C)
    masked = jnp.full((GRID_W, GRID_W), MASK_VALUE, F32)
    by_dr = []
    for dr in range(2 * NA_WIN_R - 1):
        t = masked
        for d in range(2 * NA_WIN_C - 1):
            t = jnp.where(dc == d, rpb_ref[h, dr, d] * LOG2E, t)
        by_dr.append(jnp.where(col_ok, t, MASK_VALUE))
    for v, j in enumerate(reps):
        for e in range(rq):
            qr = rq * j + e
            rs = int(np.clip(qr - wr // 2, 0, rows - wr))
            for a in range(band):
                kr = starts[j] + a
                blk = by_dr[kr - qr + NA_WIN_R - 1] if rs <= kr < rs + wr else masked
                o_ref[v, e * GRID_W:(e + 1) * GRID_W, a * GRID_W:(a + 1) * GRID_W] = blk


def _na_bias(rpb, rows, rq, band, starts, reps):
    nv = len(reps)
    tq, bk = rq * GRID_W, band * GRID_W
    return pl.pallas_call(
        functools.partial(_na_bias_kernel, rows=rows, rq=rq, band=band, starts=starts, reps=reps),
        grid=(NA_HEADS,),
        in_specs=[pl.BlockSpec(memory_space=pltpu.SMEM)],
        out_specs=pl.BlockSpec((nv, None, tq, bk), lambda h: (0, h, 0, 0)),
        out_shape=jax.ShapeDtypeStruct((nv, NA_HEADS, tq, bk), F32),
        compiler_params=_params(("parallel",), _nbytes((nv, tq, bk), F32)),
        name="na_bias",
    )(rpb)


def _na_kernel(start_ref, var_ref, q_ref, k_ref, v_ref, kc_ref, vc_ref, bias_ref, o_ref, *, band_keys):
    del var_ref
    j = pl.program_id(1)
    st = pl.multiple_of(start_ref[j] * GRID_W, GRID_W)
    for h in range(NA_HEADS):
        hs = slice(h * HEAD_DIM, (h + 1) * HEAD_DIM)
        segments = [(k_ref[pl.ds(st, band_keys), hs], v_ref[pl.ds(st, band_keys), hs], bias_ref[h]),
                    (kc_ref[:, hs], vc_ref[:, hs], None)]
        o_ref[:, hs] = _attend(q_ref[:, hs], segments).astype(o_ref.dtype)


def neighbourhood_attention(q, k, v, k_ctx, v_ctx, nb, n, m_ctx, rpb, *, rq=4):
    rows = n // GRID_W
    rq = min(rq, rows)
    starts, var, band, reps = _na_plan(rows, rq)
    bias = _na_bias(rpb, rows, rq, band, starts, reps)
    tq, band_keys = rq * GRID_W, band * GRID_W
    nblk = rows // rq
    c = q.shape[1]
    blocks = (_nbytes((tq, c), BF16) * 2 + 2 * _nbytes((n, c), BF16) + 2 * _nbytes((m_ctx, c), BF16)
              + _nbytes((NA_HEADS, tq, band_keys), F32))
    grid_spec = pltpu.PrefetchScalarGridSpec(
        num_scalar_prefetch=2,
        grid=(nb, nblk),
        in_specs=[pl.BlockSpec((tq, c), lambda b, j, st, vr: (b * nblk + j, 0)),
                  pl.BlockSpec((n, c), lambda b, j, st, vr: (b, 0)),
                  pl.BlockSpec((n, c), lambda b, j, st, vr: (b, 0)),
                  pl.BlockSpec((m_ctx, c), lambda b, j, st, vr: (b, 0)),
                  pl.BlockSpec((m_ctx, c), lambda b, j, st, vr: (b, 0)),
                  pl.BlockSpec((None, NA_HEADS, tq, band_keys), lambda b, j, st, vr: (vr[j], 0, 0, 0))],
        out_specs=pl.BlockSpec((tq, c), lambda b, j, st, vr: (b * nblk + j, 0)),
    )
    return pl.pallas_call(
        functools.partial(_na_kernel, band_keys=band_keys),
        grid_spec=grid_spec,
        out_shape=jax.ShapeDtypeStruct(q.shape, BF16),
        compiler_params=_params(("parallel", "arbitrary"), blocks, 6 * _nbytes((tq, band_keys + m_ctx), F32)),
        name="neighbourhood_attention",
    )(jnp.asarray(starts, jnp.int32), jnp.asarray(var, jnp.int32), q, k, v, k_ctx, v_ctx, bias)


def _outproj_kernel(x_ref, gt_ref, y0_ref, y1_ref, y2_ref, y3_ref, w_ref, o_ref):
    y = jnp.concatenate([y0_ref[...], y1_ref[...], y2_ref[...], y3_ref[...]], axis=-1)
    o_ref[...] = x_ref[...] + gt_ref[...] * jnp.dot(y, w_ref[...], preferred_element_type=F32)


def out_projection(x, mod3, k_gate, row_of_tile, ys, layer, w_out, *, tm=512):
    r, d = x.shape
    tm = min(tm, r)
    row = lambda i: (i, 0)
    blocks = 2 * _nbytes((tm, d), F32) + _nbytes((tm, d), BF16) + _nbytes(w_out.shape[1:], BF16) // 2
    return pl.pallas_call(
        _outproj_kernel,
        grid=(r // tm,),
        in_specs=[pl.BlockSpec((tm, d), row), _mod_spec(d, row_of_tile, k_gate)]
                 + [pl.BlockSpec((tm, MIX_GROUP), row)] * 4
                 + [pl.BlockSpec((None,) + w_out.shape[1:], lambda i: (layer, 0, 0), pipeline_mode=pl.Buffered(1))],
        out_specs=pl.BlockSpec((tm, d), row),
        out_shape=jax.ShapeDtypeStruct((r, d), F32),
        compiler_params=_params(("parallel",), blocks, 3 * _nbytes((tm, d), F32)),
        name="out_projection",
    )(x, mod3, *ys, w_out)


def _rope_tables(n):
    t = jnp.arange(n, dtype=jnp.int32)
    row = (t // GRID_W).astype(F32)
    col = (t % GRID_W).astype(F32)
    n_freq = HEAD_DIM // 4
    inv = 1.0 / (ROPE_THETA ** (jnp.arange(n_freq, dtype=F32) / n_freq))
    ang = jnp.concatenate([row[:, None] * inv, col[:, None] * inv], axis=-1)
    cos = jnp.repeat(jnp.cos(ang), 2, axis=-1)
    sin = jnp.repeat(jnp.sin(ang), 2, axis=-1)
    sign = jnp.tile(jnp.asarray([-1.0, 1.0], F32), HEAD_DIM // 2)
    return cos, sin * sign


def kernel(x, c, ctx, c_ctx, w_mod, b_mod, ffn1_norm, ffn1_gate, ffn1_up, ffn1_down, mix_norm, w_in, w_out,
           pool_w, pool_scale, q_norm, k_norm, fnet_w, na_rpb, ffn2_norm, ffn2_gate, ffn2_up, ffn2_down, final_norm):
    nb, n, d = x.shape
    m = ctx.shape[1]
    depth = w_mod.shape[0]
    assert nb + 1 <= MOD_ROWS

    cvec = jnp.zeros((MOD_ROWS, d), F32).at[:nb].set(c).at[nb].set(c_ctx)
    mod = modulation(cvec, w_mod, b_mod)

    rope = _rope_tables(n)
    seq_tables = _dft_tables(n, BF16, rows=n // 2)
    ctx_tables = _dft_tables(m, BF16, rows=m // 2)
    chan_tables = _dft_tables(MIX_GROUP // FNET_GROUPS, F32)

    f1 = tuple(cast_layer(w, 0) for w in (ffn1_gate, ffn1_up, ffn1_down))
    w_in_b = cast_layer(w_in, 0)
    pool_w_b = pool_w.astype(BF16)
    xl = x.reshape(nb * n, d)
    xc = ctx.reshape(nb * m, d)
    ctx_row = lambda i: nb

    def lat_row(tm):
        return lambda i: (i * tm) // n

    for l in range(depth):
        last = l == depth - 1
        mod3 = mod[l].reshape(MOD_ROWS, 1, N_MOD * d)

        xl = ffn(xl, mod3, 0, lat_row(FFN_TM), ffn1_norm[l], 0, *f1, tm=FFN_TM)
        xc = ffn(xc, mod3, 0, ctx_row, ffn1_norm[l], 0, *f1, tm=FFN_TM)

        px, gq, gk, gv, fx, nq, nk, nv = in_projection(
            xl, mod3, 3, lat_row(512), mix_norm[l], 0, w_in_b, q_norm[l], k_norm[l], rope)
        pc, gqc, gkc, gvc, fc, nqc, nkc, nvc = in_projection(
            xc, mod3, 3, ctx_row, mix_norm[l], 0, w_in_b, q_norm[l], k_norm[l])

        casts = [(w, l) for w in (w_out, ffn2_gate, ffn2_up, ffn2_down)]
        if not last:
            casts += [(w, l + 1) for w in (ffn1_gate, ffn1_up, ffn1_down, w_in)]
        y_gqa, cast_w = dense_attention(gq, nb, n, GQA_KV_HEADS, gkc, gvc, m, gk, gv, casts)
        w_out_b, f2 = cast_w[0], cast_w[1:4]
        if not last:
            f1, w_in_b = cast_w[4:7], cast_w[7]

        ys = (pool_mix(px, n, pool_w_b[l], pool_scale[l]),
              y_gqa,
              fourier_mix(fx, nb, n, fnet_w[l], seq_tables, chan_tables),
              neighbourhood_attention(nq, nk, nv, nkc, nvc, nb, n, m, na_rpb[l]))
        xl = out_projection(xl, mod3, 5, lat_row(512), ys, 0, w_out_b)

        if not last:
            ycs = (pool_mix(pc, m, pool_w_b[l], pool_scale[l]),
                   context_attention(gqc, nb, m, GQA_KV_HEADS, gkc, gvc, m),
                   fourier_mix(fc, nb, m, fnet_w[l], ctx_tables, chan_tables),
                   context_attention(nqc, nb, m, NA_HEADS, nkc, nvc, m))
            xc = out_projection(xc, mod3, 5, ctx_row, ycs, 0, w_out_b)
            xc = ffn(xc, mod3, 6, ctx_row, ffn2_norm[l], 0, *f2, tm=FFN_TM)

        xl = ffn(xl, mod3, 6, lat_row(FFN_TM), ffn2_norm[l], 0, *f2, final_g=final_norm if last else None, tm=FFN_TM)
    return xl.reshape(nb, n, d)
```

```python
import functools

import numpy as np
import jax
import jax.numpy as jnp
from jax import lax
from jax.experimental import pallas as pl
from jax.experimental.pallas import tpu as pltpu

F32 = jnp.float32
BF16 = jnp.bfloat16

GRID_W = 64
HEAD_DIM = 128
POOL_WINDOWS = (2, 4, 8, 16)
POOL_HALO = 8
GQA_Q_HEADS = 4
GQA_KV_HEADS = 2
FNET_GROUPS = 4
NA_HEADS = 4
NA_WIN_R = 8
NA_WIN_C = 16
ROPE_THETA = 10000.0
IN_SPLITS = (512, 512, 256, 256, 512, 512, 512, 512)
MIX_GROUP = 512
EPS = 1e-6
N_MOD = 9
MOD_ROWS = 8
MASK_VALUE = -1e30
LOG2E = float(np.log2(np.e))

V7X_VMEM_BYTES = 64 * 1024 * 1024
VMEM_CAP = V7X_VMEM_BYTES - 4 * 1024 * 1024
FFN_TM = 1024
BF16_SUBLANES = 16
LANES = 128


def _params(semantics, block_bytes, scratch_bytes=0):
    need = 2 * block_bytes + scratch_bytes
    limit = min(max(need + need // 2, 32 * 1024 * 1024), VMEM_CAP)
    return pltpu.CompilerParams(dimension_semantics=semantics, vmem_limit_bytes=limit)


def _nbytes(shape, dtype):
    return int(np.prod(shape)) * jnp.dtype(dtype).itemsize


def _norm_modulate(x, gain, shift, scale):
    inv = lax.rsqrt(jnp.mean(x * x, axis=-1, keepdims=True) + EPS)
    return (x * inv) * (gain * (1.0 + scale)) + shift


def _mod_spec(d, row_of_tile, k):
    return pl.BlockSpec((None, 1, d), lambda i, *_: (row_of_tile(i), 0, k))


def _mod_kernel(c_ref, w_ref, b_ref, o_ref):
    c = c_ref[...]
    s = c * jax.nn.sigmoid(c)
    o_ref[...] = jnp.dot(s.astype(BF16), w_ref[...].astype(BF16), preferred_element_type=F32) + b_ref[...]


def modulation(cvec, w_mod, b_mod, *, tn=512):
    nl, d, nd = w_mod.shape
    blocks = _nbytes((d, tn), F32) + _nbytes((MOD_ROWS, d), F32) + 2 * _nbytes((MOD_ROWS, tn), F32)
    return pl.pallas_call(
        _mod_kernel,
        grid=(nl, nd // tn),
        in_specs=[pl.BlockSpec((MOD_ROWS, d), lambda l, j: (0, 0)),
                  pl.BlockSpec((None, d, tn), lambda l, j: (l, 0, j)),
                  pl.BlockSpec((None, 1, tn), lambda l, j: (l, 0, j))],
        out_specs=pl.BlockSpec((None, MOD_ROWS, tn), lambda l, j: (l, 0, j)),
        out_shape=jax.ShapeDtypeStruct((nl, MOD_ROWS, nd), F32),
        compiler_params=_params(("parallel", "parallel"), blocks, _nbytes((d, tn), BF16)),
        name="modulation",
    )(cvec, w_mod, b_mod.reshape(nl, 1, nd))


def _cast_kernel(src_ref, dst_ref):
    dst_ref[...] = src_ref[...].astype(dst_ref.dtype)


def cast_layer(w, layer, *, rows=256):
    _, a, b = w.shape
    rows = min(rows, a)
    return pl.pallas_call(
        _cast_kernel,
        grid=(a // rows,),
        in_specs=[pl.BlockSpec((None, rows, b), lambda i: (layer, i, 0))],
        out_specs=pl.BlockSpec((None, rows, b), lambda i: (0, i, 0)),
        out_shape=jax.ShapeDtypeStruct((1, a, b), BF16),
        compiler_params=_params(("parallel",), _nbytes((rows, b), F32) + _nbytes((rows, b), BF16)),
        name="cast_layer",
    )(w)


def _ffn_kernel(*refs, nj, final):
    if final:
        x_ref, sh_ref, sc_ref, gt_ref, ng_ref, wg_ref, wu_ref, wd_ref, fg_ref, o_ref, h_ref = refs
    else:
        x_ref, sh_ref, sc_ref, gt_ref, ng_ref, wg_ref, wu_ref, wd_ref, o_ref, h_ref = refs
    j = pl.program_id(1)

    def partial_out():
        h = h_ref[...]
        g = jnp.dot(h, wg_ref[...], preferred_element_type=F32)
        u = jnp.dot(h, wu_ref[...], preferred_element_type=F32)
        a = (g * jax.nn.sigmoid(g) * u).astype(BF16)
        return jnp.dot(a, wd_ref[...], preferred_element_type=F32)

    @pl.when(j == 0)
    def _():
        h_ref[...] = _norm_modulate(x_ref[...], ng_ref[...], sh_ref[...], sc_ref[...]).astype(BF16)
        o_ref[...] = partial_out()

    @pl.when(j > 0)
    def _():
        o_ref[...] += partial_out()

    @pl.when(j == nj - 1)
    def _():
        r = x_ref[...] + 0.5 * gt_ref[...] * o_ref[...]
        if final:
            r = r * lax.rsqrt(jnp.mean(r * r, axis=-1, keepdims=True) + EPS) * fg_ref[...]
        o_ref[...] = r


def ffn(x, mod3, k0, row_of_tile, norm_g, layer, wg, wu, wd, final_g=None, *, tm=512, tf=512):
    r, d = x.shape
    f = wg.shape[2]
    tm = min(tm, r)
    nj = f // tf
    final = final_g is not None
    row = lambda i, j: (i, 0)
    in_specs = [pl.BlockSpec((tm, d), row),
                _mod_spec(d, row_of_tile, k0), _mod_spec(d, row_of_tile, k0 + 1), _mod_spec(d, row_of_tile, k0 + 2),
                pl.BlockSpec((1, d), lambda i, j: (0, 0)),
                pl.BlockSpec((None, d, tf), lambda i, j: (layer, 0, j)),
                pl.BlockSpec((None, d, tf), lambda i, j: (layer, 0, j)),
                pl.BlockSpec((None, tf, d), lambda i, j: (layer, j, 0))]
    args = [x, mod3, mod3, mod3, norm_g.reshape(1, d), wg, wu, wd]
    if final:
        in_specs.append(pl.BlockSpec((1, d), lambda i, j: (0, 0)))
        args.append(final_g.reshape(1, d))
    blocks = 2 * _nbytes((tm, d), F32) + 3 * _nbytes((d, tf), BF16) + 5 * _nbytes((1, d), F32)
    temps = _nbytes((tm, d), BF16) + 4 * _nbytes((tm, tf), F32) + _nbytes((tm, d), F32)
    return pl.pallas_call(
        functools.partial(_ffn_kernel, nj=nj, final=final),
        grid=(r // tm, nj),
        in_specs=in_specs,
        out_specs=pl.BlockSpec((tm, d), row),
        out_shape=jax.ShapeDtypeStruct((r, d), F32),
        scratch_shapes=[pltpu.VMEM((tm, d), BF16)],
        compiler_params=_params(("parallel", "arbitrary"), blocks, temps),
        name="ffn",
    )(*args)


def _rms_heads(z, gain, nheads):
    outs = []
    for hd in range(nheads):
        zh = z[:, hd * HEAD_DIM:(hd + 1) * HEAD_DIM]
        outs.append(zh * lax.rsqrt(jnp.mean(zh * zh, axis=-1, keepdims=True) + EPS) * gain)
    return outs


def _rope(zh, cos, sin_signed):
    lane = lax.broadcasted_iota(jnp.int32, zh.shape, 1)
    partner = jnp.where(lane % 2 == 0, pltpu.roll(zh, HEAD_DIM - 1, 1), pltpu.roll(zh, 1, 1))
    return zh * cos + partner * sin_signed


def _inproj_kernel(*refs, rope):
    if rope:
        (x_ref, sh_ref, sc_ref, ng_ref, w_ref, qn_ref, kn_ref, cos_ref, sin_ref,
         px_ref, gq_ref, gk_ref, gv_ref, fx_ref, nq_ref, nk_ref, nv_ref) = refs
    else:
        (x_ref, sh_ref, sc_ref, ng_ref, w_ref, qn_ref, kn_ref,
         px_ref, gq_ref, gk_ref, gv_ref, fx_ref, nq_ref, nk_ref, nv_ref) = refs
    h = _norm_modulate(x_ref[...], ng_ref[...], sh_ref[...], sc_ref[...]).astype(BF16)
    offs = np.cumsum((0,) + IN_SPLITS)

    def proj(s):
        return jnp.dot(h, w_ref[:, offs[s]:offs[s + 1]], preferred_element_type=F32)

    sm_scale = HEAD_DIM ** -0.5 * LOG2E
    px_ref[...] = proj(0)
    q_heads = _rms_heads(proj(1), qn_ref[...], GQA_Q_HEADS)
    k_heads = _rms_heads(proj(2), kn_ref[...], GQA_KV_HEADS)
    if rope:
        cos, sin = cos_ref[...], sin_ref[...]
        q_heads = [_rope(zh, cos, sin) for zh in q_heads]
        k_heads = [_rope(zh, cos, sin) for zh in k_heads]
    gq_ref[...] = jnp.concatenate([zh * sm_scale for zh in q_heads], axis=-1).astype(BF16)
    gk_ref[...] = jnp.concatenate(k_heads, axis=-1).astype(BF16)
    gv_ref[...] = proj(3).astype(BF16)
    fx_ref[...] = proj(4).astype(BF16)
    nq_ref[...] = (proj(5) * sm_scale).astype(BF16)
    nk_ref[...] = proj(6).astype(BF16)
    nv_ref[...] = proj(7).astype(BF16)


def in_projection(x, mod3, k0, row_of_tile, norm_g, layer, w_in, q_norm, k_norm, rope_tables=None, *, tm=512):
    r, d = x.shape
    tm = min(tm, r)
    rope = rope_tables is not None
    row = lambda i: (i, 0)
    const = lambda i: (0, 0)
    in_specs = [pl.BlockSpec((tm, d), row),
                _mod_spec(d, row_of_tile, k0), _mod_spec(d, row_of_tile, k0 + 1),
                pl.BlockSpec((1, d), const),
                pl.BlockSpec((None,) + w_in.shape[1:], lambda i: (layer, 0, 0), pipeline_mode=pl.Buffered(1)),
                pl.BlockSpec((1, HEAD_DIM), const), pl.BlockSpec((1, HEAD_DIM), const)]
    args = [x, mod3, mod3, norm_g.reshape(1, d), w_in, q_norm.reshape(1, HEAD_DIM), k_norm.reshape(1, HEAD_DIM)]
    if rope:
        tiles_per_seq = rope_tables[0].shape[0] // tm
        tab = pl.BlockSpec((tm, HEAD_DIM), lambda i: (i % tiles_per_seq, 0))
        in_specs += [tab, tab]
        args += list(rope_tables)
    widths = (512, 512, 256, 256, 512, 512, 512, 512)
    dtypes = (F32,) + (BF16,) * 7
    out_specs = [pl.BlockSpec((tm, w), row) for w in widths]
    out_shape = [jax.ShapeDtypeStruct((r, w), dt) for w, dt in zip(widths, dtypes)]
    blocks = _nbytes((tm, d), F32) + _nbytes(w_in.shape[1:], BF16) // 2 + _nbytes((tm, 4096), F32)
    temps = _nbytes((tm, d), BF16) + 2 * _nbytes((tm, d), F32) + 6 * _nbytes((tm, 512), F32)
    return pl.pallas_call(
        functools.partial(_inproj_kernel, rope=rope),
        grid=(r // tm,),
        in_specs=in_specs,
        out_specs=out_specs,
        out_shape=out_shape,
        compiler_params=_params(("parallel",), blocks, temps),
        name="in_projection",
    )(*args)


def _pool_kernel(prev_ref, cur_ref, next_ref, w_ref, sc_ref, o_ref, *, n, ts):
    tiles_per_seq = n // ts
    s = pl.program_id(0) % tiles_per_seq
    cur = cur_ref[...]
    prev = jnp.where(s == 0, 0.0, prev_ref[...])
    nxt = jnp.where(s == tiles_per_seq - 1, 0.0, next_ref[...])
    ext = jnp.concatenate([prev, cur, nxt], axis=0)
    rows = ts + 2 * POOL_HALO
    t = s * ts + lax.broadcasted_iota(jnp.int32, (ts, 1), 0)
    gw = cur.shape[1] // len(POOL_WINDOWS)
    for gi, w in enumerate(POOL_WINDOWS):
        e = ext[:, gi * gw:(gi + 1) * gw]
        width = 1
        while width < w:
            e = e + pltpu.roll(e, rows - width, 0)
            width *= 2
        lead = POOL_HALO - w // 2
        win = (pltpu.roll(e, rows - lead, 0) if lead else e)[:ts]
        lo = jnp.maximum(t - w // 2, 0)
        hi = jnp.minimum(t + w // 2 - 1, n - 1)
        diff = win / (hi - lo + 1).astype(F32) - cur[:, gi * gw:(gi + 1) * gw]
        y = jnp.dot(diff.astype(BF16), w_ref[gi], preferred_element_type=F32)
        o_ref[:, gi * gw:(gi + 1) * gw] = (y * sc_ref[:, gi * gw:(gi + 1) * gw]).astype(o_ref.dtype)


def pool_mix(px, n, w_pool, scale, *, ts=512):
    r, c = px.shape
    ts = min(ts, n)
    hb = ts // POOL_HALO
    last_halo = r // POOL_HALO - 1
    blocks = 2 * _nbytes((ts, c), F32) + _nbytes(w_pool.shape, BF16)
    return pl.pallas_call(
        functools.partial(_pool_kernel, n=n, ts=ts),
        grid=(r // ts,),
        in_specs=[pl.BlockSpec((POOL_HALO, c), lambda i: (jnp.maximum(i * hb - 1, 0), 0)),
                  pl.BlockSpec((ts, c), lambda i: (i, 0)),
                  pl.BlockSpec((POOL_HALO, c), lambda i: (jnp.minimum((i + 1) * hb, last_halo), 0)),
                  pl.BlockSpec(w_pool.shape, lambda i: (0, 0, 0)),
                  pl.BlockSpec((1, c), lambda i: (0, 0))],
        out_specs=pl.BlockSpec((ts, c), lambda i: (i, 0)),
        out_shape=jax.ShapeDtypeStruct((r, c), BF16),
        compiler_params=_params(("parallel",), blocks, 6 * _nbytes((ts + 2 * POOL_HALO, c), F32)),
        name="pool_mix",
    )(px, px, px, w_pool, scale.reshape(1, c))


def _dft_chan_kernel(u_ref, c_ref, s_ref, w_ref, ap_ref, aq_ref, alt_ref, pq_ref, *, norm, tiles_per_seq):
    gw = c_ref.shape[0]
    i = pl.program_id(0)

    @pl.when(i == 0)
    def _():
        for g in range(FNET_GROUPS):
            w = w_ref[g]
            p = jnp.dot(c_ref[...], w, preferred_element_type=F32, precision=lax.Precision.HIGHEST)
            q = jnp.dot(s_ref[...], w, preferred_element_type=F32, precision=lax.Precision.HIGHEST)
            pq_ref[g] = (jnp.concatenate([p, -q], axis=-1) * norm).astype(BF16)

    sign = (1 - 2 * (lax.broadcasted_iota(jnp.int32, (u_ref.shape[0], 1), 0) % 2)).astype(F32)
    alt = []
    for g in range(FNET_GROUPS):
        a = jnp.dot(u_ref[:, g * gw:(g + 1) * gw], pq_ref[g], preferred_element_type=F32)
        ap_ref[:, g * gw:(g + 1) * gw] = a[:, :gw].astype(BF16)
        aq_ref[:, g * gw:(g + 1) * gw] = a[:, gw:].astype(BF16)
        alt.append(jnp.sum(a[:, :gw] * sign, axis=0, keepdims=True))

    @pl.when(i % tiles_per_seq == 0)
    def _():
        alt_ref[...] = jnp.zeros_like(alt_ref)

    alt_ref[...] += jnp.concatenate(alt, axis=-1)


def _dft_seq_kernel(c_ref, s_ref, ap_ref, aq_ref, alt_ref, lo_ref, hi_ref, g_ref, h_ref, carry_ref, *, nn):
    i, j = pl.program_id(0), pl.program_id(1)
    nb, tk, _ = g_ref.shape

    @pl.when(j == 0)
    def _():
        g_ref[...] = jnp.zeros_like(g_ref)
        h_ref[...] = jnp.zeros_like(h_ref)

    @pl.when((i == 0) & (j == 0))
    def _():
        carry_ref[...] = alt_ref[...]

    for b in range(nb):
        g_ref[b] += jnp.dot(c_ref[...], ap_ref[b], preferred_element_type=F32)
        h_ref[b] += jnp.dot(s_ref[...], aq_ref[b], preferred_element_type=F32)

    @pl.when(j == nn - 1)
    def _():
        lo_ref[...] = (g_ref[...] + h_ref[...]).astype(lo_ref.dtype)
        row = lax.broadcasted_iota(jnp.int32, (tk, tk), 0)
        col = lax.broadcasted_iota(jnp.int32, (tk, tk), 1)
        mirror = ((row + col) % tk == 0).astype(BF16)
        first = lax.broadcasted_iota(jnp.int32, (tk, 1), 0) == 0
        for b in range(nb):
            d = g_ref[b] - h_ref[b]
            m = jnp.dot(mirror, d.astype(BF16), preferred_element_type=F32)
            hi_ref[b] = jnp.where(first, carry_ref[b], m).astype(hi_ref.dtype)
            carry_ref[b] = d[0:1, :]


def _dft_tables(n, dtype, rows=None, tile=512):
    rows = n if rows is None else rows

    def cos_sin(cols):
        k = jnp.arange(rows, dtype=jnp.int32)[:, None]
        ang = ((k * cols[None, :]) % n).astype(F32) * (2.0 * np.pi / n)
        return jnp.cos(ang), jnp.sin(ang)

    if n <= tile:
        c, s = cos_sin(jnp.arange(n, dtype=jnp.int32))
        return c.astype(dtype), s.astype(dtype)
    c1, s1 = cos_sin(jnp.arange(tile, dtype=jnp.int32))
    c0, s0 = cos_sin(jnp.arange(n // tile, dtype=jnp.int32) * tile)
    c = c0[:, :, None] * c1[:, None, :] - s0[:, :, None] * s1[:, None, :]
    s = s0[:, :, None] * c1[:, None, :] + c0[:, :, None] * s1[:, None, :]
    return c.reshape(rows, n).astype(dtype), s.reshape(rows, n).astype(dtype)


def fourier_mix(fx, nb, n, w_fnet, seq_tables, chan_tables, *, tm=512, tk=512, tn=512):
    r, c = fx.shape
    gw = c // FNET_GROUPS
    half = n // 2
    tm, tk, tn = min(tm, n), min(tk, half), min(tn, n)
    cc, sc = chan_tables
    tiles_per_seq = n // tm
    ap, aq, alt = pl.pallas_call(
        functools.partial(_dft_chan_kernel, norm=float(1.0 / np.sqrt(n * gw)), tiles_per_seq=tiles_per_seq),
        grid=(r // tm,),
        in_specs=[pl.BlockSpec((tm, c), lambda i: (i, 0)),
                  pl.BlockSpec((gw, gw), lambda i: (0, 0)),
                  pl.BlockSpec((gw, gw), lambda i: (0, 0)),
                  pl.BlockSpec(w_fnet.shape, lambda i: (0, 0, 0))],
        out_specs=[pl.BlockSpec((tm, c), lambda i: (i, 0))] * 2
                  + [pl.BlockSpec((None, 1, c), lambda i: (i // tiles_per_seq, 0, 0))],
        out_shape=[jax.ShapeDtypeStruct((r, c), BF16)] * 2 + [jax.ShapeDtypeStruct((nb, 1, c), F32)],
        scratch_shapes=[pltpu.VMEM((FNET_GROUPS, gw, 2 * gw), BF16)],
        compiler_params=_params(("arbitrary",), 3 * _nbytes((tm, c), BF16) + 6 * _nbytes((gw, gw), F32),
                                4 * _nbytes((tm, c), F32)),
        name="dft_channels",
    )(fx, cc, sc, w_fnet)
    cn, sn = seq_tables
    nn = n // tn
    blocks = 2 * _nbytes((tk, tn), BF16) + 2 * _nbytes((nb, tn, c), BF16) + 2 * _nbytes((nb, tk, c), BF16)
    nt = half // tk
    y_lo, y_hi = pl.pallas_call(
        functools.partial(_dft_seq_kernel, nn=nn),
        grid=(nt, nn),
        in_specs=[pl.BlockSpec((tk, tn), lambda i, j: (nt - 1 - i, j)),
                  pl.BlockSpec((tk, tn), lambda i, j: (nt - 1 - i, j)),
                  pl.BlockSpec((nb, tn, c), lambda i, j: (0, j, 0)),
                  pl.BlockSpec((nb, tn, c), lambda i, j: (0, j, 0)),
                  pl.BlockSpec((nb, 1, c), lambda i, j: (0, 0, 0))],
        out_specs=[pl.BlockSpec((nb, tk, c), lambda i, j: (0, nt - 1 - i, 0)),
                   pl.BlockSpec((nb, tk, c), lambda i, j: (0, i, 0))],
        out_shape=[jax.ShapeDtypeStruct((nb, half, c), BF16)] * 2,
        scratch_shapes=[pltpu.VMEM((nb, tk, c), F32)] * 2 + [pltpu.VMEM((nb, 1, c), F32)],
        compiler_params=_params(("arbitrary", "arbitrary"), blocks, 4 * _nbytes((nb, tk, c), F32)),
        name="dft_positions",
    )(cn, sn, ap.reshape(nb, n, c), aq.reshape(nb, n, c), alt)
    return jnp.concatenate([y_lo, y_hi], axis=1).reshape(r, c)


def _attend(q, segments):
    dn = (((1,), (1,)), ((), ()))
    scores = []
    for k, _, bias in segments:
        s = lax.dot_general(q, k, dn, preferred_element_type=F32)
        scores.append(s if bias is None else s + bias)
    m = functools.reduce(jnp.maximum, [jnp.max(s, axis=-1, keepdims=True) for s in scores])
    probs = [jnp.exp2(s - m) for s in scores]
    l = sum(jnp.sum(p, axis=-1, keepdims=True) for p in probs)
    o = sum(jnp.dot(p.astype(BF16), v, preferred_element_type=F32) for p, (_, v, _) in zip(probs, segments))
    return o / l


def _ctx_attn_kernel(q_ref, kc_ref, vc_ref, o_ref, *, group):
    segments = [(kc_ref[...], vc_ref[...], None)]
    for g in range(group):
        hs = slice(g * HEAD_DIM, (g + 1) * HEAD_DIM)
        o_ref[:, hs] = _attend(q_ref[:, hs], segments).astype(o_ref.dtype)


def context_attention(q, nb, nq, kv_heads, k_ctx, v_ctx, m_ctx):
    group = q.shape[1] // HEAD_DIM // kv_heads
    gd = group * HEAD_DIM
    ctx = pl.BlockSpec((m_ctx, HEAD_DIM), lambda b, h: (b, h))
    blocks = 2 * _nbytes((nq, gd), BF16) + 2 * _nbytes((m_ctx, HEAD_DIM), BF16)
    return pl.pallas_call(
        functools.partial(_ctx_attn_kernel, group=group),
        grid=(nb, kv_heads),
        in_specs=[pl.BlockSpec((nq, gd), lambda b, h: (b, h)), ctx, ctx],
        out_specs=pl.BlockSpec((nq, gd), lambda b, h: (b, h)),
        out_shape=jax.ShapeDtypeStruct(q.shape, BF16),
        compiler_params=_params(("parallel", "parallel"), blocks, 6 * _nbytes((nq, m_ctx), F32)),
        name="context_attention",
    )(q, k_ctx, v_ctx)


def _dense_attn_kernel(*refs, group, n_casts):
    q_ref, kl_ref, vl_ref, kc_ref, vc_ref = refs[:5]
    cast_src = refs[5:5 + n_casts]
    o_ref = refs[5 + n_casts]
    cast_dst = refs[6 + n_casts:]
    n_lat = kl_ref.shape[0]
    segments = [(kc_ref[...], vc_ref[...], None)]
    for c in range(0, n_lat, ATTN_KEY_CHUNK):
        segments.append((kl_ref[c:c + ATTN_KEY_CHUNK, :], vl_ref[c:c + ATTN_KEY_CHUNK, :], None))
    for g in range(group):
        hs = slice(g * HEAD_DIM, (g + 1) * HEAD_DIM)
        for r in range(0, q_ref.shape[0], ATTN_CHAIN_ROWS):
            rs = slice(r, r + ATTN_CHAIN_ROWS)
            o_ref[rs, hs] = _attend(q_ref[rs, hs], segments).astype(o_ref.dtype)
    for src, dst in zip(cast_src, cast_dst):
        dst[...] = src[...].astype(dst.dtype)


ATTN_CHAIN_ROWS = 256
ATTN_KEY_CHUNK = 1024


def dense_attention(q, nb, nq, kv_heads, k_ctx, v_ctx, m_ctx, k_lat, v_lat, casts=(), *, tq=512):
    heads = q.shape[1] // HEAD_DIM
    group = heads // kv_heads
    nqt = nq // tq
    gd = group * HEAD_DIM
    rows, keys = group * tq, nq + m_ctx
    qmap = lambda b, h, i: (b * nqt + i, h)
    lat = pl.BlockSpec((nq, HEAD_DIM), lambda b, h, i: (b, h))
    ctx = pl.BlockSpec((m_ctx, HEAD_DIM), lambda b, h, i: (b, h))
    blocks = 2 * _nbytes((tq, gd), BF16) + 2 * _nbytes((keys, HEAD_DIM), BF16)
    steps = nb * kv_heads * nqt
    cast_specs_in, cast_specs_out, cast_shapes = [], [], []
    for w, layer in casts:
        _, rows_w, cols_w = w.shape
        col_blocks = next(c for c in (1, 2, 4, 8) if steps % c == 0 and rows_w % (steps // c) == 0
                          and (rows_w // (steps // c)) % BF16_SUBLANES == 0 and cols_w % (c * LANES) == 0)
        slab = (rows_w // (steps // col_blocks), cols_w // col_blocks)

        def slab_index(b, h, i, col_blocks=col_blocks):
            step = (b * kv_heads + h) * nqt + i
            return step // col_blocks, step % col_blocks

        cast_specs_in.append(pl.BlockSpec((None,) + slab, lambda b, h, i, f=slab_index, layer=layer: (layer,) + f(b, h, i)))
        cast_specs_out.append(pl.BlockSpec((None,) + slab, lambda b, h, i, f=slab_index: (0,) + f(b, h, i)))
        cast_shapes.append(jax.ShapeDtypeStruct((1, rows_w, cols_w), BF16))
        blocks += _nbytes(slab, F32) + _nbytes(slab, BF16)
    outs = pl.pallas_call(
        functools.partial(_dense_attn_kernel, group=group, n_casts=len(casts)),
        grid=(nb, kv_heads, nqt),
        in_specs=[pl.BlockSpec((tq, gd), qmap), lat, lat, ctx, ctx] + cast_specs_in,
        out_specs=[pl.BlockSpec((tq, gd), qmap)] + cast_specs_out,
        out_shape=[jax.ShapeDtypeStruct(q.shape, BF16)] + cast_shapes,
        compiler_params=_params(("parallel", "parallel", "parallel"), blocks, 3 * _nbytes((rows, keys), F32)),
        name="dense_attention",
    )(q, k_lat, v_lat, k_ctx, v_ctx, *[w for w, _ in casts])
    return outs[0], list(outs[1:])


def _na_plan(rows, rq):
    wr = min(NA_WIN_R, rows)
    band = min(rq + wr - 1, rows)
    nblk = rows // rq
    starts = [int(np.clip(rq * j - wr // 2, 0, rows - band)) for j in range(nblk)]
    deltas = [starts[j] - rq * j for j in range(nblk)]
    uniq = sorted(set(deltas))
    var = [uniq.index(dl) for dl in deltas]
    reps = [deltas.index(dl) for dl in uniq]
    return starts, var, band, reps


def _na_bias_kernel(rpb_ref, o_ref, *, rows, rq, band, starts, reps):
    h = pl.program_id(0)
    wr = min(NA_WIN_R, rows)
    qc = lax.broadcasted_iota(jnp.int32, (GRID_W, GRID_W), 0)
    kc = lax.broadcasted_iota(jnp.int32, (GRID_W, GRID_W), 1)
    dc = kc - qc + (NA_WIN_C - 1)
    cs = jnp.clip(qc - NA_WIN_C // 2, 0, GRID_W - NA_WIN_C)
    col_ok = (kc >= cs) & (kc < cs + NA_WIN_C)
    masked = jnp.full((GRID_W, GRID_W), MASK_VALUE, F32)
    by_dr = []
    for dr in range(2 * NA_WIN_R - 1):
        t = masked
        for d in range(2 * NA_WIN_C - 1):
            t = jnp.where(dc == d, rpb_ref[h, dr, d] * LOG2E, t)
        by_dr.append(jnp.where(col_ok, t, MASK_VALUE))
    for v, j in enumerate(reps):
        for e in range(rq):
            qr = rq * j + e
            rs = int(np.clip(qr - wr // 2, 0, rows - wr))
            for a in range(band):
                kr = starts[j] + a
                blk = by_dr[kr - qr + NA_WIN_R - 1] if rs <= kr < rs + wr else masked
                o_ref[v, e * GRID_W:(e + 1) * GRID_W, a * GRID_W:(a + 1) * GRID_W] = blk


def _na_bias(rpb, rows, rq, band, starts, reps):
    nv = len(reps)
    tq, bk = rq * GRID_W, band * GRID_W
    return pl.pallas_call(
        functools.partial(_na_bias_kernel, rows=rows, rq=rq, band=band, starts=starts, reps=reps),
        grid=(NA_HEADS,),
        in_specs=[pl.BlockSpec(memory_space=pltpu.SMEM)],
        out_specs=pl.BlockSpec((nv, None, tq, bk), lambda h: (0, h, 0, 0)),
        out_shape=jax.ShapeDtypeStruct((nv, NA_HEADS, tq, bk), F32),
        compiler_params=_params(("parallel",), _nbytes((nv, tq, bk), F32)),
        name="na_bias",
    )(rpb)


def _na_kernel(start_ref, var_ref, q_ref, k_ref, v_ref, kc_ref, vc_ref, bias_ref, o_ref, *, band_keys):
    del var_ref
    j = pl.program_id(1)
    st = pl.multiple_of(start_ref[j] * GRID_W, GRID_W)
    for h in range(NA_HEADS):
        hs = slice(h * HEAD_DIM, (h + 1) * HEAD_DIM)
        segments = [(k_ref[pl.ds(st, band_keys), hs], v_ref[pl.ds(st, band_keys), hs], bias_ref[h]),
                    (kc_ref[:, hs], vc_ref[:, hs], None)]
        o_ref[:, hs] = _attend(q_ref[:, hs], segments).astype(o_ref.dtype)


def neighbourhood_attention(q, k, v, k_ctx, v_ctx, nb, n, m_ctx, rpb, *, rq=4):
    rows = n // GRID_W
    rq = min(rq, rows)
    starts, var, band, reps = _na_plan(rows, rq)
    bias = _na_bias(rpb, rows, rq, band, starts, reps)
    tq, band_keys = rq * GRID_W, band * GRID_W
    nblk = rows // rq
    c = q.shape[1]
    blocks = (_nbytes((tq, c), BF16) * 2 + 2 * _nbytes((n, c), BF16) + 2 * _nbytes((m_ctx, c), BF16)
              + _nbytes((NA_HEADS, tq, band_keys), F32))
    grid_spec = pltpu.PrefetchScalarGridSpec(
        num_scalar_prefetch=2,
        grid=(nb, nblk),
        in_specs=[pl.BlockSpec((tq, c), lambda b, j, st, vr: (b * nblk + j, 0)),
                  pl.BlockSpec((n, c), lambda b, j, st, vr: (b, 0)),
                  pl.BlockSpec((n, c), lambda b, j, st, vr: (b, 0)),
                  pl.BlockSpec((m_ctx, c), lambda b, j, st, vr: (b, 0)),
                  pl.BlockSpec((m_ctx, c), lambda b, j, st, vr: (b, 0)),
                  pl.BlockSpec((None, NA_HEADS, tq, band_keys), lambda b, j, st, vr: (vr[j], 0, 0, 0))],
        out_specs=pl.BlockSpec((tq, c), lambda b, j, st, vr: (b * nblk + j, 0)),
    )
    return pl.pallas_call(
        functools.partial(_na_kernel, band_keys=band_keys),
        grid_spec=grid_spec,
        out_shape=jax.ShapeDtypeStruct(q.shape, BF16),
        compiler_params=_params(("parallel", "arbitrary"), blocks, 6 * _nbytes((tq, band_keys + m_ctx), F32)),
        name="neighbourhood_attention",
    )(jnp.asarray(starts, jnp.int32), jnp.asarray(var, jnp.int32), q, k, v, k_ctx, v_ctx, bias)


def _outproj_kernel(x_ref, gt_ref, y0_ref, y1_ref, y2_ref, y3_ref, w_ref, o_ref):
    y = jnp.concatenate([y0_ref[...], y1_ref[...], y2_ref[...], y3_ref[...]], axis=-1)
    o_ref[...] = x_ref[...] + gt_ref[...] * jnp.dot(y, w_ref[...], preferred_element_type=F32)


def out_projection(x, mod3, k_gate, row_of_tile, ys, layer, w_out, *, tm=512):
    r, d = x.shape
    tm = min(tm, r)
    row = lambda i: (i, 0)
    blocks = 2 * _nbytes((tm, d), F32) + _nbytes((tm, d), BF16) + _nbytes(w_out.shape[1:], BF16) // 2
    return pl.pallas_call(
        _outproj_kernel,
        grid=(r // tm,),
        in_specs=[pl.BlockSpec((tm, d), row), _mod_spec(d, row_of_tile, k_gate)]
                 + [pl.BlockSpec((tm, MIX_GROUP), row)] * 4
                 + [pl.BlockSpec((None,) + w_out.shape[1:], lambda i: (layer, 0, 0), pipeline_mode=pl.Buffered(1))],
        out_specs=pl.BlockSpec((tm, d), row),
        out_shape=jax.ShapeDtypeStruct((r, d), F32),
        compiler_params=_params(("parallel",), blocks, 3 * _nbytes((tm, d), F32)),
        name="out_projection",
    )(x, mod3, *ys, w_out)


def _rope_tables(n):
    t = jnp.arange(n, dtype=jnp.int32)
    row = (t // GRID_W).astype(F32)
    col = (t % GRID_W).astype(F32)
    n_freq = HEAD_DIM // 4
    inv = 1.0 / (ROPE_THETA ** (jnp.arange(n_freq, dtype=F32) / n_freq))
    ang = jnp.concatenate([row[:, None] * inv, col[:, None] * inv], axis=-1)
    cos = jnp.repeat(jnp.cos(ang), 2, axis=-1)
    sin = jnp.repeat(jnp.sin(ang), 2, axis=-1)
    sign = jnp.tile(jnp.asarray([-1.0, 1.0], F32), HEAD_DIM // 2)
    return cos, sin * sign


def kernel(x, c, ctx, c_ctx, w_mod, b_mod, ffn1_norm, ffn1_gate, ffn1_up, ffn1_down, mix_norm, w_in, w_out,
           pool_w, pool_scale, q_norm, k_norm, fnet_w, na_rpb, ffn2_norm, ffn2_gate, ffn2_up, ffn2_down, final_norm):
    nb, n, d = x.shape
    m = ctx.shape[1]
    depth = w_mod.shape[0]
    assert nb + 1 <= MOD_ROWS

    cvec = jnp.zeros((MOD_ROWS, d), F32).at[:nb].set(c).at[nb].set(c_ctx)
    mod = modulation(cvec, w_mod, b_mod)

    rope = _rope_tables(n)
    seq_tables = _dft_tables(n, BF16, rows=n // 2)
    ctx_tables = _dft_tables(m, BF16, rows=m // 2)
    chan_tables = _dft_tables(MIX_GROUP // FNET_GROUPS, F32)

    f1 = tuple(cast_layer(w, 0) for w in (ffn1_gate, ffn1_up, ffn1_down))
    w_in_b = cast_layer(w_in, 0)
    pool_w_b = pool_w.astype(BF16)
    xl = x.reshape(nb * n, d)
    xc = ctx.reshape(nb * m, d)
    ctx_row = lambda i: nb

    def lat_row(tm):
        return lambda i: (i * tm) // n

    for l in range(depth):
        last = l == depth - 1
        mod3 = mod[l].reshape(MOD_ROWS, 1, N_MOD * d)

        xl = ffn(xl, mod3, 0, lat_row(FFN_TM), ffn1_norm[l], 0, *f1, tm=FFN_TM)
        xc = ffn(xc, mod3, 0, ctx_row, ffn1_norm[l], 0, *f1, tm=FFN_TM)

        px, gq, gk, gv, fx, nq, nk, nv = in_projection(
            xl, mod3, 3, lat_row(512), mix_norm[l], 0, w_in_b, q_norm[l], k_norm[l], rope)
        pc, gqc, gkc, gvc, fc, nqc, nkc, nvc = in_projection(
            xc, mod3, 3, ctx_row, mix_norm[l], 0, w_in_b, q_norm[l], k_norm[l])

        casts = [(w, l) for w in (w_out, ffn2_gate, ffn2_up, ffn2_down)]
        if not last:
            casts += [(w, l + 1) for w in (ffn1_gate, ffn1_up, ffn1_down, w_in)]
        y_gqa, cast_w = dense_attention(gq, nb, n, GQA_KV_HEADS, gkc, gvc, m, gk, gv, casts)
        w_out_b, f2 = cast_w[0], cast_w[1:4]
        if not last:
            f1, w_in_b = cast_w[4:7], cast_w[7]

        ys = (pool_mix(px, n, pool_w_b[l], pool_scale[l]),
              y_gqa,
              fourier_mix(fx, nb, n, fnet_w[l], seq_tables, chan_tables),
              neighbourhood_attention(nq, nk, nv, nkc, nvc, nb, n, m, na_rpb[l]))
        xl = out_projection(xl, mod3, 5, lat_row(512), ys, 0, w_out_b)

        if not last:
            ycs = (pool_mix(pc, m, pool_w_b[l], pool_scale[l]),
                   context_attention(gqc, nb, m, GQA_KV_HEADS, gkc, gvc, m),
                   fourier_mix(fc, nb, m, fnet_w[l], ctx_tables, chan_tables),
                   context_attention(nqc, nb, m, NA_HEADS, nkc, nvc, m))
            xc = out_projection(xc, mod3, 5, ctx_row, ycs, 0, w_out_b)
            xc = ffn(xc, mod3, 6, ctx_row, ffn2_norm[l], 0, *f2, tm=FFN_TM)

        xl = ffn(xl, mod3, 6, lat_row(FFN_TM), ffn2_norm[l], 0, *f2, final_g=final_norm if last else None, tm=FFN_TM)
    return xl.reshape(nb, n, d)
```

```python
import functools

import numpy as np
import jax
import jax.numpy as jnp
from jax import lax
from jax.experimental import pallas as pl
from jax.experimental.pallas import tpu as pltpu

F32 = jnp.float32
BF16 = jnp.bfloat16

GRID_W = 64
HEAD_DIM = 128
POOL_WINDOWS = (2, 4, 8, 16)
POOL_HALO = 8
GQA_Q_HEADS = 4
GQA_KV_HEADS = 2
FNET_GROUPS = 4
NA_HEADS = 4
NA_WIN_R = 8
NA_WIN_C = 16
ROPE_THETA = 10000.0
IN_SPLITS = (512, 512, 256, 256, 512, 512, 512, 512)
MIX_GROUP = 512
EPS = 1e-6
N_MOD = 9
MOD_ROWS = 8
MASK_VALUE = -1e30
LOG2E = float(np.log2(np.e))

V7X_VMEM_BYTES = 64 * 1024 * 1024
VMEM_CAP = V7X_VMEM_BYTES - 4 * 1024 * 1024
FFN_TM = 1024
BF16_SUBLANES = 16
LANES = 128


def _params(semantics, block_bytes, scratch_bytes=0):
    need = 2 * block_bytes + scratch_bytes
    limit = min(max(need + need // 2, 32 * 1024 * 1024), VMEM_CAP)
    return pltpu.CompilerParams(dimension_semantics=semantics, vmem_limit_bytes=limit)


def _nbytes(shape, dtype):
    return int(np.prod(shape)) * jnp.dtype(dtype).itemsize


def _norm_modulate(x, gain, shift, scale):
    inv = lax.rsqrt(jnp.mean(x * x, axis=-1, keepdims=True) + EPS)
    return (x * inv) * (gain * (1.0 + scale)) + shift


def _mod_spec(d, row_of_tile, k):
    return pl.BlockSpec((None, 1, d), lambda i, *_: (row_of_tile(i), 0, k))


def _mod_kernel(c_ref, w_ref, b_ref, o_ref):
    c = c_ref[...]
    s = c * jax.nn.sigmoid(c)
    o_ref[...] = jnp.dot(s.astype(BF16), w_ref[...].astype(BF16), preferred_element_type=F32) + b_ref[...]


def modulation(cvec, w_mod, b_mod, *, tn=512):
    nl, d, nd = w_mod.shape
    blocks = _nbytes((d, tn), F32) + _nbytes((MOD_ROWS, d), F32) + 2 * _nbytes((MOD_ROWS, tn), F32)
    return pl.pallas_call(
        _mod_kernel,
        grid=(nl, nd // tn),
        in_specs=[pl.BlockSpec((MOD_ROWS, d), lambda l, j: (0, 0)),
                  pl.BlockSpec((None, d, tn), lambda l, j: (l, 0, j)),
                  pl.BlockSpec((None, 1, tn), lambda l, j: (l, 0, j))],
        out_specs=pl.BlockSpec((None, MOD_ROWS, tn), lambda l, j: (l, 0, j)),
        out_shape=jax.ShapeDtypeStruct((nl, MOD_ROWS, nd), F32),
        compiler_params=_params(("parallel", "parallel"), blocks, _nbytes((d, tn), BF16)),
        name="modulation",
    )(cvec, w_mod, b_mod.reshape(nl, 1, nd))


def _cast_kernel(src_ref, dst_ref):
    dst_ref[...] = src_ref[...].astype(dst_ref.dtype)


def cast_layer(w, layer, *, rows=256):
    _, a, b = w.shape
    rows = min(rows, a)
    return pl.pallas_call(
        _cast_kernel,
        grid=(a // rows,),
        in_specs=[pl.BlockSpec((None, rows, b), lambda i: (layer, i, 0))],
        out_specs=pl.BlockSpec((None, rows, b), lambda i: (0, i, 0)),
        out_shape=jax.ShapeDtypeStruct((1, a, b), BF16),
        compiler_params=_params(("parallel",), _nbytes((rows, b), F32) + _nbytes((rows, b), BF16)),
        name="cast_layer",
    )(w)


def _ffn_kernel(*refs, nj, final):
    if final:
        x_ref, sh_ref, sc_ref, gt_ref, ng_ref, wg_ref, wu_ref, wd_ref, fg_ref, o_ref, h_ref = refs
    else:
        x_ref, sh_ref, sc_ref, gt_ref, ng_ref, wg_ref, wu_ref, wd_ref, o_ref, h_ref = refs
    j = pl.program_id(1)

    def partial_out():
        h = h_ref[...]
        g = jnp.dot(h, wg_ref[...], preferred_element_type=F32)
        u = jnp.dot(h, wu_ref[...], preferred_element_type=F32)
        a = (g * jax.nn.sigmoid(g) * u).astype(BF16)
        return jnp.dot(a, wd_ref[...], preferred_element_type=F32)

    @pl.when(j == 0)
    def _():
        h_ref[...] = _norm_modulate(x_ref[...], ng_ref[...], sh_ref[...], sc_ref[...]).astype(BF16)
        o_ref[...] = partial_out()

    @pl.when(j > 0)
    def _():
        o_ref[...] += partial_out()

    @pl.when(j == nj - 1)
    def _():
        r = x_ref[...] + 0.5 * gt_ref[...] * o_ref[...]
        if final:
            r = r * lax.rsqrt(jnp.mean(r * r, axis=-1, keepdims=True) + EPS) * fg_ref[...]
        o_ref[...] = r


def ffn(x, mod3, k0, row_of_tile, norm_g, layer, wg, wu, wd, final_g=None, *, tm=512, tf=512):
    r, d = x.shape
    f = wg.shape[2]
    tm = min(tm, r)
    nj = f // tf
    final = final_g is not None
    row = lambda i, j: (i, 0)
    in_specs = [pl.BlockSpec((tm, d), row),
                _mod_spec(d, row_of_tile, k0), _mod_spec(d, row_of_tile, k0 + 1), _mod_spec(d, row_of_tile, k0 + 2),
                pl.BlockSpec((1, d), lambda i, j: (0, 0)),
                pl.BlockSpec((None, d, tf), lambda i, j: (layer, 0, j)),
                pl.BlockSpec((None, d, tf), lambda i, j: (layer, 0, j)),
                pl.BlockSpec((None, tf, d), lambda i, j: (layer, j, 0))]
    args = [x, mod3, mod3, mod3, norm_g.reshape(1, d), wg, wu, wd]
    if final:
        in_specs.append(pl.BlockSpec((1, d), lambda i, j: (0, 0)))
        args.append(final_g.reshape(1, d))
    blocks = 2 * _nbytes((tm, d), F32) + 3 * _nbytes((d, tf), BF16) + 5 * _nbytes((1, d), F32)
    temps = _nbytes((tm, d), BF16) + 4 * _nbytes((tm, tf), F32) + _nbytes((tm, d), F32)
    return pl.pallas_call(
        functools.partial(_ffn_kernel, nj=nj, final=final),
        grid=(r // tm, nj),
        in_specs=in_specs,
        out_specs=pl.BlockSpec((tm, d), row),
        out_shape=jax.ShapeDtypeStruct((r, d), F32),
        scratch_shapes=[pltpu.VMEM((tm, d), BF16)],
        compiler_params=_params(("parallel", "arbitrary"), blocks, temps),
        name="ffn",
    )(*args)


def _rms_heads(z, gain, nheads):
    outs = []
    for hd in range(nheads):
        zh = z[:, hd * HEAD_DIM:(hd + 1) * HEAD_DIM]
        outs.append(zh * lax.rsqrt(jnp.mean(zh * zh, axis=-1, keepdims=True) + EPS) * gain)
    return outs


def _rope(zh, cos, sin_signed):
    lane = lax.broadcasted_iota(jnp.int32, zh.shape, 1)
    partner = jnp.where(lane % 2 == 0, pltpu.roll(zh, HEAD_DIM - 1, 1), pltpu.roll(zh, 1, 1))
    return zh * cos + partner * sin_signed


def _inproj_kernel(*refs, rope):
    if rope:
        (x_ref, sh_ref, sc_ref, ng_ref, w_ref, qn_ref, kn_ref, cos_ref, sin_ref,
         px_ref, gq_ref, gk_ref, gv_ref, fx_ref, nq_ref, nk_ref, nv_ref) = refs
    else:
        (x_ref, sh_ref, sc_ref, ng_ref, w_ref, qn_ref, kn_ref,
         px_ref, gq_ref, gk_ref, gv_ref, fx_ref, nq_ref, nk_ref, nv_ref) = refs
    h = _norm_modulate(x_ref[...], ng_ref[...], sh_ref[...], sc_ref[...]).astype(BF16)
    offs = np.cumsum((0,) + IN_SPLITS)

    def proj(s):
        return jnp.dot(h, w_ref[:, offs[s]:offs[s + 1]], preferred_element_type=F32)

    sm_scale = HEAD_DIM ** -0.5 * LOG2E
    px_ref[...] = proj(0)
    q_heads = _rms_heads(proj(1), qn_ref[...], GQA_Q_HEADS)
    k_heads = _rms_heads(proj(2), kn_ref[...], GQA_KV_HEADS)
    if rope:
        cos, sin = cos_ref[...], sin_ref[...]
        q_heads = [_rope(zh, cos, sin) for zh in q_heads]
        k_heads = [_rope(zh, cos, sin) for zh in k_heads]
    gq_ref[...] = jnp.concatenate([zh * sm_scale for zh in q_heads], axis=-1).astype(BF16)
    gk_ref[...] = jnp.concatenate(k_heads, axis=-1).astype(BF16)
    gv_ref[...] = proj(3).astype(BF16)
    fx_ref[...] = proj(4).astype(BF16)
    nq_ref[...] = (proj(5) * sm_scale).astype(BF16)
    nk_ref[...] = proj(6).astype(BF16)
    nv_ref[...] = proj(7).astype(BF16)


def in_projection(x, mod3, k0, row_of_tile, norm_g, layer, w_in, q_norm, k_norm, rope_tables=None, *, tm=512):
    r, d = x.shape
    tm = min(tm, r)
    rope = rope_tables is not None
    row = lambda i: (i, 0)
    const = lambda i: (0, 0)
    in_specs = [pl.BlockSpec((tm, d), row),
                _mod_spec(d, row_of_tile, k0), _mod_spec(d, row_of_tile, k0 + 1),
                pl.BlockSpec((1, d), const),
                pl.BlockSpec((None,) + w_in.shape[1:], lambda i: (layer, 0, 0), pipeline_mode=pl.Buffered(1)),
                pl.BlockSpec((1, HEAD_DIM), const), pl.BlockSpec((1, HEAD_DIM), const)]
    args = [x, mod3, mod3, norm_g.reshape(1, d), w_in, q_norm.reshape(1, HEAD_DIM), k_norm.reshape(1, HEAD_DIM)]
    if rope:
        tiles_per_seq = rope_tables[0].shape[0] // tm
        tab = pl.BlockSpec((tm, HEAD_DIM), lambda i: (i % tiles_per_seq, 0))
        in_specs += [tab, tab]
        args += list(rope_tables)
    widths = (512, 512, 256, 256, 512, 512, 512, 512)
    dtypes = (F32,) + (BF16,) * 7
    out_specs = [pl.BlockSpec((tm, w), row) for w in widths]
    out_shape = [jax.ShapeDtypeStruct((r, w), dt) for w, dt in zip(widths, dtypes)]
    blocks = _nbytes((tm, d), F32) + _nbytes(w_in.shape[1:], BF16) // 2 + _nbytes((tm, 4096), F32)
    temps = _nbytes((tm, d), BF16) + 2 * _nbytes((tm, d), F32) + 6 * _nbytes((tm, 512), F32)
    return pl.pallas_call(
        functools.partial(_inproj_kernel, rope=rope),
        grid=(r // tm,),
        in_specs=in_specs,
        out_specs=out_specs,
        out_shape=out_shape,
        compiler_params=_params(("parallel",), blocks, temps),
        name="in_projection",
    )(*args)


def _pool_kernel(prev_ref, cur_ref, next_ref, w_ref, sc_ref, o_ref, *, n, ts):
    tiles_per_seq = n // ts
    s = pl.program_id(0) % tiles_per_seq
    cur = cur_ref[...]
    prev = jnp.where(s == 0, 0.0, prev_ref[...])
    nxt = jnp.where(s == tiles_per_seq - 1, 0.0, next_ref[...])
    ext = jnp.concatenate([prev, cur, nxt], axis=0)
    rows = ts + 2 * POOL_HALO
    t = s * ts + lax.broadcasted_iota(jnp.int32, (ts, 1), 0)
    gw = cur.shape[1] // len(POOL_WINDOWS)
    for gi, w in enumerate(POOL_WINDOWS):
        e = ext[:, gi * gw:(gi + 1) * gw]
        width = 1
        while width < w:
            e = e + pltpu.roll(e, rows - width, 0)
            width *= 2
        lead = POOL_HALO - w // 2
        win = (pltpu.roll(e, rows - lead, 0) if lead else e)[:ts]
        lo = jnp.maximum(t - w // 2, 0)
        hi = jnp.minimum(t + w // 2 - 1, n - 1)
        diff = win / (hi - lo + 1).astype(F32) - cur[:, gi * gw:(gi + 1) * gw]
        y = jnp.dot(diff.astype(BF16), w_ref[gi], preferred_element_type=F32)
        o_ref[:, gi * gw:(gi + 1) * gw] = (y * sc_ref[:, gi * gw:(gi + 1) * gw]).astype(o_ref.dtype)


def pool_mix(px, n, w_pool, scale, *, ts=512):
    r, c = px.shape
    ts = min(ts, n)
    hb = ts // POOL_HALO
    last_halo = r // POOL_HALO - 1
    blocks = 2 * _nbytes((ts, c), F32) + _nbytes(w_pool.shape, BF16)
    return pl.pallas_call(
        functools.partial(_pool_kernel, n=n, ts=ts),
        grid=(r // ts,),
        in_specs=[pl.BlockSpec((POOL_HALO, c), lambda i: (jnp.maximum(i * hb - 1, 0), 0)),
                  pl.BlockSpec((ts, c), lambda i: (i, 0)),
                  pl.BlockSpec((POOL_HALO, c), lambda i: (jnp.minimum((i + 1) * hb, last_halo), 0)),
                  pl.BlockSpec(w_pool.shape, lambda i: (0, 0, 0)),
                  pl.BlockSpec((1, c), lambda i: (0, 0))],
        out_specs=pl.BlockSpec((ts, c), lambda i: (i, 0)),
        out_shape=jax.ShapeDtypeStruct((r, c), BF16),
        compiler_params=_params(("parallel",), blocks, 6 * _nbytes((ts + 2 * POOL_HALO, c), F32)),
        name="pool_mix",
    )(px, px, px, w_pool, scale.reshape(1, c))


def _dft_chan_kernel(u_ref, c_ref, s_ref, w_ref, ap_ref, aq_ref, alt_ref, pq_ref, *, norm, tiles_per_seq):
    gw = c_ref.shape[0]
    i = pl.program_id(0)

    @pl.when(i == 0)
    def _():
        for g in range(FNET_GROUPS):
            w = w_ref[g]
            p = jnp.dot(c_ref[...], w, preferred_element_type=F32, precision=lax.Precision.HIGHEST)
            q = jnp.dot(s_ref[...], w, preferred_element_type=F32, precision=lax.Precision.HIGHEST)
            pq_ref[g] = (jnp.concatenate([p, -q], axis=-1) * norm).astype(BF16)

    sign = (1 - 2 * (lax.broadcasted_iota(jnp.int32, (u_ref.shape[0], 1), 0) % 2)).astype(F32)
    alt = []
    for g in range(FNET_GROUPS):
        a = jnp.dot(u_ref[:, g * gw:(g + 1) * gw], pq_ref[g], preferred_element_type=F32)
        ap_ref[:, g * gw:(g + 1) * gw] = a[:, :gw].astype(BF16)
        aq_ref[:, g * gw:(g + 1) * gw] = a[:, gw:].astype(BF16)
        alt.append(jnp.sum(a[:, :gw] * sign, axis=0, keepdims=True))

    @pl.when(i % tiles_per_seq == 0)
    def _():
        alt_ref[...] = jnp.zeros_like(alt_ref)

    alt_ref[...] += jnp.concatenate(alt, axis=-1)


def _dft_seq_kernel(c_ref, s_ref, ap_ref, aq_ref, alt_ref, lo_ref, hi_ref, g_ref, h_ref, carry_ref, *, nn):
    i, j = pl.program_id(0), pl.program_id(1)
    nb, tk, _ = g_ref.shape

    @pl.when(j == 0)
    def _():
        g_ref[...] = jnp.zeros_like(g_ref)
        h_ref[...] = jnp.zeros_like(h_ref)

    @pl.when((i == 0) & (j == 0))
    def _():
        carry_ref[...] = alt_ref[...]

    for b in range(nb):
        g_ref[b] += jnp.dot(c_ref[...], ap_ref[b], preferred_element_type=F32)
        h_ref[b] += jnp.dot(s_ref[...], aq_ref[b], preferred_element_type=F32)

    @pl.when(j == nn - 1)
    def _():
        lo_ref[...] = (g_ref[...] + h_ref[...]).astype(lo_ref.dtype)
        row = lax.broadcasted_iota(jnp.int32, (tk, tk), 0)
        col = lax.broadcasted_iota(jnp.int32, (tk, tk), 1)
        mirror = ((row + col) % tk == 0).astype(BF16)
        first = lax.broadcasted_iota(jnp.int32, (tk, 1), 0) == 0
        for b in range(nb):
            d = g_ref[b] - h_ref[b]
            m = jnp.dot(mirror, d.astype(BF16), preferred_element_type=F32)
            hi_ref[b] = jnp.where(first, carry_ref[b], m).astype(hi_ref.dtype)
            carry_ref[b] = d[0:1, :]


def _dft_tables(n, dtype, rows=None, tile=512):
    rows = n if rows is None else rows

    def cos_sin(cols):
        k = jnp.arange(rows, dtype=jnp.int32)[:, None]
        ang = ((k * cols[None, :]) % n).astype(F32) * (2.0 * np.pi / n)
        return jnp.cos(ang), jnp.sin(ang)

    if n <= tile:
        c, s = cos_sin(jnp.arange(n, dtype=jnp.int32))
        return c.astype(dtype), s.astype(dtype)
    c1, s1 = cos_sin(jnp.arange(tile, dtype=jnp.int32))
    c0, s0 = cos_sin(jnp.arange(n // tile, dtype=jnp.int32) * tile)
    c = c0[:, :, None] * c1[:, None, :] - s0[:, :, None] * s1[:, None, :]
    s = s0[:, :, None] * c1[:, None, :] + c0[:, :, None] * s1[:, None, :]
    return c.reshape(rows, n).astype(dtype), s.reshape(rows, n).astype(dtype)


def fourier_mix(fx, nb, n, w_fnet, seq_tables, chan_tables, *, tm=512, tk=512, tn=512):
    r, c = fx.shape
    gw = c // FNET_GROUPS
    half = n // 2
    tm, tk, tn = min(tm, n), min(tk, half), min(tn, n)
    cc, sc = chan_tables
    tiles_per_seq = n // tm
    ap, aq, alt = pl.pallas_call(
        functools.partial(_dft_chan_kernel, norm=float(1.0 / np.sqrt(n * gw)), tiles_per_seq=tiles_per_seq),
        grid=(r // tm,),
        in_specs=[pl.BlockSpec((tm, c), lambda i: (i, 0)),
                  pl.BlockSpec((gw, gw), lambda i: (0, 0)),
                  pl.BlockSpec((gw, gw), lambda i: (0, 0)),
                  pl.BlockSpec(w_fnet.shape, lambda i: (0, 0, 0))],
        out_specs=[pl.BlockSpec((tm, c), lambda i: (i, 0))] * 2
                  + [pl.BlockSpec((None, 1, c), lambda i: (i // tiles_per_seq, 0, 0))],
        out_shape=[jax.ShapeDtypeStruct((r, c), BF16)] * 2 + [jax.ShapeDtypeStruct((nb, 1, c), F32)],
        scratch_shapes=[pltpu.VMEM((FNET_GROUPS, gw, 2 * gw), BF16)],
        compiler_params=_params(("arbitrary",), 3 * _nbytes((tm, c), BF16) + 6 * _nbytes((gw, gw), F32),
                                4 * _nbytes((tm, c), F32)),
        name="dft_channels",
    )(fx, cc, sc, w_fnet)
    cn, sn = seq_tables
    nn = n // tn
    blocks = 2 * _nbytes((tk, tn), BF16) + 2 * _nbytes((nb, tn, c), BF16) + 2 * _nbytes((nb, tk, c), BF16)
    nt = half // tk
    y_lo, y_hi = pl.pallas_call(
        functools.partial(_dft_seq_kernel, nn=nn),
        grid=(nt, nn),
        in_specs=[pl.BlockSpec((tk, tn), lambda i, j: (nt - 1 - i, j)),
                  pl.BlockSpec((tk, tn), lambda i, j: (nt - 1 - i, j)),
                  pl.BlockSpec((nb, tn, c), lambda i, j: (0, j, 0)),
                  pl.BlockSpec((nb, tn, c), lambda i, j: (0, j, 0)),
                  pl.BlockSpec((nb, 1, c), lambda i, j: (0, 0, 0))],
        out_specs=[pl.BlockSpec((nb, tk, c), lambda i, j: (0, nt - 1 - i, 0)),
                   pl.BlockSpec((nb, tk, c), lambda i, j: (0, i, 0))],
        out_shape=[jax.ShapeDtypeStruct((nb, half, c), BF16)] * 2,
        scratch_shapes=[pltpu.VMEM((nb, tk, c), F32)] * 2 + [pltpu.VMEM((nb, 1, c), F32)],
        compiler_params=_params(("arbitrary", "arbitrary"), blocks, 4 * _nbytes((nb, tk, c), F32)),
        name="dft_positions",
    )(cn, sn, ap.reshape(nb, n, c), aq.reshape(nb, n, c), alt)
    return jnp.concatenate([y_lo, y_hi], axis=1).reshape(r, c)


def _attend(q, segments):
    dn = (((1,), (1,)), ((), ()))
    scores = []
    for k, _, bias in segments:
        s = lax.dot_general(q, k, dn, preferred_element_type=F32)
        scores.append(s if bias is None else s + bias)
    m = functools.reduce(jnp.maximum, [jnp.max(s, axis=-1, keepdims=True) for s in scores])
    probs = [jnp.exp2(s - m) for s in scores]
    l = sum(jnp.sum(p, axis=-1, keepdims=True) for p in probs)
    o = sum(jnp.dot(p.astype(BF16), v, preferred_element_type=F32) for p, (_, v, _) in zip(probs, segments))
    return o / l


def _ctx_attn_kernel(q_ref, kc_ref, vc_ref, o_ref, *, group):
    segments = [(kc_ref[...], vc_ref[...], None)]
    for g in range(group):
        hs = slice(g * HEAD_DIM, (g + 1) * HEAD_DIM)
        o_ref[:, hs] = _attend(q_ref[:, hs], segments).astype(o_ref.dtype)


def context_attention(q, nb, nq, kv_heads, k_ctx, v_ctx, m_ctx):
    group = q.shape[1] // HEAD_DIM // kv_heads
    gd = group * HEAD_DIM
    ctx = pl.BlockSpec((m_ctx, HEAD_DIM), lambda b, h: (b, h))
    blocks = 2 * _nbytes((nq, gd), BF16) + 2 * _nbytes((m_ctx, HEAD_DIM), BF16)
    return pl.pallas_call(
        functools.partial(_ctx_attn_kernel, group=group),
        grid=(nb, kv_heads),
        in_specs=[pl.BlockSpec((nq, gd), lambda b, h: (b, h)), ctx, ctx],
        out_specs=pl.BlockSpec((nq, gd), lambda b, h: (b, h)),
        out_shape=jax.ShapeDtypeStruct(q.shape, BF16),
        compiler_params=_params(("parallel", "parallel"), blocks, 6 * _nbytes((nq, m_ctx), F32)),
        name="context_attention",
    )(q, k_ctx, v_ctx)


def _dense_attn_kernel(*refs, group, n_casts):
    q_ref, kl_ref, vtl_ref, kc_ref, vtc_ref = refs[:5]
    cast_src = refs[5:5 + n_casts]
    o_ref = refs[5 + n_casts]
    cast_dst = refs[6 + n_casts:6 + 2 * n_casts]
    s_ref = refs[6 + 2 * n_casts]
    n_lat, n_ctx = kl_ref.shape[0], kc_ref.shape[0]
    dn = (((1,), (1,)), ((), ()))
    chunks = [(kc_ref, vtc_ref, 0, n_ctx, n_lat)]
    chunks += [(kl_ref, vtl_ref, c, ATTN_KEY_CHUNK, c) for c in range(0, n_lat, ATTN_KEY_CHUNK)]
    qs = [q_ref[:, g * HEAD_DIM:(g + 1) * HEAD_DIM] for g in range(group)]
    m = [None] * group
    for k_ref, _, start, size, row in chunks:
        k = k_ref[start:start + size, :]
        for g in range(group):
            s = lax.dot_general(k, qs[g], dn, preferred_element_type=F32)
            s_ref[g, row:row + size, :] = s
            mc = jnp.max(s, axis=0, keepdims=True)
            m[g] = mc if m[g] is None else jnp.maximum(m[g], mc)
    l = [None] * group
    o = [None] * group
    for _, vt_ref, start, size, row in chunks:
        vt = vt_ref[:, start:start + size]
        for g in range(group):
            p = jnp.exp2(s_ref[g, row:row + size, :] - m[g])
            lc = jnp.sum(p, axis=0, keepdims=True)
            oc = jnp.dot(vt, p.astype(BF16), preferred_element_type=F32)
            l[g] = lc if l[g] is None else l[g] + lc
            o[g] = oc if o[g] is None else o[g] + oc
    for g in range(group):
        o_ref[:, g * HEAD_DIM:(g + 1) * HEAD_DIM] = (o[g] / l[g]).T.astype(o_ref.dtype)
    for src, dst in zip(cast_src, cast_dst):
        dst[...] = src[...].astype(dst.dtype)


ATTN_KEY_CHUNK = 256


def dense_attention(q, nb, nq, kv_heads, k_ctx, v_ctx, m_ctx, k_lat, v_lat, casts=(), *, tq=256):
    heads = q.shape[1] // HEAD_DIM
    group = heads // kv_heads
    nqt = nq // tq
    gd = group * HEAD_DIM
    keys = nq + m_ctx
    qmap = lambda b, h, i: (b * nqt + i, h)
    lat = pl.BlockSpec((nq, HEAD_DIM), lambda b, h, i: (b, h))
    ctx = pl.BlockSpec((m_ctx, HEAD_DIM), lambda b, h, i: (b, h))
    def head_major(v, n_keys):
        return v.reshape(nb, n_keys, kv_heads, HEAD_DIM).transpose(0, 2, 3, 1).reshape(nb * kv_heads * HEAD_DIM, n_keys)
    lat_t = pl.BlockSpec((HEAD_DIM, nq), lambda b, h, i: (b * kv_heads + h, 0))
    ctx_t = pl.BlockSpec((HEAD_DIM, m_ctx), lambda b, h, i: (b * kv_heads + h, 0))
    blocks = 2 * _nbytes((tq, gd), BF16) + 2 * _nbytes((keys, HEAD_DIM), BF16)
    steps = nb * kv_heads * nqt
    cast_specs_in, cast_specs_out, cast_shapes = [], [], []
    for w, layer in casts:
        _, rows_w, cols_w = w.shape
        col_blocks = next(c for c in (1, 2, 4, 8) if steps % c == 0 and rows_w % (steps // c) == 0
                          and (rows_w // (steps // c)) % BF16_SUBLANES == 0 and cols_w % (c * LANES) == 0)
        slab = (rows_w // (steps // col_blocks), cols_w // col_blocks)

        def slab_index(b, h, i, col_blocks=col_blocks):
            step = (b * kv_heads + h) * nqt + i
            return step // col_blocks, step % col_blocks

        cast_specs_in.append(pl.BlockSpec((None,) + slab, lambda b, h, i, f=slab_index, layer=layer: (layer,) + f(b, h, i)))
        cast_specs_out.append(pl.BlockSpec((None,) + slab, lambda b, h, i, f=slab_index: (0,) + f(b, h, i)))
        cast_shapes.append(jax.ShapeDtypeStruct((1, rows_w, cols_w), BF16))
        blocks += _nbytes(slab, F32) + _nbytes(slab, BF16)
    outs = pl.pallas_call(
        functools.partial(_dense_attn_kernel, group=group, n_casts=len(casts)),
        grid=(nb, kv_heads, nqt),
        in_specs=[pl.BlockSpec((tq, gd), qmap), lat, lat_t, ctx, ctx_t] + cast_specs_in,
        out_specs=[pl.BlockSpec((tq, gd), qmap)] + cast_specs_out,
        out_shape=[jax.ShapeDtypeStruct(q.shape, BF16)] + cast_shapes,
        scratch_shapes=[pltpu.VMEM((group, keys, tq), F32)],
        compiler_params=_params(("parallel", "parallel", "parallel"), blocks, 2 * _nbytes((group, keys, tq), F32)),
        name="dense_attention",
    )(q, k_lat, head_major(v_lat, nq), k_ctx, head_major(v_ctx, m_ctx), *[w for w, _ in casts])
    return outs[0], list(outs[1:])


def _na_plan(rows, rq):
    wr = min(NA_WIN_R, rows)
    band = min(rq + wr - 1, rows)
    nblk = rows // rq
    starts = [int(np.clip(rq * j - wr // 2, 0, rows - band)) for j in range(nblk)]
    deltas = [starts[j] - rq * j for j in range(nblk)]
    uniq = sorted(set(deltas))
    var = [uniq.index(dl) for dl in deltas]
    reps = [deltas.index(dl) for dl in uniq]
    return starts, var, band, reps


def _na_bias_kernel(rpb_ref, o_ref, *, rows, rq, band, starts, reps):
    h = pl.program_id(0)
    wr = min(NA_WIN_R, rows)
    qc = lax.broadcasted_iota(jnp.int32, (GRID_W, GRID_W), 0)
    kc = lax.broadcasted_iota(jnp.int32, (GRID_W, GRID_W), 1)
    dc = kc - qc + (NA_WIN_C - 1)
    cs = jnp.clip(qc - NA_WIN_C // 2, 0, GRID_W - NA_WIN_C)
    col_ok = (kc >= cs) & (kc < cs + NA_WIN_C)
    masked = jnp.full((GRID_W, GRID_W), MASK_VALUE, F32)
    by_dr = []
    for dr in range(2 * NA_WIN_R - 1):
        t = masked
        for d in range(2 * NA_WIN_C - 1):
            t = jnp.where(dc == d, rpb_ref[h, dr, d] * LOG2E, t)
        by_dr.append(jnp.where(col_ok, t, MASK_VALUE))
    for v, j in enumerate(reps):
        for e in range(rq):
            qr = rq * j + e
            rs = int(np.clip(qr - wr // 2, 0, rows - wr))
            for a in range(band):
                kr = starts[j] + a
                blk = by_dr[kr - qr + NA_WIN_R - 1] if rs <= kr < rs + wr else masked
                o_ref[v, e * GRID_W:(e + 1) * GRID_W, a * GRID_W:(a + 1) * GRID_W] = blk


def _na_bias(rpb, rows, rq, band, starts, reps):
    nv = len(reps)
    tq, bk = rq * GRID_W, band * GRID_W
    return pl.pallas_call(
        functools.partial(_na_bias_kernel, rows=rows, rq=rq, band=band, starts=starts, reps=reps),
        grid=(NA_HEADS,),
        in_specs=[pl.BlockSpec(memory_space=pltpu.SMEM)],
        out_specs=pl.BlockSpec((nv, None, tq, bk), lambda h: (0, h, 0, 0)),
        out_shape=jax.ShapeDtypeStruct((nv, NA_HEADS, tq, bk), F32),
        compiler_params=_params(("parallel",), _nbytes((nv, tq, bk), F32)),
        name="na_bias",
    )(rpb)


def _na_kernel(start_ref, var_ref, q_ref, k_ref, v_ref, kc_ref, vc_ref, bias_ref, o_ref, *, band_keys):
    del var_ref
    j = pl.program_id(1)
    st = pl.multiple_of(start_ref[j] * GRID_W, GRID_W)
    for h in range(NA_HEADS):
        hs = slice(h * HEAD_DIM, (h + 1) * HEAD_DIM)
        segments = [(k_ref[pl.ds(st, band_keys), hs], v_ref[pl.ds(st, band_keys), hs], bias_ref[h]),
                    (kc_ref[:, hs], vc_ref[:, hs], None)]
        o_ref[:, hs] = _attend(q_ref[:, hs], segments).astype(o_ref.dtype)


def neighbourhood_attention(q, k, v, k_ctx, v_ctx, nb, n, m_ctx, rpb, *, rq=4):
    rows = n // GRID_W
    rq = min(rq, rows)
    starts, var, band, reps = _na_plan(rows, rq)
    bias = _na_bias(rpb, rows, rq, band, starts, reps)
    tq, band_keys = rq * GRID_W, band * GRID_W
    nblk = rows // rq
    c = q.shape[1]
    blocks = (_nbytes((tq, c), BF16) * 2 + 2 * _nbytes((n, c), BF16) + 2 * _nbytes((m_ctx, c), BF16)
              + _nbytes((NA_HEADS, tq, band_keys), F32))
    grid_spec = pltpu.PrefetchScalarGridSpec(
        num_scalar_prefetch=2,
        grid=(nb, nblk),
        in_specs=[pl.BlockSpec((tq, c), lambda b, j, st, vr: (b * nblk + j, 0)),
                  pl.BlockSpec((n, c), lambda b, j, st, vr: (b, 0)),
                  pl.BlockSpec((n, c), lambda b, j, st, vr: (b, 0)),
                  pl.BlockSpec((m_ctx, c), lambda b, j, st, vr: (b, 0)),
                  pl.BlockSpec((m_ctx, c), lambda b, j, st, vr: (b, 0)),
                  pl.BlockSpec((None, NA_HEADS, tq, band_keys), lambda b, j, st, vr: (vr[j], 0, 0, 0))],
        out_specs=pl.BlockSpec((tq, c), lambda b, j, st, vr: (b * nblk + j, 0)),
    )
    return pl.pallas_call(
        functools.partial(_na_kernel, band_keys=band_keys),
        grid_spec=grid_spec,
        out_shape=jax.ShapeDtypeStruct(q.shape, BF16),
        compiler_params=_params(("parallel", "arbitrary"), blocks, 6 * _nbytes((tq, band_keys + m_ctx), F32)),
        name="neighbourhood_attention",
    )(jnp.asarray(starts, jnp.int32), jnp.asarray(var, jnp.int32), q, k, v, k_ctx, v_ctx, bias)


def _outproj_kernel(x_ref, gt_ref, y0_ref, y1_ref, y2_ref, y3_ref, w_ref, o_ref):
    y = jnp.concatenate([y0_ref[...], y1_ref[...], y2_ref[...], y3_ref[...]], axis=-1)
    o_ref[...] = x_ref[...] + gt_ref[...] * jnp.dot(y, w_ref[...], preferred_element_type=F32)


def out_projection(x, mod3, k_gate, row_of_tile, ys, layer, w_out, *, tm=512):
    r, d = x.shape
    tm = min(tm, r)
    row = lambda i: (i, 0)
    blocks = 2 * _nbytes((tm, d), F32) + _nbytes((tm, d), BF16) + _nbytes(w_out.shape[1:], BF16) // 2
    return pl.pallas_call(
        _outproj_kernel,
        grid=(r // tm,),
        in_specs=[pl.BlockSpec((tm, d), row), _mod_spec(d, row_of_tile, k_gate)]
                 + [pl.BlockSpec((tm, MIX_GROUP), row)] * 4
                 + [pl.BlockSpec((None,) + w_out.shape[1:], lambda i: (layer, 0, 0), pipeline_mode=pl.Buffered(1))],
        out_specs=pl.BlockSpec((tm, d), row),
        out_shape=jax.ShapeDtypeStruct((r, d), F32),
        compiler_params=_params(("parallel",), blocks, 3 * _nbytes((tm, d), F32)),
        name="out_projection",
    )(x, mod3, *ys, w_out)


def _rope_tables(n):
    t = jnp.arange(n, dtype=jnp.int32)
    row = (t // GRID_W).astype(F32)
    col = (t % GRID_W).astype(F32)
    n_freq = HEAD_DIM // 4
    inv = 1.0 / (ROPE_THETA ** (jnp.arange(n_freq, dtype=F32) / n_freq))
    ang = jnp.concatenate([row[:, None] * inv, col[:, None] * inv], axis=-1)
    cos = jnp.repeat(jnp.cos(ang), 2, axis=-1)
    sin = jnp.repeat(jnp.sin(ang), 2, axis=-1)
    sign = jnp.tile(jnp.asarray([-1.0, 1.0], F32), HEAD_DIM // 2)
    return cos, sin * sign


def kernel(x, c, ctx, c_ctx, w_mod, b_mod, ffn1_norm, ffn1_gate, ffn1_up, ffn1_down, mix_norm, w_in, w_out,
           pool_w, pool_scale, q_norm, k_norm, fnet_w, na_rpb, ffn2_norm, ffn2_gate, ffn2_up, ffn2_down, final_norm):
    nb, n, d = x.shape
    m = ctx.shape[1]
    depth = w_mod.shape[0]
    assert nb + 1 <= MOD_ROWS

    cvec = jnp.zeros((MOD_ROWS, d), F32).at[:nb].set(c).at[nb].set(c_ctx)
    mod = modulation(cvec, w_mod, b_mod)

    rope = _rope_tables(n)
    seq_tables = _dft_tables(n, BF16, rows=n // 2)
    ctx_tables = _dft_tables(m, BF16, rows=m // 2)
    chan_tables = _dft_tables(MIX_GROUP // FNET_GROUPS, F32)

    f1 = tuple(cast_layer(w, 0) for w in (ffn1_gate, ffn1_up, ffn1_down))
    w_in_b = cast_layer(w_in, 0)
    pool_w_b = pool_w.astype(BF16)
    xl = x.reshape(nb * n, d)
    xc = ctx.reshape(nb * m, d)
    ctx_row = lambda i: nb

    def lat_row(tm):
        return lambda i: (i * tm) // n

    for l in range(depth):
        last = l == depth - 1
        mod3 = mod[l].reshape(MOD_ROWS, 1, N_MOD * d)

        xl = ffn(xl, mod3, 0, lat_row(FFN_TM), ffn1_norm[l], 0, *f1, tm=FFN_TM)
        xc = ffn(xc, mod3, 0, ctx_row, ffn1_norm[l], 0, *f1, tm=FFN_TM)

        px, gq, gk, gv, fx, nq, nk, nv = in_projection(
            xl, mod3, 3, lat_row(512), mix_norm[l], 0, w_in_b, q_norm[l], k_norm[l], rope)
        pc, gqc, gkc, gvc, fc, nqc, nkc, nvc = in_projection(
            xc, mod3, 3, ctx_row, mix_norm[l], 0, w_in_b, q_norm[l], k_norm[l])

        casts = [(w, l) for w in (w_out, ffn2_gate, ffn2_up, ffn2_down)]
        if not last:
            casts += [(w, l + 1) for w in (ffn1_gate, ffn1_up, ffn1_down, w_in)]
        y_gqa, cast_w = dense_attention(gq, nb, n, GQA_KV_HEADS, gkc, gvc, m, gk, gv, casts)
        w_out_b, f2 = cast_w[0], cast_w[1:4]
        if not last:
            f1, w_in_b = cast_w[4:7], cast_w[7]

        ys = (pool_mix(px, n, pool_w_b[l], pool_scale[l]),
              y_gqa,
              fourier_mix(fx, nb, n, fnet_w[l], seq_tables, chan_tables),
              neighbourhood_attention(nq, nk, nv, nkc, nvc, nb, n, m, na_rpb[l]))
        xl = out_projection(xl, mod3, 5, lat_row(512), ys, 0, w_out_b)

        if not last:
            ycs = (pool_mix(pc, m, pool_w_b[l], pool_scale[l]),
                   context_attention(gqc, nb, m, GQA_KV_HEADS, gkc, gvc, m),
                   fourier_mix(fc, nb, m, fnet_w[l], ctx_tables, chan_tables),
                   context_attention(nqc, nb, m, NA_HEADS, nkc, nvc, m))
            xc = out_projection(xc, mod3, 5, ctx_row, ycs, 0, w_out_b)
            xc = ffn(xc, mod3, 6, ctx_row, ffn2_norm[l], 0, *f2, tm=FFN_TM)

        xl = ffn(xl, mod3, 6, lat_row(FFN_TM), ffn2_norm[l], 0, *f2, final_g=final_norm if last else None, tm=FFN_TM)
    return xl.reshape(nb, n, d)
```

```python
import functools

import numpy as np
import jax
import jax.numpy as jnp
from jax import lax
from jax.experimental import pallas as pl
from jax.experimental.pallas import tpu as pltpu

F32 = jnp.float32
BF16 = jnp.bfloat16

GRID_W = 64
HEAD_DIM = 128
POOL_WINDOWS = (2, 4, 8, 16)
POOL_HALO = 8
GQA_Q_HEADS = 4
GQA_KV_HEADS = 2
FNET_GROUPS = 4
NA_HEADS = 4
NA_WIN_R = 8
NA_WIN_C = 16
ROPE_THETA = 10000.0
IN_SPLITS = (512, 512, 256, 256, 512, 512, 512, 512)
MIX_GROUP = 512
EPS = 1e-6
N_MOD = 9
MOD_ROWS = 8
MASK_VALUE = -1e30
LOG2E = float(np.log2(np.e))

V7X_VMEM_BYTES = 64 * 1024 * 1024
VMEM_CAP = V7X_VMEM_BYTES - 4 * 1024 * 1024
FFN_TM = 1024
BF16_SUBLANES = 16
LANES = 128


def _params(semantics, block_bytes, scratch_bytes=0):
    need = 2 * block_bytes + scratch_bytes
    limit = min(max(need + need // 2, 32 * 1024 * 1024), VMEM_CAP)
    return pltpu.CompilerParams(dimension_semantics=semantics, vmem_limit_bytes=limit)


def _nbytes(shape, dtype):
    return int(np.prod(shape)) * jnp.dtype(dtype).itemsize


def _norm_modulate(x, gain, shift, scale):
    inv = lax.rsqrt(jnp.mean(x * x, axis=-1, keepdims=True) + EPS)
    return (x * inv) * (gain * (1.0 + scale)) + shift


def _mod_spec(d, row_of_tile, k):
    return pl.BlockSpec((None, 1, d), lambda i, *_: (row_of_tile(i), 0, k))


def _mod_kernel(c_ref, w_ref, b_ref, o_ref):
    c = c_ref[...]
    s = c * jax.nn.sigmoid(c)
    o_ref[...] = jnp.dot(s.astype(BF16), w_ref[...].astype(BF16), preferred_element_type=F32) + b_ref[...]


def modulation(cvec, w_mod, b_mod, *, tn=512):
    nl, d, nd = w_mod.shape
    blocks = _nbytes((d, tn), F32) + _nbytes((MOD_ROWS, d), F32) + 2 * _nbytes((MOD_ROWS, tn), F32)
    return pl.pallas_call(
        _mod_kernel,
        grid=(nl, nd // tn),
        in_specs=[pl.BlockSpec((MOD_ROWS, d), lambda l, j: (0, 0)),
                  pl.BlockSpec((None, d, tn), lambda l, j: (l, 0, j)),
                  pl.BlockSpec((None, 1, tn), lambda l, j: (l, 0, j))],
        out_specs=pl.BlockSpec((None, MOD_ROWS, tn), lambda l, j: (l, 0, j)),
        out_shape=jax.ShapeDtypeStruct((nl, MOD_ROWS, nd), F32),
        compiler_params=_params(("parallel", "parallel"), blocks, _nbytes((d, tn), BF16)),
        name="modulation",
    )(cvec, w_mod, b_mod.reshape(nl, 1, nd))


def _cast_kernel(src_ref, dst_ref):
    dst_ref[...] = src_ref[...].astype(dst_ref.dtype)


def cast_layer(w, layer, *, rows=256):
    _, a, b = w.shape
    rows = min(rows, a)
    return pl.pallas_call(
        _cast_kernel,
        grid=(a // rows,),
        in_specs=[pl.BlockSpec((None, rows, b), lambda i: (layer, i, 0))],
        out_specs=pl.BlockSpec((None, rows, b), lambda i: (0, i, 0)),
        out_shape=jax.ShapeDtypeStruct((1, a, b), BF16),
        compiler_params=_params(("parallel",), _nbytes((rows, b), F32) + _nbytes((rows, b), BF16)),
        name="cast_layer",
    )(w)


def _ffn_kernel(*refs, nj, final):
    if final:
        x_ref, sh_ref, sc_ref, gt_ref, ng_ref, wg_ref, wu_ref, wd_ref, fg_ref, o_ref, h_ref = refs
    else:
        x_ref, sh_ref, sc_ref, gt_ref, ng_ref, wg_ref, wu_ref, wd_ref, o_ref, h_ref = refs
    j = pl.program_id(1)

    def partial_out():
        h = h_ref[...]
        g = jnp.dot(h, wg_ref[...], preferred_element_type=F32)
        u = jnp.dot(h, wu_ref[...], preferred_element_type=F32)
        a = (g * jax.nn.sigmoid(g) * u).astype(BF16)
        return jnp.dot(a, wd_ref[...], preferred_element_type=F32)

    @pl.when(j == 0)
    def _():
        h_ref[...] = _norm_modulate(x_ref[...], ng_ref[...], sh_ref[...], sc_ref[...]).astype(BF16)
        o_ref[...] = partial_out()

    @pl.when(j > 0)
    def _():
        o_ref[...] += partial_out()

    @pl.when(j == nj - 1)
    def _():
        r = x_ref[...] + 0.5 * gt_ref[...] * o_ref[...]
        if final:
            r = r * lax.rsqrt(jnp.mean(r * r, axis=-1, keepdims=True) + EPS) * fg_ref[...]
        o_ref[...] = r


def ffn(x, mod3, k0, row_of_tile, norm_g, layer, wg, wu, wd, final_g=None, *, tm=512, tf=512):
    r, d = x.shape
    f = wg.shape[2]
    tm = min(tm, r)
    nj = f // tf
    final = final_g is not None
    row = lambda i, j: (i, 0)
    in_specs = [pl.BlockSpec((tm, d), row),
                _mod_spec(d, row_of_tile, k0), _mod_spec(d, row_of_tile, k0 + 1), _mod_spec(d, row_of_tile, k0 + 2),
                pl.BlockSpec((1, d), lambda i, j: (0, 0)),
                pl.BlockSpec((None, d, tf), lambda i, j: (layer, 0, j)),
                pl.BlockSpec((None, d, tf), lambda i, j: (layer, 0, j)),
                pl.BlockSpec((None, tf, d), lambda i, j: (layer, j, 0))]
    args = [x, mod3, mod3, mod3, norm_g.reshape(1, d), wg, wu, wd]
    if final:
        in_specs.append(pl.BlockSpec((1, d), lambda i, j: (0, 0)))
        args.append(final_g.reshape(1, d))
    blocks = 2 * _nbytes((tm, d), F32) + 3 * _nbytes((d, tf), BF16) + 5 * _nbytes((1, d), F32)
    temps = _nbytes((tm, d), BF16) + 4 * _nbytes((tm, tf), F32) + _nbytes((tm, d), F32)
    return pl.pallas_call(
        functools.partial(_ffn_kernel, nj=nj, final=final),
        grid=(r // tm, nj),
        in_specs=in_specs,
        out_specs=pl.BlockSpec((tm, d), row),
        out_shape=jax.ShapeDtypeStruct((r, d), F32),
        scratch_shapes=[pltpu.VMEM((tm, d), BF16)],
        compiler_params=_params(("parallel", "arbitrary"), blocks, temps),
        name="ffn",
    )(*args)


def _rms_heads(z, gain, nheads):
    outs = []
    for hd in range(nheads):
        zh = z[:, hd * HEAD_DIM:(hd + 1) * HEAD_DIM]
        outs.append(zh * lax.rsqrt(jnp.mean(zh * zh, axis=-1, keepdims=True) + EPS) * gain)
    return outs


def _rope(zh, cos, sin_signed):
    lane = lax.broadcasted_iota(jnp.int32, zh.shape, 1)
    partner = jnp.where(lane % 2 == 0, pltpu.roll(zh, HEAD_DIM - 1, 1), pltpu.roll(zh, 1, 1))
    return zh * cos + partner * sin_signed


def _inproj_kernel(*refs, rope):
    if rope:
        (x_ref, sh_ref, sc_ref, ng_ref, w_ref, qn_ref, kn_ref, cos_ref, sin_ref,
         px_ref, gq_ref, gk_ref, gv_ref, fx_ref, nq_ref, nk_ref, nv_ref) = refs
    else:
        (x_ref, sh_ref, sc_ref, ng_ref, w_ref, qn_ref, kn_ref,
         px_ref, gq_ref, gk_ref, gv_ref, fx_ref, nq_ref, nk_ref, nv_ref) = refs
    h = _norm_modulate(x_ref[...], ng_ref[...], sh_ref[...], sc_ref[...]).astype(BF16)
    offs = np.cumsum((0,) + IN_SPLITS)

    def proj(s):
        return jnp.dot(h, w_ref[:, offs[s]:offs[s + 1]], preferred_element_type=F32)

    sm_scale = HEAD_DIM ** -0.5 * LOG2E
    px_ref[...] = proj(0)
    q_heads = _rms_heads(proj(1), qn_ref[...], GQA_Q_HEADS)
    k_heads = _rms_heads(proj(2), kn_ref[...], GQA_KV_HEADS)
    if rope:
        cos, sin = cos_ref[...], sin_ref[...]
        q_heads = [_rope(zh, cos, sin) for zh in q_heads]
        k_heads = [_rope(zh, cos, sin) for zh in k_heads]
    gq_ref[...] = jnp.concatenate([zh * sm_scale for zh in q_heads], axis=-1).astype(BF16)
    gk_ref[...] = jnp.concatenate(k_heads, axis=-1).astype(BF16)
    gv_ref[...] = proj(3).astype(BF16)
    fx_ref[...] = proj(4).astype(BF16)
    nq_ref[...] = (proj(5) * sm_scale).astype(BF16)
    nk_ref[...] = proj(6).astype(BF16)
    nv_ref[...] = proj(7).astype(BF16)


def in_projection(x, mod3, k0, row_of_tile, norm_g, layer, w_in, q_norm, k_norm, rope_tables=None, *, tm=512):
    r, d = x.shape
    tm = min(tm, r)
    rope = rope_tables is not None
    row = lambda i: (i, 0)
    const = lambda i: (0, 0)
    in_specs = [pl.BlockSpec((tm, d), row),
                _mod_spec(d, row_of_tile, k0), _mod_spec(d, row_of_tile, k0 + 1),
                pl.BlockSpec((1, d), const),
                pl.BlockSpec((None,) + w_in.shape[1:], lambda i: (layer, 0, 0), pipeline_mode=pl.Buffered(1)),
                pl.BlockSpec((1, HEAD_DIM), const), pl.BlockSpec((1, HEAD_DIM), const)]
    args = [x, mod3, mod3, norm_g.reshape(1, d), w_in, q_norm.reshape(1, HEAD_DIM), k_norm.reshape(1, HEAD_DIM)]
    if rope:
        tiles_per_seq = rope_tables[0].shape[0] // tm
        tab = pl.BlockSpec((tm, HEAD_DIM), lambda i: (i % tiles_per_seq, 0))
        in_specs += [tab, tab]
        args += list(rope_tables)
    widths = (512, 512, 256, 256, 512, 512, 512, 512)
    dtypes = (F32,) + (BF16,) * 7
    out_specs = [pl.BlockSpec((tm, w), row) for w in widths]
    out_shape = [jax.ShapeDtypeStruct((r, w), dt) for w, dt in zip(widths, dtypes)]
    blocks = _nbytes((tm, d), F32) + _nbytes(w_in.shape[1:], BF16) // 2 + _nbytes((tm, 4096), F32)
    temps = _nbytes((tm, d), BF16) + 2 * _nbytes((tm, d), F32) + 6 * _nbytes((tm, 512), F32)
    return pl.pallas_call(
        functools.partial(_inproj_kernel, rope=rope),
        grid=(r // tm,),
        in_specs=in_specs,
        out_specs=out_specs,
        out_shape=out_shape,
        compiler_params=_params(("parallel",), blocks, temps),
        name="in_projection",
    )(*args)


def _pool_kernel(prev_ref, cur_ref, next_ref, w_ref, sc_ref, o_ref, *, n, ts):
    tiles_per_seq = n // ts
    s = pl.program_id(0) % tiles_per_seq
    cur = cur_ref[...]
    prev = jnp.where(s == 0, 0.0, prev_ref[...])
    nxt = jnp.where(s == tiles_per_seq - 1, 0.0, next_ref[...])
    ext = jnp.concatenate([prev, cur, nxt], axis=0)
    rows = ts + 2 * POOL_HALO
    t = s * ts + lax.broadcasted_iota(jnp.int32, (ts, 1), 0)
    gw = cur.shape[1] // len(POOL_WINDOWS)
    for gi, w in enumerate(POOL_WINDOWS):
        e = ext[:, gi * gw:(gi + 1) * gw]
        width = 1
        while width < w:
            e = e + pltpu.roll(e, rows - width, 0)
            width *= 2
        lead = POOL_HALO - w // 2
        win = (pltpu.roll(e, rows - lead, 0) if lead else e)[:ts]
        lo = jnp.maximum(t - w // 2, 0)
        hi = jnp.minimum(t + w // 2 - 1, n - 1)
        diff = win / (hi - lo + 1).astype(F32) - cur[:, gi * gw:(gi + 1) * gw]
        y = jnp.dot(diff.astype(BF16), w_ref[gi], preferred_element_type=F32)
        o_ref[:, gi * gw:(gi + 1) * gw] = (y * sc_ref[:, gi * gw:(gi + 1) * gw]).astype(o_ref.dtype)


def pool_mix(px, n, w_pool, scale, *, ts=512):
    r, c = px.shape
    ts = min(ts, n)
    hb = ts // POOL_HALO
    last_halo = r // POOL_HALO - 1
    blocks = 2 * _nbytes((ts, c), F32) + _nbytes(w_pool.shape, BF16)
    return pl.pallas_call(
        functools.partial(_pool_kernel, n=n, ts=ts),
        grid=(r // ts,),
        in_specs=[pl.BlockSpec((POOL_HALO, c), lambda i: (jnp.maximum(i * hb - 1, 0), 0)),
                  pl.BlockSpec((ts, c), lambda i: (i, 0)),
                  pl.BlockSpec((POOL_HALO, c), lambda i: (jnp.minimum((i + 1) * hb, last_halo), 0)),
                  pl.BlockSpec(w_pool.shape, lambda i: (0, 0, 0)),
                  pl.BlockSpec((1, c), lambda i: (0, 0))],
        out_specs=pl.BlockSpec((ts, c), lambda i: (i, 0)),
        out_shape=jax.ShapeDtypeStruct((r, c), BF16),
        compiler_params=_params(("parallel",), blocks, 6 * _nbytes((ts + 2 * POOL_HALO, c), F32)),
        name="pool_mix",
    )(px, px, px, w_pool, scale.reshape(1, c))


def _dft_chan_kernel(u_ref, c_ref, s_ref, w_ref, ap_ref, aq_ref, alt_ref, pq_ref, *, norm, tiles_per_seq):
    gw = c_ref.shape[0]
    i = pl.program_id(0)

    @pl.when(i == 0)
    def _():
        for g in range(FNET_GROUPS):
            w = w_ref[g]
            p = jnp.dot(c_ref[...], w, preferred_element_type=F32, precision=lax.Precision.HIGHEST)
            q = jnp.dot(s_ref[...], w, preferred_element_type=F32, precision=lax.Precision.HIGHEST)
            pq_ref[g] = (jnp.concatenate([p, -q], axis=-1) * norm).astype(BF16)

    sign = (1 - 2 * (lax.broadcasted_iota(jnp.int32, (u_ref.shape[0], 1), 0) % 2)).astype(F32)
    alt = []
    for g in range(FNET_GROUPS):
        a = jnp.dot(u_ref[:, g * gw:(g + 1) * gw], pq_ref[g], preferred_element_type=F32)
        ap_ref[:, g * gw:(g + 1) * gw] = a[:, :gw].astype(BF16)
        aq_ref[:, g * gw:(g + 1) * gw] = a[:, gw:].astype(BF16)
        alt.append(jnp.sum(a[:, :gw] * sign, axis=0, keepdims=True))

    @pl.when(i % tiles_per_seq == 0)
    def _():
        alt_ref[...] = jnp.zeros_like(alt_ref)

    alt_ref[...] += jnp.concatenate(alt, axis=-1)


def _dft_seq_kernel(c_ref, s_ref, ap_ref, aq_ref, alt_ref, lo_ref, hi_ref, g_ref, h_ref, carry_ref, *, nn):
    i, j = pl.program_id(0), pl.program_id(1)
    nb, tk, _ = g_ref.shape

    @pl.when(j == 0)
    def _():
        g_ref[...] = jnp.zeros_like(g_ref)
        h_ref[...] = jnp.zeros_like(h_ref)

    @pl.when((i == 0) & (j == 0))
    def _():
        carry_ref[...] = alt_ref[...]

    for b in range(nb):
        g_ref[b] += jnp.dot(c_ref[...], ap_ref[b], preferred_element_type=F32)
        h_ref[b] += jnp.dot(s_ref[...], aq_ref[b], preferred_element_type=F32)

    @pl.when(j == nn - 1)
    def _():
        lo_ref[...] = (g_ref[...] + h_ref[...]).astype(lo_ref.dtype)
        row = lax.broadcasted_iota(jnp.int32, (tk, tk), 0)
        col = lax.broadcasted_iota(jnp.int32, (tk, tk), 1)
        mirror = ((row + col) % tk == 0).astype(BF16)
        first = lax.broadcasted_iota(jnp.int32, (tk, 1), 0) == 0
        for b in range(nb):
            d = g_ref[b] - h_ref[b]
            m = jnp.dot(mirror, d.astype(BF16), preferred_element_type=F32)
            hi_ref[b] = jnp.where(first, carry_ref[b], m).astype(hi_ref.dtype)
            carry_ref[b] = d[0:1, :]


def _dft_tables(n, dtype, rows=None, tile=512):
    rows = n if rows is None else rows

    def cos_sin(cols):
        k = jnp.arange(rows, dtype=jnp.int32)[:, None]
        ang = ((k * cols[None, :]) % n).astype(F32) * (2.0 * np.pi / n)
        return jnp.cos(ang), jnp.sin(ang)

    if n <= tile:
        c, s = cos_sin(jnp.arange(n, dtype=jnp.int32))
        return c.astype(dtype), s.astype(dtype)
    c1, s1 = cos_sin(jnp.arange(tile, dtype=jnp.int32))
    c0, s0 = cos_sin(jnp.arange(n // tile, dtype=jnp.int32) * tile)
    c = c0[:, :, None] * c1[:, None, :] - s0[:, :, None] * s1[:, None, :]
    s = s0[:, :, None] * c1[:, None, :] + c0[:, :, None] * s1[:, None, :]
    return c.reshape(rows, n).astype(dtype), s.reshape(rows, n).astype(dtype)


def fourier_mix(fx, nb, n, w_fnet, seq_tables, chan_tables, *, tm=512, tk=512, tn=512):
    r, c = fx.shape
    gw = c // FNET_GROUPS
    half = n // 2
    tm, tk, tn = min(tm, n), min(tk, half), min(tn, n)
    cc, sc = chan_tables
    tiles_per_seq = n // tm
    ap, aq, alt = pl.pallas_call(
        functools.partial(_dft_chan_kernel, norm=float(1.0 / np.sqrt(n * gw)), tiles_per_seq=tiles_per_seq),
        grid=(r // tm,),
        in_specs=[pl.BlockSpec((tm, c), lambda i: (i, 0)),
                  pl.BlockSpec((gw, gw), lambda i: (0, 0)),
                  pl.BlockSpec((gw, gw), lambda i: (0, 0)),
                  pl.BlockSpec(w_fnet.shape, lambda i: (0, 0, 0))],
        out_specs=[pl.BlockSpec((tm, c), lambda i: (i, 0))] * 2
                  + [pl.BlockSpec((None, 1, c), lambda i: (i // tiles_per_seq, 0, 0))],
        out_shape=[jax.ShapeDtypeStruct((r, c), BF16)] * 2 + [jax.ShapeDtypeStruct((nb, 1, c), F32)],
        scratch_shapes=[pltpu.VMEM((FNET_GROUPS, gw, 2 * gw), BF16)],
        compiler_params=_params(("arbitrary",), 3 * _nbytes((tm, c), BF16) + 6 * _nbytes((gw, gw), F32),
                                4 * _nbytes((tm, c), F32)),
        name="dft_channels",
    )(fx, cc, sc, w_fnet)
    cn, sn = seq_tables
    nn = n // tn
    blocks = 2 * _nbytes((tk, tn), BF16) + 2 * _nbytes((nb, tn, c), BF16) + 2 * _nbytes((nb, tk, c), BF16)
    nt = half // tk
    y_lo, y_hi = pl.pallas_call(
        functools.partial(_dft_seq_kernel, nn=nn),
        grid=(nt, nn),
        in_specs=[pl.BlockSpec((tk, tn), lambda i, j: (nt - 1 - i, j)),
                  pl.BlockSpec((tk, tn), lambda i, j: (nt - 1 - i, j)),
                  pl.BlockSpec((nb, tn, c), lambda i, j: (0, j, 0)),
                  pl.BlockSpec((nb, tn, c), lambda i, j: (0, j, 0)),
                  pl.BlockSpec((nb, 1, c), lambda i, j: (0, 0, 0))],
        out_specs=[pl.BlockSpec((nb, tk, c), lambda i, j: (0, nt - 1 - i, 0)),
                   pl.BlockSpec((nb, tk, c), lambda i, j: (0, i, 0))],
        out_shape=[jax.ShapeDtypeStruct((nb, half, c), BF16)] * 2,
        scratch_shapes=[pltpu.VMEM((nb, tk, c), F32)] * 2 + [pltpu.VMEM((nb, 1, c), F32)],
        compiler_params=_params(("arbitrary", "arbitrary"), blocks, 4 * _nbytes((nb, tk, c), F32)),
        name="dft_positions",
    )(cn, sn, ap.reshape(nb, n, c), aq.reshape(nb, n, c), alt)
    return jnp.concatenate([y_lo, y_hi], axis=1).reshape(r, c)


def _attend(q, segments):
    dn = (((1,), (1,)), ((), ()))
    scores = []
    for k, _, bias in segments:
        s = lax.dot_general(q, k, dn, preferred_element_type=F32)
        scores.append(s if bias is None else s + bias)
    m = functools.reduce(jnp.maximum, [jnp.max(s, axis=-1, keepdims=True) for s in scores])
    probs = [jnp.exp2(s - m) for s in scores]
    l = sum(jnp.sum(p, axis=-1, keepdims=True) for p in probs)
    o = sum(jnp.dot(p.astype(BF16), v, preferred_element_type=F32) for p, (_, v, _) in zip(probs, segments))
    return o / l


def _ctx_attn_kernel(q_ref, kc_ref, vc_ref, o_ref, *, group):
    segments = [(kc_ref[...], vc_ref[...], None)]
    for g in range(group):
        hs = slice(g * HEAD_DIM, (g + 1) * HEAD_DIM)
        o_ref[:, hs] = _attend(q_ref[:, hs], segments).astype(o_ref.dtype)


def context_attention(q, nb, nq, kv_heads, k_ctx, v_ctx, m_ctx):
    group = q.shape[1] // HEAD_DIM // kv_heads
    gd = group * HEAD_DIM
    ctx = pl.BlockSpec((m_ctx, HEAD_DIM), lambda b, h: (b, h))
    blocks = 2 * _nbytes((nq, gd), BF16) + 2 * _nbytes((m_ctx, HEAD_DIM), BF16)
    return pl.pallas_call(
        functools.partial(_ctx_attn_kernel, group=group),
        grid=(nb, kv_heads),
        in_specs=[pl.BlockSpec((nq, gd), lambda b, h: (b, h)), ctx, ctx],
        out_specs=pl.BlockSpec((nq, gd), lambda b, h: (b, h)),
        out_shape=jax.ShapeDtypeStruct(q.shape, BF16),
        compiler_params=_params(("parallel", "parallel"), blocks, 6 * _nbytes((nq, m_ctx), F32)),
        name="context_attention",
    )(q, k_ctx, v_ctx)


def _dense_attn_kernel(*refs, group, n_casts):
    q_ref, kl_ref, vtl_ref, kc_ref, vtc_ref = refs[:5]
    cast_src = refs[5:5 + n_casts]
    o_ref = refs[5 + n_casts]
    cast_dst = refs[6 + n_casts:6 + 2 * n_casts]
    s_ref = refs[6 + 2 * n_casts]
    dn = (((1,), (1,)), ((), ()))
    k = jnp.concatenate([kl_ref[...], kc_ref[...]], axis=0)
    vt = jnp.concatenate([vtl_ref[...], vtc_ref[...]], axis=1)
    m = []
    for g in range(group):
        s = lax.dot_general(k, q_ref[:, g * HEAD_DIM:(g + 1) * HEAD_DIM], dn, preferred_element_type=F32)
        s_ref[g] = s
        m.append(jnp.max(s, axis=0, keepdims=True))
    for g in range(group):
        p = jnp.exp2(s_ref[g] - m[g])
        l = jnp.sum(p, axis=0, keepdims=True)
        o = jnp.dot(vt, p.astype(BF16), preferred_element_type=F32)
        o_ref[:, g * HEAD_DIM:(g + 1) * HEAD_DIM] = (o / l).T.astype(o_ref.dtype)
    for src, dst in zip(cast_src, cast_dst):
        dst[...] = src[...].astype(dst.dtype)


def dense_attention(q, nb, nq, kv_heads, k_ctx, v_ctx, m_ctx, k_lat, v_lat, casts=(), *, tq=256):
    heads = q.shape[1] // HEAD_DIM
    group = heads // kv_heads
    nqt = nq // tq
    gd = group * HEAD_DIM
    keys = nq + m_ctx
    qmap = lambda b, h, i: (b * nqt + i, h)
    lat = pl.BlockSpec((nq, HEAD_DIM), lambda b, h, i: (b, h))
    ctx = pl.BlockSpec((m_ctx, HEAD_DIM), lambda b, h, i: (b, h))
    def head_major(v, n_keys):
        return v.reshape(nb, n_keys, kv_heads, HEAD_DIM).transpose(0, 2, 3, 1).reshape(nb * kv_heads * HEAD_DIM, n_keys)

    lat_t = pl.BlockSpec((HEAD_DIM, nq), lambda b, h, i: (b * kv_heads + h, 0))
    ctx_t = pl.BlockSpec((HEAD_DIM, m_ctx), lambda b, h, i: (b * kv_heads + h, 0))
    blocks = 2 * _nbytes((tq, gd), BF16) + 2 * _nbytes((keys, HEAD_DIM), BF16)
    steps = nb * kv_heads * nqt
    cast_specs_in, cast_specs_out, cast_shapes = [], [], []
    for w, layer in casts:
        _, rows_w, cols_w = w.shape
        col_blocks = next(c for c in (1, 2, 4, 8) if steps % c == 0 and rows_w % (steps // c) == 0
                          and (rows_w // (steps // c)) % BF16_SUBLANES == 0 and cols_w % (c * LANES) == 0)
        slab = (rows_w // (steps // col_blocks), cols_w // col_blocks)

        def slab_index(b, h, i, col_blocks=col_blocks):
            step = (b * kv_heads + h) * nqt + i
            return step // col_blocks, step % col_blocks

        cast_specs_in.append(pl.BlockSpec((None,) + slab, lambda b, h, i, f=slab_index, layer=layer: (layer,) + f(b, h, i)))
        cast_specs_out.append(pl.BlockSpec((None,) + slab, lambda b, h, i, f=slab_index: (0,) + f(b, h, i)))
        cast_shapes.append(jax.ShapeDtypeStruct((1, rows_w, cols_w), BF16))
        blocks += _nbytes(slab, F32) + _nbytes(slab, BF16)
    outs = pl.pallas_call(
        functools.partial(_dense_attn_kernel, group=group, n_casts=len(casts)),
        grid=(nb, kv_heads, nqt),
        in_specs=[pl.BlockSpec((tq, gd), qmap), lat, lat_t, ctx, ctx_t] + cast_specs_in,
        out_specs=[pl.BlockSpec((tq, gd), qmap)] + cast_specs_out,
        out_shape=[jax.ShapeDtypeStruct(q.shape, BF16)] + cast_shapes,
        scratch_shapes=[pltpu.VMEM((group, keys, tq), F32)],
        compiler_params=_params(("parallel", "parallel", "parallel"), blocks, 2 * _nbytes((group, keys, tq), F32)),
        name="dense_attention",
    )(q, k_lat, head_major(v_lat, nq), k_ctx, head_major(v_ctx, m_ctx), *[w for w, _ in casts])
    return outs[0], list(outs[1:])


def _na_plan(rows, rq):
    wr = min(NA_WIN_R, rows)
    band = min(rq + wr - 1, rows)
    nblk = rows // rq
    starts = [int(np.clip(rq * j - wr // 2, 0, rows - band)) for j in range(nblk)]
    deltas = [starts[j] - rq * j for j in range(nblk)]
    uniq = sorted(set(deltas))
    var = [uniq.index(dl) for dl in deltas]
    reps = [deltas.index(dl) for dl in uniq]
    return starts, var, band, reps


def _na_bias_kernel(rpb_ref, o_ref, *, rows, rq, band, starts, reps):
    h = pl.program_id(0)
    wr = min(NA_WIN_R, rows)
    qc = lax.broadcasted_iota(jnp.int32, (GRID_W, GRID_W), 0)
    kc = lax.broadcasted_iota(jnp.int32, (GRID_W, GRID_W), 1)
    dc = kc - qc + (NA_WIN_C - 1)
    cs = jnp.clip(qc - NA_WIN_C // 2, 0, GRID_W - NA_WIN_C)
    col_ok = (kc >= cs) & (kc < cs + NA_WIN_C)
    masked = jnp.full((GRID_W, GRID_W), MASK_VALUE, F32)
    by_dr = []
    for dr in range(2 * NA_WIN_R - 1):
        t = masked
        for d in range(2 * NA_WIN_C - 1):
            t = jnp.where(dc == d, rpb_ref[h, dr, d] * LOG2E, t)
        by_dr.append(jnp.where(col_ok, t, MASK_VALUE))
    for v, j in enumerate(reps):
        for e in range(rq):
            qr = rq * j + e
            rs = int(np.clip(qr - wr // 2, 0, rows - wr))
            for a in range(band):
                kr = starts[j] + a
                blk = by_dr[kr - qr + NA_WIN_R - 1] if rs <= kr < rs + wr else masked
                o_ref[v, e * GRID_W:(e + 1) * GRID_W, a * GRID_W:(a + 1) * GRID_W] = blk


def _na_bias(rpb, rows, rq, band, starts, reps):
    nv = len(reps)
    tq, bk = rq * GRID_W, band * GRID_W
    return pl.pallas_call(
        functools.partial(_na_bias_kernel, rows=rows, rq=rq, band=band, starts=starts, reps=reps),
        grid=(NA_HEADS,),
        in_specs=[pl.BlockSpec(memory_space=pltpu.SMEM)],
        out_specs=pl.BlockSpec((nv, None, tq, bk), lambda h: (0, h, 0, 0)),
        out_shape=jax.ShapeDtypeStruct((nv, NA_HEADS, tq, bk), F32),
        compiler_params=_params(("parallel",), _nbytes((nv, tq, bk), F32)),
        name="na_bias",
    )(rpb)


def _na_kernel(start_ref, var_ref, q_ref, k_ref, v_ref, kc_ref, vc_ref, bias_ref, o_ref, s_ref, *, band_keys):
    del var_ref
    j = pl.program_id(1)
    st = pl.multiple_of(start_ref[j] * GRID_W, GRID_W)
    n_ctx = kc_ref.shape[0]
    dn = (((1,), (1,)), ((), ()))
    m = []
    for h in range(NA_HEADS):
        hs = slice(h * HEAD_DIM, (h + 1) * HEAD_DIM)
        q = q_ref[:, hs]
        s_ctx = lax.dot_general(q, kc_ref[:, hs], dn, preferred_element_type=F32)
        s_loc = lax.dot_general(q, k_ref[pl.ds(st, band_keys), hs], dn, preferred_element_type=F32) + bias_ref[h]
        s_ref[h, :, :n_ctx] = s_ctx
        s_ref[h, :, n_ctx:] = s_loc
        m.append(jnp.maximum(jnp.max(s_ctx, axis=-1, keepdims=True), jnp.max(s_loc, axis=-1, keepdims=True)))
    for h in range(NA_HEADS):
        hs = slice(h * HEAD_DIM, (h + 1) * HEAD_DIM)
        p = jnp.exp2(s_ref[h] - m[h])
        l = jnp.sum(p, axis=-1, keepdims=True)
        p = p.astype(BF16)
        o = (jnp.dot(p[:, :n_ctx], vc_ref[:, hs], preferred_element_type=F32)
             + jnp.dot(p[:, n_ctx:], v_ref[pl.ds(st, band_keys), hs], preferred_element_type=F32))
        o_ref[:, hs] = (o / l).astype(o_ref.dtype)


def neighbourhood_attention(q, k, v, k_ctx, v_ctx, nb, n, m_ctx, rpb, *, rq=4):
    rows = n // GRID_W
    rq = min(rq, rows)
    starts, var, band, reps = _na_plan(rows, rq)
    bias = _na_bias(rpb, rows, rq, band, starts, reps)
    tq, band_keys = rq * GRID_W, band * GRID_W
    nblk = rows // rq
    c = q.shape[1]
    blocks = (_nbytes((tq, c), BF16) * 2 + 2 * _nbytes((n, c), BF16) + 2 * _nbytes((m_ctx, c), BF16)
              + _nbytes((NA_HEADS, tq, band_keys), F32))
    grid_spec = pltpu.PrefetchScalarGridSpec(
        num_scalar_prefetch=2,
        grid=(nb, nblk),
        in_specs=[pl.BlockSpec((tq, c), lambda b, j, st, vr: (b * nblk + j, 0)),
                  pl.BlockSpec((n, c), lambda b, j, st, vr: (b, 0)),
                  pl.BlockSpec((n, c), lambda b, j, st, vr: (b, 0)),
                  pl.BlockSpec((m_ctx, c), lambda b, j, st, vr: (b, 0)),
                  pl.BlockSpec((m_ctx, c), lambda b, j, st, vr: (b, 0)),
                  pl.BlockSpec((None, NA_HEADS, tq, band_keys), lambda b, j, st, vr: (vr[j], 0, 0, 0))],
        out_specs=pl.BlockSpec((tq, c), lambda b, j, st, vr: (b * nblk + j, 0)),
        scratch_shapes=[pltpu.VMEM((NA_HEADS, tq, m_ctx + band_keys), F32)],
    )
    return pl.pallas_call(
        functools.partial(_na_kernel, band_keys=band_keys),
        grid_spec=grid_spec,
        out_shape=jax.ShapeDtypeStruct(q.shape, BF16),
        compiler_params=_params(("parallel", "arbitrary"), blocks, 6 * _nbytes((tq, band_keys + m_ctx), F32)),
        name="neighbourhood_attention",
    )(jnp.asarray(starts, jnp.int32), jnp.asarray(var, jnp.int32), q, k, v, k_ctx, v_ctx, bias)


def _outproj_kernel(x_ref, gt_ref, y0_ref, y1_ref, y2_ref, y3_ref, w_ref, o_ref):
    y = jnp.concatenate([y0_ref[...], y1_ref[...], y2_ref[...], y3_ref[...]], axis=-1)
    o_ref[...] = x_ref[...] + gt_ref[...] * jnp.dot(y, w_ref[...], preferred_element_type=F32)


def out_projection(x, mod3, k_gate, row_of_tile, ys, layer, w_out, *, tm=512):
    r, d = x.shape
    tm = min(tm, r)
    row = lambda i: (i, 0)
    blocks = 2 * _nbytes((tm, d), F32) + _nbytes((tm, d), BF16) + _nbytes(w_out.shape[1:], BF16) // 2
    return pl.pallas_call(
        _outproj_kernel,
        grid=(r // tm,),
        in_specs=[pl.BlockSpec((tm, d), row), _mod_spec(d, row_of_tile, k_gate)]
                 + [pl.BlockSpec((tm, MIX_GROUP), row)] * 4
                 + [pl.BlockSpec((None,) + w_out.shape[1:], lambda i: (layer, 0, 0), pipeline_mode=pl.Buffered(1))],
        out_specs=pl.BlockSpec((tm, d), row),
        out_shape=jax.ShapeDtypeStruct((r, d), F32),
        compiler_params=_params(("parallel",), blocks, 3 * _nbytes((tm, d), F32)),
        name="out_projection",
    )(x, mod3, *ys, w_out)


def _rope_tables(n):
    t = jnp.arange(n, dtype=jnp.int32)
    row = (t // GRID_W).astype(F32)
    col = (t % GRID_W).astype(F32)
    n_freq = HEAD_DIM // 4
    inv = 1.0 / (ROPE_THETA ** (jnp.arange(n_freq, dtype=F32) / n_freq))
    ang = jnp.concatenate([row[:, None] * inv, col[:, None] * inv], axis=-1)
    cos = jnp.repeat(jnp.cos(ang), 2, axis=-1)
    sin = jnp.repeat(jnp.sin(ang), 2, axis=-1)
    sign = jnp.tile(jnp.asarray([-1.0, 1.0], F32), HEAD_DIM // 2)
    return cos, sin * sign


def kernel(x, c, ctx, c_ctx, w_mod, b_mod, ffn1_norm, ffn1_gate, ffn1_up, ffn1_down, mix_norm, w_in, w_out,
           pool_w, pool_scale, q_norm, k_norm, fnet_w, na_rpb, ffn2_norm, ffn2_gate, ffn2_up, ffn2_down, final_norm):
    nb, n, d = x.shape
    m = ctx.shape[1]
    depth = w_mod.shape[0]
    assert nb + 1 <= MOD_ROWS

    cvec = jnp.zeros((MOD_ROWS, d), F32).at[:nb].set(c).at[nb].set(c_ctx)
    mod = modulation(cvec, w_mod, b_mod)

    rope = _rope_tables(n)
    seq_tables = _dft_tables(n, BF16, rows=n // 2)
    ctx_tables = _dft_tables(m, BF16, rows=m // 2)
    chan_tables = _dft_tables(MIX_GROUP // FNET_GROUPS, F32)

    f1 = tuple(cast_layer(w, 0) for w in (ffn1_gate, ffn1_up, ffn1_down))
    w_in_b = cast_layer(w_in, 0)
    pool_w_b = pool_w.astype(BF16)
    xl = x.reshape(nb * n, d)
    xc = ctx.reshape(nb * m, d)
    ctx_row = lambda i: nb

    def lat_row(tm):
        return lambda i: (i * tm) // n

    for l in range(depth):
        last = l == depth - 1
        mod3 = mod[l].reshape(MOD_ROWS, 1, N_MOD * d)

        xl = ffn(xl, mod3, 0, lat_row(FFN_TM), ffn1_norm[l], 0, *f1, tm=FFN_TM)
        xc = ffn(xc, mod3, 0, ctx_row, ffn1_norm[l], 0, *f1, tm=FFN_TM)

        px, gq, gk, gv, fx, nq, nk, nv = in_projection(
            xl, mod3, 3, lat_row(512), mix_norm[l], 0, w_in_b, q_norm[l], k_norm[l], rope)
        pc, gqc, gkc, gvc, fc, nqc, nkc, nvc = in_projection(
            xc, mod3, 3, ctx_row, mix_norm[l], 0, w_in_b, q_norm[l], k_norm[l])

        casts = [(w, l) for w in (w_out, ffn2_gate, ffn2_up, ffn2_down)]
        if not last:
            casts += [(w, l + 1) for w in (ffn1_gate, ffn1_up, ffn1_down, w_in)]
        y_gqa, cast_w = dense_attention(gq, nb, n, GQA_KV_HEADS, gkc, gvc, m, gk, gv, casts)
        w_out_b, f2 = cast_w[0], cast_w[1:4]
        if not last:
            f1, w_in_b = cast_w[4:7], cast_w[7]

        ys = (pool_mix(px, n, pool_w_b[l], pool_scale[l]),
              y_gqa,
              fourier_mix(fx, nb, n, fnet_w[l], seq_tables, chan_tables),
              neighbourhood_attention(nq, nk, nv, nkc, nvc, nb, n, m, na_rpb[l]))
        xl = out_projection(xl, mod3, 5, lat_row(512), ys, 0, w_out_b)

        if not last:
            ycs = (pool_mix(pc, m, pool_w_b[l], pool_scale[l]),
                   context_attention(gqc, nb, m, GQA_KV_HEADS, gkc, gvc, m),
                   fourier_mix(fc, nb, m, fnet_w[l], ctx_tables, chan_tables),
                   context_attention(nqc, nb, m, NA_HEADS, nkc, nvc, m))
            xc = out_projection(xc, mod3, 5, ctx_row, ycs, 0, w_out_b)
            xc = ffn(xc, mod3, 6, ctx_row, ffn2_norm[l], 0, *f2, tm=FFN_TM)

        xl = ffn(xl, mod3, 6, lat_row(FFN_TM), ffn2_norm[l], 0, *f2, final_g=final_norm if last else None, tm=FFN_TM)
    return xl.reshape(nb, n, d)
```

```python
import functools

import numpy as np
import jax
import jax.numpy as jnp
from jax import lax
from jax.experimental import pallas as pl
from jax.experimental.pallas import tpu as pltpu

F32 = jnp.float32
BF16 = jnp.bfloat16

GRID_W = 64
HEAD_DIM = 128
POOL_WINDOWS = (2, 4, 8, 16)
POOL_HALO = 8
GQA_Q_HEADS = 4
GQA_KV_HEADS = 2
FNET_GROUPS = 4
NA_HEADS = 4
NA_WIN_R = 8
NA_WIN_C = 16
ROPE_THETA = 10000.0
IN_SPLITS = (512, 512, 256, 256, 512, 512, 512, 512)
MIX_GROUP = 512
EPS = 1e-6
N_MOD = 9
MOD_ROWS = 8
MASK_VALUE = -1e30
LOG2E = float(np.log2(np.e))

V7X_VMEM_BYTES = 64 * 1024 * 1024
VMEM_CAP = V7X_VMEM_BYTES - 4 * 1024 * 1024
FFN_TM = 1024
BF16_SUBLANES = 16
LANES = 128


def _params(semantics, block_bytes, scratch_bytes=0):
    need = 2 * block_bytes + scratch_bytes
    limit = min(max(need + need // 2, 32 * 1024 * 1024), VMEM_CAP)
    return pltpu.CompilerParams(dimension_semantics=semantics, vmem_limit_bytes=limit)


def _nbytes(shape, dtype):
    return int(np.prod(shape)) * jnp.dtype(dtype).itemsize


def _norm_modulate(x, gain, shift, scale):
    inv = lax.rsqrt(jnp.mean(x * x, axis=-1, keepdims=True) + EPS)
    return (x * inv) * (gain * (1.0 + scale)) + shift


def _mod_spec(d, row_of_tile, k):
    return pl.BlockSpec((None, 1, d), lambda i, *_: (row_of_tile(i), 0, k))


def _mod_kernel(c_ref, w_ref, b_ref, o_ref):
    c = c_ref[...]
    s = c * jax.nn.sigmoid(c)
    o_ref[...] = jnp.dot(s.astype(BF16), w_ref[...].astype(BF16), preferred_element_type=F32) + b_ref[...]


def modulation(cvec, w_mod, b_mod, *, tn=512):
    nl, d, nd = w_mod.shape
    blocks = _nbytes((d, tn), F32) + _nbytes((MOD_ROWS, d), F32) + 2 * _nbytes((MOD_ROWS, tn), F32)
    return pl.pallas_call(
        _mod_kernel,
        grid=(nl, nd // tn),
        in_specs=[pl.BlockSpec((MOD_ROWS, d), lambda l, j: (0, 0)),
                  pl.BlockSpec((None, d, tn), lambda l, j: (l, 0, j)),
                  pl.BlockSpec((None, 1, tn), lambda l, j: (l, 0, j))],
        out_specs=pl.BlockSpec((None, MOD_ROWS, tn), lambda l, j: (l, 0, j)),
        out_shape=jax.ShapeDtypeStruct((nl, MOD_ROWS, nd), F32),
        compiler_params=_params(("parallel", "parallel"), blocks, _nbytes((d, tn), BF16)),
        name="modulation",
    )(cvec, w_mod, b_mod.reshape(nl, 1, nd))


def _cast_kernel(src_ref, dst_ref):
    dst_ref[...] = src_ref[...].astype(dst_ref.dtype)


def cast_layer(w, layer, *, rows=256):
    _, a, b = w.shape
    rows = min(rows, a)
    return pl.pallas_call(
        _cast_kernel,
        grid=(a // rows,),
        in_specs=[pl.BlockSpec((None, rows, b), lambda i: (layer, i, 0))],
        out_specs=pl.BlockSpec((None, rows, b), lambda i: (0, i, 0)),
        out_shape=jax.ShapeDtypeStruct((1, a, b), BF16),
        compiler_params=_params(("parallel",), _nbytes((rows, b), F32) + _nbytes((rows, b), BF16)),
        name="cast_layer",
    )(w)


def _ffn_kernel(*refs, nj, final):
    if final:
        x_ref, sh_ref, sc_ref, gt_ref, ng_ref, wg_ref, wu_ref, wd_ref, fg_ref, o_ref, h_ref = refs
    else:
        x_ref, sh_ref, sc_ref, gt_ref, ng_ref, wg_ref, wu_ref, wd_ref, o_ref, h_ref = refs
    j = pl.program_id(1)

    def partial_out():
        h = h_ref[...]
        g = jnp.dot(h, wg_ref[...], preferred_element_type=F32)
        u = jnp.dot(h, wu_ref[...], preferred_element_type=F32)
        a = (g * jax.nn.sigmoid(g) * u).astype(BF16)
        return jnp.dot(a, wd_ref[...], preferred_element_type=F32)

    @pl.when(j == 0)
    def _():
        h_ref[...] = _norm_modulate(x_ref[...], ng_ref[...], sh_ref[...], sc_ref[...]).astype(BF16)
        o_ref[...] = partial_out()

    @pl.when((j > 0) & (j < nj - 1))
    def _():
        o_ref[...] += partial_out()

    @pl.when(j == nj - 1)
    def _():
        r = x_ref[...] + 0.5 * gt_ref[...] * (o_ref[...] + partial_out())
        if final:
            r = r * lax.rsqrt(jnp.mean(r * r, axis=-1, keepdims=True) + EPS) * fg_ref[...]
        o_ref[...] = r


def ffn(x, mod3, k0, row_of_tile, norm_g, layer, wg, wu, wd, final_g=None, *, tm=512, tf=512):
    r, d = x.shape
    f = wg.shape[2]
    tm = min(tm, r)
    nj = f // tf
    assert nj >= 2
    final = final_g is not None
    row = lambda i, j: (i, 0)
    in_specs = [pl.BlockSpec((tm, d), row),
                _mod_spec(d, row_of_tile, k0), _mod_spec(d, row_of_tile, k0 + 1), _mod_spec(d, row_of_tile, k0 + 2),
                pl.BlockSpec((1, d), lambda i, j: (0, 0)),
                pl.BlockSpec((None, d, tf), lambda i, j: (layer, 0, j)),
                pl.BlockSpec((None, d, tf), lambda i, j: (layer, 0, j)),
                pl.BlockSpec((None, tf, d), lambda i, j: (layer, j, 0))]
    args = [x, mod3, mod3, mod3, norm_g.reshape(1, d), wg, wu, wd]
    if final:
        in_specs.append(pl.BlockSpec((1, d), lambda i, j: (0, 0)))
        args.append(final_g.reshape(1, d))
    blocks = 2 * _nbytes((tm, d), F32) + 3 * _nbytes((d, tf), BF16) + 5 * _nbytes((1, d), F32)
    temps = _nbytes((tm, d), BF16) + 4 * _nbytes((tm, tf), F32) + _nbytes((tm, d), F32)
    return pl.pallas_call(
        functools.partial(_ffn_kernel, nj=nj, final=final),
        grid=(r // tm, nj),
        in_specs=in_specs,
        out_specs=pl.BlockSpec((tm, d), row),
        out_shape=jax.ShapeDtypeStruct((r, d), F32),
        scratch_shapes=[pltpu.VMEM((tm, d), BF16)],
        compiler_params=_params(("parallel", "arbitrary"), blocks, temps),
        name="ffn",
    )(*args)


def _rms_heads(z, gain, nheads):
    outs = []
    for hd in range(nheads):
        zh = z[:, hd * HEAD_DIM:(hd + 1) * HEAD_DIM]
        outs.append(zh * lax.rsqrt(jnp.mean(zh * zh, axis=-1, keepdims=True) + EPS) * gain)
    return outs


def _rope(zh, cos, sin_signed):
    lane = lax.broadcasted_iota(jnp.int32, zh.shape, 1)
    partner = jnp.where(lane % 2 == 0, pltpu.roll(zh, HEAD_DIM - 1, 1), pltpu.roll(zh, 1, 1))
    return zh * cos + partner * sin_signed


def _inproj_kernel(*refs, rope):
    if rope:
        (x_ref, sh_ref, sc_ref, ng_ref, w_ref, qn_ref, kn_ref, cos_ref, sin_ref,
         px_ref, gq_ref, gk_ref, gv_ref, fx_ref, nq_ref, nk_ref, nv_ref, gvt_ref) = refs
    else:
        (x_ref, sh_ref, sc_ref, ng_ref, w_ref, qn_ref, kn_ref,
         px_ref, gq_ref, gk_ref, gv_ref, fx_ref, nq_ref, nk_ref, nv_ref, gvt_ref) = refs
    h = _norm_modulate(x_ref[...], ng_ref[...], sh_ref[...], sc_ref[...]).astype(BF16)
    offs = np.cumsum((0,) + IN_SPLITS)

    def proj(s):
        return jnp.dot(h, w_ref[:, offs[s]:offs[s + 1]], preferred_element_type=F32)

    sm_scale = HEAD_DIM ** -0.5 * LOG2E
    px_ref[...] = proj(0)
    q_heads = _rms_heads(proj(1), qn_ref[...], GQA_Q_HEADS)
    k_heads = _rms_heads(proj(2), kn_ref[...], GQA_KV_HEADS)
    if rope:
        cos, sin = cos_ref[...], sin_ref[...]
        q_heads = [_rope(zh, cos, sin) for zh in q_heads]
        k_heads = [_rope(zh, cos, sin) for zh in k_heads]
    gq_ref[...] = jnp.concatenate([zh * sm_scale for zh in q_heads], axis=-1).astype(BF16)
    gk_ref[...] = jnp.concatenate(k_heads, axis=-1).astype(BF16)
    gv = proj(3)
    gv_ref[...] = gv.astype(BF16)
    gvt_ref[...] = gv.T.astype(BF16)
    fx_ref[...] = proj(4).astype(BF16)
    nq_ref[...] = (proj(5) * sm_scale).astype(BF16)
    nk_ref[...] = proj(6).astype(BF16)
    nv_ref[...] = proj(7).astype(BF16)


def in_projection(x, seq_len, mod3, k0, row_of_tile, norm_g, layer, w_in, q_norm, k_norm, rope_tables=None, *, tm=512):
    r, d = x.shape
    tm = min(tm, seq_len)
    tiles_per_seq = seq_len // tm
    rope = rope_tables is not None
    row = lambda i: (i, 0)
    const = lambda i: (0, 0)
    in_specs = [pl.BlockSpec((tm, d), row),
                _mod_spec(d, row_of_tile, k0), _mod_spec(d, row_of_tile, k0 + 1),
                pl.BlockSpec((1, d), const),
                pl.BlockSpec((None,) + w_in.shape[1:], lambda i: (layer, 0, 0), pipeline_mode=pl.Buffered(1)),
                pl.BlockSpec((1, HEAD_DIM), const), pl.BlockSpec((1, HEAD_DIM), const)]
    args = [x, mod3, mod3, norm_g.reshape(1, d), w_in, q_norm.reshape(1, HEAD_DIM), k_norm.reshape(1, HEAD_DIM)]
    if rope:
        tab = pl.BlockSpec((tm, HEAD_DIM), lambda i: (i % tiles_per_seq, 0))
        in_specs += [tab, tab]
        args += list(rope_tables)
    widths = (512, 512, 256, 256, 512, 512, 512, 512)
    dtypes = (F32,) + (BF16,) * 7
    out_specs = [pl.BlockSpec((tm, w), row) for w in widths]
    out_shape = [jax.ShapeDtypeStruct((r, w), dt) for w, dt in zip(widths, dtypes)]
    kv_width = widths[3]
    out_specs.append(pl.BlockSpec((kv_width, tm), lambda i: (i // tiles_per_seq, i % tiles_per_seq)))
    out_shape.append(jax.ShapeDtypeStruct((r // seq_len * kv_width, seq_len), BF16))
    blocks = _nbytes((tm, d), F32) + _nbytes(w_in.shape[1:], BF16) // 2 + _nbytes((tm, 4096), F32)
    temps = _nbytes((tm, d), BF16) + 2 * _nbytes((tm, d), F32) + 6 * _nbytes((tm, 512), F32)
    return pl.pallas_call(
        functools.partial(_inproj_kernel, rope=rope),
        grid=(r // tm,),
        in_specs=in_specs,
        out_specs=out_specs,
        out_shape=out_shape,
        compiler_params=_params(("parallel",), blocks, temps),
        name="in_projection",
    )(*args)


def _pool_kernel(prev_ref, cur_ref, next_ref, w_ref, sc_ref, o_ref, *, n, ts):
    tiles_per_seq = n // ts
    s = pl.program_id(0) % tiles_per_seq
    cur = cur_ref[...]
    prev = jnp.where(s == 0, 0.0, prev_ref[...])
    nxt = jnp.where(s == tiles_per_seq - 1, 0.0, next_ref[...])
    ext = jnp.concatenate([prev, cur, nxt], axis=0)
    rows = ts + 2 * POOL_HALO
    t = s * ts + lax.broadcasted_iota(jnp.int32, (ts, 1), 0)
    gw = cur.shape[1] // len(POOL_WINDOWS)
    for gi, w in enumerate(POOL_WINDOWS):
        e = ext[:, gi * gw:(gi + 1) * gw]
        width = 1
        while width < w:
            e = e + pltpu.roll(e, rows - width, 0)
            width *= 2
        lead = POOL_HALO - w // 2
        win = (pltpu.roll(e, rows - lead, 0) if lead else e)[:ts]
        lo = jnp.maximum(t - w // 2, 0)
        hi = jnp.minimum(t + w // 2 - 1, n - 1)
        diff = win / (hi - lo + 1).astype(F32) - cur[:, gi * gw:(gi + 1) * gw]
        y = jnp.dot(diff.astype(BF16), w_ref[gi], preferred_element_type=F32)
        o_ref[:, gi * gw:(gi + 1) * gw] = (y * sc_ref[:, gi * gw:(gi + 1) * gw]).astype(o_ref.dtype)


def pool_mix(px, n, w_pool, scale, *, ts=512):
    r, c = px.shape
    ts = min(ts, n)
    hb = ts // POOL_HALO
    last_halo = r // POOL_HALO - 1
    blocks = 2 * _nbytes((ts, c), F32) + _nbytes(w_pool.shape, BF16)
    return pl.pallas_call(
        functools.partial(_pool_kernel, n=n, ts=ts),
        grid=(r // ts,),
        in_specs=[pl.BlockSpec((POOL_HALO, c), lambda i: (jnp.maximum(i * hb - 1, 0), 0)),
                  pl.BlockSpec((ts, c), lambda i: (i, 0)),
                  pl.BlockSpec((POOL_HALO, c), lambda i: (jnp.minimum((i + 1) * hb, last_halo), 0)),
                  pl.BlockSpec(w_pool.shape, lambda i: (0, 0, 0)),
                  pl.BlockSpec((1, c), lambda i: (0, 0))],
        out_specs=pl.BlockSpec((ts, c), lambda i: (i, 0)),
        out_shape=jax.ShapeDtypeStruct((r, c), BF16),
        compiler_params=_params(("parallel",), blocks, 6 * _nbytes((ts + 2 * POOL_HALO, c), F32)),
        name="pool_mix",
    )(px, px, px, w_pool, scale.reshape(1, c))


def _dft_chan_kernel(u_ref, c_ref, s_ref, w_ref, ap_ref, aq_ref, alt_ref, pq_ref, *, norm, tiles_per_seq):
    gw = c_ref.shape[0]
    i = pl.program_id(0)

    @pl.when(i == 0)
    def _():
        for g in range(FNET_GROUPS):
            w = w_ref[g]
            p = jnp.dot(c_ref[...], w, preferred_element_type=F32, precision=lax.Precision.HIGHEST)
            q = jnp.dot(s_ref[...], w, preferred_element_type=F32, precision=lax.Precision.HIGHEST)
            pq_ref[g] = (jnp.concatenate([p, -q], axis=-1) * norm).astype(BF16)

    sign = (1 - 2 * (lax.broadcasted_iota(jnp.int32, (u_ref.shape[0], 1), 0) % 2)).astype(F32)
    alt = []
    for g in range(FNET_GROUPS):
        a = jnp.dot(u_ref[:, g * gw:(g + 1) * gw], pq_ref[g], preferred_element_type=F32)
        ap_ref[:, g * gw:(g + 1) * gw] = a[:, :gw].astype(BF16)
        aq_ref[:, g * gw:(g + 1) * gw] = a[:, gw:].astype(BF16)
        alt.append(jnp.sum(a[:, :gw] * sign, axis=0, keepdims=True))

    @pl.when(i % tiles_per_seq == 0)
    def _():
        alt_ref[...] = jnp.zeros_like(alt_ref)

    alt_ref[...] += jnp.concatenate(alt, axis=-1)


def _dft_seq_kernel(c_ref, s_ref, ap_ref, aq_ref, alt_ref, lo_ref, hi_ref, g_ref, h_ref, carry_ref, *, nn):
    i, j = pl.program_id(0), pl.program_id(1)
    nb, tk, _ = g_ref.shape

    @pl.when(j == 0)
    def _():
        g_ref[...] = jnp.zeros_like(g_ref)
        h_ref[...] = jnp.zeros_like(h_ref)

    @pl.when((i == 0) & (j == 0))
    def _():
        carry_ref[...] = alt_ref[...]

    for b in range(nb):
        g_ref[b] += jnp.dot(c_ref[...], ap_ref[b], preferred_element_type=F32)
        h_ref[b] += jnp.dot(s_ref[...], aq_ref[b], preferred_element_type=F32)

    @pl.when(j == nn - 1)
    def _():
        lo_ref[...] = (g_ref[...] + h_ref[...]).astype(lo_ref.dtype)
        row = lax.broadcasted_iota(jnp.int32, (tk, tk), 0)
        col = lax.broadcasted_iota(jnp.int32, (tk, tk), 1)
        mirror = ((row + col) % tk == 0).astype(BF16)
        first = lax.broadcasted_iota(jnp.int32, (tk, 1), 0) == 0
        for b in range(nb):
            d = g_ref[b] - h_ref[b]
            m = jnp.dot(mirror, d.astype(BF16), preferred_element_type=F32)
            hi_ref[b] = jnp.where(first, carry_ref[b], m).astype(hi_ref.dtype)
            carry_ref[b] = d[0:1, :]


def _dft_tables(n, dtype, rows=None, tile=512):
    rows = n if rows is None else rows

    def cos_sin(cols):
        k = jnp.arange(rows, dtype=jnp.int32)[:, None]
        ang = ((k * cols[None, :]) % n).astype(F32) * (2.0 * np.pi / n)
        return jnp.cos(ang), jnp.sin(ang)

    if n <= tile:
        c, s = cos_sin(jnp.arange(n, dtype=jnp.int32))
        return c.astype(dtype), s.astype(dtype)
    c1, s1 = cos_sin(jnp.arange(tile, dtype=jnp.int32))
    c0, s0 = cos_sin(jnp.arange(n // tile, dtype=jnp.int32) * tile)
    c = c0[:, :, None] * c1[:, None, :] - s0[:, :, None] * s1[:, None, :]
    s = s0[:, :, None] * c1[:, None, :] + c0[:, :, None] * s1[:, None, :]
    return c.reshape(rows, n).astype(dtype), s.reshape(rows, n).astype(dtype)


def fourier_mix(fx, nb, n, w_fnet, seq_tables, chan_tables, *, tm=512, tk=512, tn=512):
    r, c = fx.shape
    gw = c // FNET_GROUPS
    half = n // 2
    tm, tk, tn = min(tm, n), min(tk, half), min(tn, n)
    cc, sc = chan_tables
    tiles_per_seq = n // tm
    ap, aq, alt = pl.pallas_call(
        functools.partial(_dft_chan_kernel, norm=float(1.0 / np.sqrt(n * gw)), tiles_per_seq=tiles_per_seq),
        grid=(r // tm,),
        in_specs=[pl.BlockSpec((tm, c), lambda i: (i, 0)),
                  pl.BlockSpec((gw, gw), lambda i: (0, 0)),
                  pl.BlockSpec((gw, gw), lambda i: (0, 0)),
                  pl.BlockSpec(w_fnet.shape, lambda i: (0, 0, 0))],
        out_specs=[pl.BlockSpec((tm, c), lambda i: (i, 0))] * 2
                  + [pl.BlockSpec((None, 1, c), lambda i: (i // tiles_per_seq, 0, 0))],
        out_shape=[jax.ShapeDtypeStruct((r, c), BF16)] * 2 + [jax.ShapeDtypeStruct((nb, 1, c), F32)],
        scratch_shapes=[pltpu.VMEM((FNET_GROUPS, gw, 2 * gw), BF16)],
        compiler_params=_params(("arbitrary",), 3 * _nbytes((tm, c), BF16) + 6 * _nbytes((gw, gw), F32),
                                4 * _nbytes((tm, c), F32)),
        name="dft_channels",
    )(fx, cc, sc, w_fnet)
    cn, sn = seq_tables
    nn = n // tn
    blocks = 2 * _nbytes((tk, tn), BF16) + 2 * _nbytes((nb, tn, c), BF16) + 2 * _nbytes((nb, tk, c), BF16)
    nt = half // tk
    y_lo, y_hi = pl.pallas_call(
        functools.partial(_dft_seq_kernel, nn=nn),
        grid=(nt, nn),
        in_specs=[pl.BlockSpec((tk, tn), lambda i, j: (nt - 1 - i, j)),
                  pl.BlockSpec((tk, tn), lambda i, j: (nt - 1 - i, j)),
                  pl.BlockSpec((nb, tn, c), lambda i, j: (0, j, 0)),
                  pl.BlockSpec((nb, tn, c), lambda i, j: (0, j, 0)),
                  pl.BlockSpec((nb, 1, c), lambda i, j: (0, 0, 0))],
        out_specs=[pl.BlockSpec((nb, tk, c), lambda i, j: (0, nt - 1 - i, 0)),
                   pl.BlockSpec((nb, tk, c), lambda i, j: (0, i, 0))],
        out_shape=[jax.ShapeDtypeStruct((nb, half, c), BF16)] * 2,
        scratch_shapes=[pltpu.VMEM((nb, tk, c), F32)] * 2 + [pltpu.VMEM((nb, 1, c), F32)],
        compiler_params=_params(("arbitrary", "arbitrary"), blocks, 4 * _nbytes((nb, tk, c), F32)),
        name="dft_positions",
    )(cn, sn, ap.reshape(nb, n, c), aq.reshape(nb, n, c), alt)
    return jnp.concatenate([y_lo, y_hi], axis=1).reshape(r, c)


def _attend(q, segments):
    dn = (((1,), (1,)), ((), ()))
    scores = []
    for k, _, bias in segments:
        s = lax.dot_general(q, k, dn, preferred_element_type=F32)
        scores.append(s if bias is None else s + bias)
    m = functools.reduce(jnp.maximum, [jnp.max(s, axis=-1, keepdims=True) for s in scores])
    probs = [jnp.exp2(s - m) for s in scores]
    l = sum(jnp.sum(p, axis=-1, keepdims=True) for p in probs)
    o = sum(jnp.dot(p.astype(BF16), v, preferred_element_type=F32) for p, (_, v, _) in zip(probs, segments))
    return o / l


def _ctx_attn_kernel(q_ref, kc_ref, vc_ref, o_ref, *, group):
    segments = [(kc_ref[...], vc_ref[...], None)]
    for g in range(group):
        hs = slice(g * HEAD_DIM, (g + 1) * HEAD_DIM)
        o_ref[:, hs] = _attend(q_ref[:, hs], segments).astype(o_ref.dtype)


def context_attention(q, nb, nq, kv_heads, k_ctx, v_ctx, m_ctx):
    group = q.shape[1] // HEAD_DIM // kv_heads
    gd = group * HEAD_DIM
    ctx = pl.BlockSpec((m_ctx, HEAD_DIM), lambda b, h: (b, h))
    blocks = 2 * _nbytes((nq, gd), BF16) + 2 * _nbytes((m_ctx, HEAD_DIM), BF16)
    return pl.pallas_call(
        functools.partial(_ctx_attn_kernel, group=group),
        grid=(nb, kv_heads),
        in_specs=[pl.BlockSpec((nq, gd), lambda b, h: (b, h)), ctx, ctx],
        out_specs=pl.BlockSpec((nq, gd), lambda b, h: (b, h)),
        out_shape=jax.ShapeDtypeStruct(q.shape, BF16),
        compiler_params=_params(("parallel", "parallel"), blocks, 6 * _nbytes((nq, m_ctx), F32)),
        name="context_attention",
    )(q, k_ctx, v_ctx)


def _dense_attn_kernel(*refs, group, n_casts):
    q_ref, kl_ref, vtl_ref, kc_ref, vtc_ref = refs[:5]
    cast_src = refs[5:5 + n_casts]
    o_ref = refs[5 + n_casts]
    cast_dst = refs[6 + n_casts:6 + 2 * n_casts]
    s_ref = refs[6 + 2 * n_casts]
    dn = (((1,), (1,)), ((), ()))
    k = jnp.concatenate([kl_ref[...], kc_ref[...]], axis=0)
    vt = jnp.concatenate([vtl_ref[...], vtc_ref[...]], axis=1)
    m = []
    for g in range(group):
        s = lax.dot_general(k, q_ref[:, g * HEAD_DIM:(g + 1) * HEAD_DIM], dn, preferred_element_type=F32)
        s_ref[g] = s
        m.append(jnp.max(s, axis=0, keepdims=True))
    for g in range(group):
        p = jnp.exp2(s_ref[g] - m[g])
        l = jnp.sum(p, axis=0, keepdims=True)
        o = jnp.dot(vt, p.astype(BF16), preferred_element_type=F32)
        o_ref[:, g * HEAD_DIM:(g + 1) * HEAD_DIM] = (o / l).T.astype(o_ref.dtype)
    for src, dst in zip(cast_src, cast_dst):
        dst[...] = src[...].astype(dst.dtype)


def dense_attention(q, nb, nq, kv_heads, k_ctx, v_ctx, m_ctx, k_lat, v_lat, casts=(), *, tq=256):
    heads = q.shape[1] // HEAD_DIM
    group = heads // kv_heads
    nqt = nq // tq
    gd = group * HEAD_DIM
    keys = nq + m_ctx
    qmap = lambda b, h, i: (b * nqt + i, h)
    lat = pl.BlockSpec((nq, HEAD_DIM), lambda b, h, i: (b, h))
    ctx = pl.BlockSpec((m_ctx, HEAD_DIM), lambda b, h, i: (b, h))
    lat_t = pl.BlockSpec((HEAD_DIM, nq), lambda b, h, i: (b * kv_heads + h, 0))
    ctx_t = pl.BlockSpec((HEAD_DIM, m_ctx), lambda b, h, i: (b * kv_heads + h, 0))
    blocks = 2 * _nbytes((tq, gd), BF16) + 2 * _nbytes((keys, HEAD_DIM), BF16)
    steps = nb * kv_heads * nqt
    cast_specs_in, cast_specs_out, cast_shapes = [], [], []
    for w, layer in casts:
        _, rows_w, cols_w = w.shape
        col_blocks = next(c for c in (1, 2, 4, 8) if steps % c == 0 and rows_w % (steps // c) == 0
                          and (rows_w // (steps // c)) % BF16_SUBLANES == 0 and cols_w % (c * LANES) == 0)
        slab = (rows_w // (steps // col_blocks), cols_w // col_blocks)

        def slab_index(b, h, i, col_blocks=col_blocks):
            step = (b * kv_heads + h) * nqt + i
            return step // col_blocks, step % col_blocks

        cast_specs_in.append(pl.BlockSpec((None,) + slab, lambda b, h, i, f=slab_index, layer=layer: (layer,) + f(b, h, i)))
        cast_specs_out.append(pl.BlockSpec((None,) + slab, lambda b, h, i, f=slab_index: (0,) + f(b, h, i)))
        cast_shapes.append(jax.ShapeDtypeStruct((1, rows_w, cols_w), BF16))
        blocks += _nbytes(slab, F32) + _nbytes(slab, BF16)
    outs = pl.pallas_call(
        functools.partial(_dense_attn_kernel, group=group, n_casts=len(casts)),
        grid=(nb, kv_heads, nqt),
        in_specs=[pl.BlockSpec((tq, gd), qmap), lat, lat_t, ctx, ctx_t] + cast_specs_in,
        out_specs=[pl.BlockSpec((tq, gd), qmap)] + cast_specs_out,
        out_shape=[jax.ShapeDtypeStruct(q.shape, BF16)] + cast_shapes,
        scratch_shapes=[pltpu.VMEM((group, keys, tq), F32)],
        compiler_params=_params(("parallel", "parallel", "parallel"), blocks, 2 * _nbytes((group, keys, tq), F32)),
        name="dense_attention",
    )(q, k_lat, v_lat, k_ctx, v_ctx, *[w for w, _ in casts])
    return outs[0], list(outs[1:])


def _na_plan(rows, rq):
    wr = min(NA_WIN_R, rows)
    band = min(rq + wr - 1, rows)
    nblk = rows // rq
    starts = [int(np.clip(rq * j - wr // 2, 0, rows - band)) for j in range(nblk)]
    deltas = [starts[j] - rq * j for j in range(nblk)]
    uniq = sorted(set(deltas))
    var = [uniq.index(dl) for dl in deltas]
    reps = [deltas.index(dl) for dl in uniq]
    return starts, var, band, reps


def _na_bias_kernel(rpb_ref, o_ref, *, rows, rq, band, starts, reps):
    h = pl.program_id(0)
    wr = min(NA_WIN_R, rows)
    qc = lax.broadcasted_iota(jnp.int32, (GRID_W, GRID_W), 0)
    kc = lax.broadcasted_iota(jnp.int32, (GRID_W, GRID_W), 1)
    dc = kc - qc + (NA_WIN_C - 1)
    cs = jnp.clip(qc - NA_WIN_C // 2, 0, GRID_W - NA_WIN_C)
    col_ok = (kc >= cs) & (kc < cs + NA_WIN_C)
    masked = jnp.full((GRID_W, GRID_W), MASK_VALUE, F32)
    by_dr = []
    for dr in range(2 * NA_WIN_R - 1):
        t = masked
        for d in range(2 * NA_WIN_C - 1):
            t = jnp.where(dc == d, rpb_ref[h, dr, d] * LOG2E, t)
        by_dr.append(jnp.where(col_ok, t, MASK_VALUE))
    for v, j in enumerate(reps):
        for e in range(rq):
            qr = rq * j + e
            rs = int(np.clip(qr - wr // 2, 0, rows - wr))
            for a in range(band):
                kr = starts[j] + a
                blk = by_dr[kr - qr + NA_WIN_R - 1] if rs <= kr < rs + wr else masked
                o_ref[v, e * GRID_W:(e + 1) * GRID_W, a * GRID_W:(a + 1) * GRID_W] = blk


def _na_bias(rpb, rows, rq, band, starts, reps):
    nv = len(reps)
    tq, bk = rq * GRID_W, band * GRID_W
    return pl.pallas_call(
        functools.partial(_na_bias_kernel, rows=rows, rq=rq, band=band, starts=starts, reps=reps),
        grid=(NA_HEADS,),
        in_specs=[pl.BlockSpec(memory_space=pltpu.SMEM)],
        out_specs=pl.BlockSpec((nv, None, tq, bk), lambda h: (0, h, 0, 0)),
        out_shape=jax.ShapeDtypeStruct((nv, NA_HEADS, tq, bk), F32),
        compiler_params=_params(("parallel",), _nbytes((nv, tq, bk), F32)),
        name="na_bias",
    )(rpb)


def _na_kernel(start_ref, var_ref, q_ref, k_ref, v_ref, kc_ref, vc_ref, bias_ref, o_ref, s_ref, *, band_keys):
    del var_ref
    j = pl.program_id(1)
    st = pl.multiple_of(start_ref[j] * GRID_W, GRID_W)
    n_ctx = kc_ref.shape[0]
    dn = (((1,), (1,)), ((), ()))
    m = []
    for h in range(NA_HEADS):
        hs = slice(h * HEAD_DIM, (h + 1) * HEAD_DIM)
        q = q_ref[:, hs]
        s_ctx = lax.dot_general(q, kc_ref[:, hs], dn, preferred_element_type=F32)
        s_loc = lax.dot_general(q, k_ref[pl.ds(st, band_keys), hs], dn, preferred_element_type=F32) + bias_ref[h]
        s_ref[h, :, :n_ctx] = s_ctx
        s_ref[h, :, n_ctx:] = s_loc
        m.append(jnp.maximum(jnp.max(s_ctx, axis=-1, keepdims=True), jnp.max(s_loc, axis=-1, keepdims=True)))
    for h in range(NA_HEADS):
        hs = slice(h * HEAD_DIM, (h + 1) * HEAD_DIM)
        p = jnp.exp2(s_ref[h] - m[h])
        l = jnp.sum(p, axis=-1, keepdims=True)
        p = p.astype(BF16)
        o = (jnp.dot(p[:, :n_ctx], vc_ref[:, hs], preferred_element_type=F32)
             + jnp.dot(p[:, n_ctx:], v_ref[pl.ds(st, band_keys), hs], preferred_element_type=F32))
        o_ref[:, hs] = (o / l).astype(o_ref.dtype)


def neighbourhood_attention(q, k, v, k_ctx, v_ctx, nb, n, m_ctx, rpb, *, rq=4):
    rows = n // GRID_W
    rq = min(rq, rows)
    starts, var, band, reps = _na_plan(rows, rq)
    bias = _na_bias(rpb, rows, rq, band, starts, reps)
    tq, band_keys = rq * GRID_W, band * GRID_W
    nblk = rows // rq
    c = q.shape[1]
    blocks = (_nbytes((tq, c), BF16) * 2 + 2 * _nbytes((n, c), BF16) + 2 * _nbytes((m_ctx, c), BF16)
              + _nbytes((NA_HEADS, tq, band_keys), F32))
    grid_spec = pltpu.PrefetchScalarGridSpec(
        num_scalar_prefetch=2,
        grid=(nb, nblk),
        in_specs=[pl.BlockSpec((tq, c), lambda b, j, st, vr: (b * nblk + j, 0)),
                  pl.BlockSpec((n, c), lambda b, j, st, vr: (b, 0)),
                  pl.BlockSpec((n, c), lambda b, j, st, vr: (b, 0)),
                  pl.BlockSpec((m_ctx, c), lambda b, j, st, vr: (b, 0)),
                  pl.BlockSpec((m_ctx, c), lambda b, j, st, vr: (b, 0)),
                  pl.BlockSpec((None, NA_HEADS, tq, band_keys), lambda b, j, st, vr: (vr[j], 0, 0, 0))],
        out_specs=pl.BlockSpec((tq, c), lambda b, j, st, vr: (b * nblk + j, 0)),
        scratch_shapes=[pltpu.VMEM((NA_HEADS, tq, m_ctx + band_keys), F32)],
    )
    return pl.pallas_call(
        functools.partial(_na_kernel, band_keys=band_keys),
        grid_spec=grid_spec,
        out_shape=jax.ShapeDtypeStruct(q.shape, BF16),
        compiler_params=_params(("parallel", "arbitrary"), blocks, 6 * _nbytes((tq, band_keys + m_ctx), F32)),
        name="neighbourhood_attention",
    )(jnp.asarray(starts, jnp.int32), jnp.asarray(var, jnp.int32), q, k, v, k_ctx, v_ctx, bias)


def _outproj_kernel(x_ref, gt_ref, y0_ref, y1_ref, y2_ref, y3_ref, w_ref, o_ref):
    y = jnp.concatenate([y0_ref[...], y1_ref[...], y2_ref[...], y3_ref[...]], axis=-1)
    o_ref[...] = x_ref[...] + gt_ref[...] * jnp.dot(y, w_ref[...], preferred_element_type=F32)


def out_projection(x, mod3, k_gate, row_of_tile, ys, layer, w_out, *, tm=512):
    r, d = x.shape
    tm = min(tm, r)
    row = lambda i: (i, 0)
    blocks = 2 * _nbytes((tm, d), F32) + _nbytes((tm, d), BF16) + _nbytes(w_out.shape[1:], BF16) // 2
    return pl.pallas_call(
        _outproj_kernel,
        grid=(r // tm,),
        in_specs=[pl.BlockSpec((tm, d), row), _mod_spec(d, row_of_tile, k_gate)]
                 + [pl.BlockSpec((tm, MIX_GROUP), row)] * 4
                 + [pl.BlockSpec((None,) + w_out.shape[1:], lambda i: (layer, 0, 0), pipeline_mode=pl.Buffered(1))],
        out_specs=pl.BlockSpec((tm, d), row),
        out_shape=jax.ShapeDtypeStruct((r, d), F32),
        compiler_params=_params(("parallel",), blocks, 3 * _nbytes((tm, d), F32)),
        name="out_projection",
    )(x, mod3, *ys, w_out)


def _rope_tables(n):
    t = jnp.arange(n, dtype=jnp.int32)
    row = (t // GRID_W).astype(F32)
    col = (t % GRID_W).astype(F32)
    n_freq = HEAD_DIM // 4
    inv = 1.0 / (ROPE_THETA ** (jnp.arange(n_freq, dtype=F32) / n_freq))
    ang = jnp.concatenate([row[:, None] * inv, col[:, None] * inv], axis=-1)
    cos = jnp.repeat(jnp.cos(ang), 2, axis=-1)
    sin = jnp.repeat(jnp.sin(ang), 2, axis=-1)
    sign = jnp.tile(jnp.asarray([-1.0, 1.0], F32), HEAD_DIM // 2)
    return cos, sin * sign


def kernel(x, c, ctx, c_ctx, w_mod, b_mod, ffn1_norm, ffn1_gate, ffn1_up, ffn1_down, mix_norm, w_in, w_out,
           pool_w, pool_scale, q_norm, k_norm, fnet_w, na_rpb, ffn2_norm, ffn2_gate, ffn2_up, ffn2_down, final_norm):
    nb, n, d = x.shape
    m = ctx.shape[1]
    depth = w_mod.shape[0]
    assert nb + 1 <= MOD_ROWS

    cvec = jnp.zeros((MOD_ROWS, d), F32).at[:nb].set(c).at[nb].set(c_ctx)
    mod = modulation(cvec, w_mod, b_mod)

    rope = _rope_tables(n)
    seq_tables = _dft_tables(n, BF16, rows=n // 2)
    ctx_tables = _dft_tables(m, BF16, rows=m // 2)
    chan_tables = _dft_tables(MIX_GROUP // FNET_GROUPS, F32)

    f1 = tuple(cast_layer(w, 0) for w in (ffn1_gate, ffn1_up, ffn1_down))
    w_in_b = cast_layer(w_in, 0)
    pool_w_b = pool_w.astype(BF16)
    xl = x.reshape(nb * n, d)
    xc = ctx.reshape(nb * m, d)
    ctx_row = lambda i: nb

    def lat_row(tm):
        return lambda i: (i * tm) // n

    for l in range(depth):
        last = l == depth - 1
        mod3 = mod[l].reshape(MOD_ROWS, 1, N_MOD * d)

        xl = ffn(xl, mod3, 0, lat_row(FFN_TM), ffn1_norm[l], 0, *f1, tm=FFN_TM)
        xc = ffn(xc, mod3, 0, ctx_row, ffn1_norm[l], 0, *f1, tm=FFN_TM)

        px, gq, gk, _, fx, nq, nk, nv, gvt = in_projection(
            xl, n, mod3, 3, lat_row(512), mix_norm[l], 0, w_in_b, q_norm[l], k_norm[l], rope)
        pc, gqc, gkc, gvc, fc, nqc, nkc, nvc, gvct = in_projection(
            xc, m, mod3, 3, ctx_row, mix_norm[l], 0, w_in_b, q_norm[l], k_norm[l])

        casts = [(w, l) for w in (w_out, ffn2_gate, ffn2_up, ffn2_down)]
        if not last:
            casts += [(w, l + 1) for w in (ffn1_gate, ffn1_up, ffn1_down, w_in)]
        y_gqa, cast_w = dense_attention(gq, nb, n, GQA_KV_HEADS, gkc, gvct, m, gk, gvt, casts)
        w_out_b, f2 = cast_w[0], cast_w[1:4]
        if not last:
            f1, w_in_b = cast_w[4:7], cast_w[7]

        ys = (pool_mix(px, n, pool_w_b[l], pool_scale[l]),
              y_gqa,
              fourier_mix(fx, nb, n, fnet_w[l], seq_tables, chan_tables),
              neighbourhood_attention(nq, nk, nv, nkc, nvc, nb, n, m, na_rpb[l]))
        xl = out_projection(xl, mod3, 5, lat_row(512), ys, 0, w_out_b)

        if not last:
            ycs = (pool_mix(pc, m, pool_w_b[l], pool_scale[l]),
                   context_attention(gqc, nb, m, GQA_KV_HEADS, gkc, gvc, m),
                   fourier_mix(fc, nb, m, fnet_w[l], ctx_tables, chan_tables),
                   context_attention(nqc, nb, m, NA_HEADS, nkc, nvc, m))
            xc = out_projection(xc, mod3, 5, ctx_row, ycs, 0, w_out_b)
            xc = ffn(xc, mod3, 6, ctx_row, ffn2_norm[l], 0, *f2, tm=FFN_TM)

        xl = ffn(xl, mod3, 6, lat_row(FFN_TM), ffn2_norm[l], 0, *f2, final_g=final_norm if last else None, tm=FFN_TM)
    return xl.reshape(nb, n, d)
```

```python
import functools

import numpy as np
import jax
import jax.numpy as jnp
from jax import lax
from jax.experimental import pallas as pl
from jax.experimental.pallas import tpu as pltpu

F32 = jnp.float32
BF16 = jnp.bfloat16

GRID_W = 64
HEAD_DIM = 128
POOL_WINDOWS = (2, 4, 8, 16)
POOL_HALO = 8
GQA_Q_HEADS = 4
GQA_KV_HEADS = 2
FNET_GROUPS = 4
NA_HEADS = 4
NA_WIN_R = 8
NA_WIN_C = 16
ROPE_THETA = 10000.0
IN_SPLITS = (512, 512, 256, 256, 512, 512, 512, 512)
MIX_GROUP = 512
EPS = 1e-6
N_MOD = 9
MOD_ROWS = 8
MASK_VALUE = -1e30
LOG2E = float(np.log2(np.e))

V7X_VMEM_BYTES = 64 * 1024 * 1024
VMEM_CAP = V7X_VMEM_BYTES - 4 * 1024 * 1024
FFN_TM = 1024
BF16_SUBLANES = 16
LANES = 128


def _params(semantics, block_bytes, scratch_bytes=0):
    need = 2 * block_bytes + scratch_bytes
    limit = min(max(need + need // 2, 32 * 1024 * 1024), VMEM_CAP)
    return pltpu.CompilerParams(dimension_semantics=semantics, vmem_limit_bytes=limit)


def _nbytes(shape, dtype):
    return int(np.prod(shape)) * jnp.dtype(dtype).itemsize


def _norm_modulate(x, gain, shift, scale):
    inv = lax.rsqrt(jnp.mean(x * x, axis=-1, keepdims=True) + EPS)
    return (x * inv) * (gain * (1.0 + scale)) + shift


def _mod_spec(d, row_of_tile, k):
    return pl.BlockSpec((None, 1, d), lambda i, *_: (row_of_tile(i), 0, k))


def _mod_kernel(c_ref, w_ref, b_ref, o_ref):
    c = c_ref[...]
    s = c * jax.nn.sigmoid(c)
    o_ref[...] = jnp.dot(s.astype(BF16), w_ref[...].astype(BF16), preferred_element_type=F32) + b_ref[...]


def modulation(cvec, w_mod, b_mod, *, tn=512):
    nl, d, nd = w_mod.shape
    blocks = _nbytes((d, tn), F32) + _nbytes((MOD_ROWS, d), F32) + 2 * _nbytes((MOD_ROWS, tn), F32)
    return pl.pallas_call(
        _mod_kernel,
        grid=(nl, nd // tn),
        in_specs=[pl.BlockSpec((MOD_ROWS, d), lambda l, j: (0, 0)),
                  pl.BlockSpec((None, d, tn), lambda l, j: (l, 0, j)),
                  pl.BlockSpec((None, 1, tn), lambda l, j: (l, 0, j))],
        out_specs=pl.BlockSpec((None, MOD_ROWS, tn), lambda l, j: (l, 0, j)),
        out_shape=jax.ShapeDtypeStruct((nl, MOD_ROWS, nd), F32),
        compiler_params=_params(("parallel", "parallel"), blocks, _nbytes((d, tn), BF16)),
        name="modulation",
    )(cvec, w_mod, b_mod.reshape(nl, 1, nd))


def _cast_kernel(src_ref, dst_ref):
    dst_ref[...] = src_ref[...].astype(dst_ref.dtype)


def cast_layer(w, layer, *, rows=256):
    _, a, b = w.shape
    rows = min(rows, a)
    return pl.pallas_call(
        _cast_kernel,
        grid=(a // rows,),
        in_specs=[pl.BlockSpec((None, rows, b), lambda i: (layer, i, 0))],
        out_specs=pl.BlockSpec((None, rows, b), lambda i: (0, i, 0)),
        out_shape=jax.ShapeDtypeStruct((1, a, b), BF16),
        compiler_params=_params(("parallel",), _nbytes((rows, b), F32) + _nbytes((rows, b), BF16)),
        name="cast_layer",
    )(w)


def _ffn_kernel(*refs, nj, final):
    if final:
        x_ref, sh_ref, sc_ref, gt_ref, ng_ref, wg_ref, wu_ref, wd_ref, fg_ref, o_ref, h_ref = refs
    else:
        x_ref, sh_ref, sc_ref, gt_ref, ng_ref, wg_ref, wu_ref, wd_ref, o_ref, h_ref = refs
    j = pl.program_id(1)

    def partial_out():
        h = h_ref[...]
        g = jnp.dot(h, wg_ref[...], preferred_element_type=F32)
        u = jnp.dot(h, wu_ref[...], preferred_element_type=F32)
        a = (g * jax.nn.sigmoid(g) * u).astype(BF16)
        return jnp.dot(a, wd_ref[...], preferred_element_type=F32)

    @pl.when(j == 0)
    def _():
        h_ref[...] = _norm_modulate(x_ref[...], ng_ref[...], sh_ref[...], sc_ref[...]).astype(BF16)
        o_ref[...] = partial_out()

    @pl.when((j > 0) & (j < nj - 1))
    def _():
        o_ref[...] += partial_out()

    @pl.when(j == nj - 1)
    def _():
        r = x_ref[...] + 0.5 * gt_ref[...] * (o_ref[...] + partial_out())
        if final:
            r = r * lax.rsqrt(jnp.mean(r * r, axis=-1, keepdims=True) + EPS) * fg_ref[...]
        o_ref[...] = r


def ffn(x, mod3, k0, row_of_tile, norm_g, layer, wg, wu, wd, final_g=None, *, tm=512, tf=512):
    r, d = x.shape
    f = wg.shape[2]
    tm = min(tm, r)
    nj = f // tf
    assert nj >= 2
    final = final_g is not None
    row = lambda i, j: (i, 0)
    in_specs = [pl.BlockSpec((tm, d), row),
                _mod_spec(d, row_of_tile, k0), _mod_spec(d, row_of_tile, k0 + 1), _mod_spec(d, row_of_tile, k0 + 2),
                pl.BlockSpec((1, d), lambda i, j: (0, 0)),
                pl.BlockSpec((None, d, tf), lambda i, j: (layer, 0, j)),
                pl.BlockSpec((None, d, tf), lambda i, j: (layer, 0, j)),
                pl.BlockSpec((None, tf, d), lambda i, j: (layer, j, 0))]
    args = [x, mod3, mod3, mod3, norm_g.reshape(1, d), wg, wu, wd]
    if final:
        in_specs.append(pl.BlockSpec((1, d), lambda i, j: (0, 0)))
        args.append(final_g.reshape(1, d))
    blocks = 2 * _nbytes((tm, d), F32) + 3 * _nbytes((d, tf), BF16) + 5 * _nbytes((1, d), F32)
    temps = _nbytes((tm, d), BF16) + 4 * _nbytes((tm, tf), F32) + _nbytes((tm, d), F32)
    return pl.pallas_call(
        functools.partial(_ffn_kernel, nj=nj, final=final),
        grid=(r // tm, nj),
        in_specs=in_specs,
        out_specs=pl.BlockSpec((tm, d), row),
        out_shape=jax.ShapeDtypeStruct((r, d), F32),
        scratch_shapes=[pltpu.VMEM((tm, d), BF16)],
        compiler_params=_params(("parallel", "arbitrary"), blocks, temps),
        name="ffn",
    )(*args)


def _rms_heads(z, gain, nheads):
    outs = []
    for hd in range(nheads):
        zh = z[:, hd * HEAD_DIM:(hd + 1) * HEAD_DIM]
        outs.append(zh * lax.rsqrt(jnp.mean(zh * zh, axis=-1, keepdims=True) + EPS) * gain)
    return outs


def _rope(zh, cos, sin_signed):
    lane = lax.broadcasted_iota(jnp.int32, zh.shape, 1)
    partner = jnp.where(lane % 2 == 0, pltpu.roll(zh, HEAD_DIM - 1, 1), pltpu.roll(zh, 1, 1))
    return zh * cos + partner * sin_signed


def _inproj_kernel(*refs, rope):
    if rope:
        (x_ref, sh_ref, sc_ref, ng_ref, w_ref, qn_ref, kn_ref, cos_ref, sin_ref,
         px_ref, gq_ref, gk_ref, gv_ref, fx_ref, nq_ref, nk_ref, nv_ref, gvt_ref) = refs
    else:
        (x_ref, sh_ref, sc_ref, ng_ref, w_ref, qn_ref, kn_ref,
         px_ref, gq_ref, gk_ref, gv_ref, fx_ref, nq_ref, nk_ref, nv_ref, gvt_ref) = refs
    h = _norm_modulate(x_ref[...], ng_ref[...], sh_ref[...], sc_ref[...]).astype(BF16)
    offs = np.cumsum((0,) + IN_SPLITS)

    def proj(s):
        return jnp.dot(h, w_ref[:, offs[s]:offs[s + 1]], preferred_element_type=F32)

    sm_scale = HEAD_DIM ** -0.5 * LOG2E
    px_ref[...] = proj(0)
    q_heads = _rms_heads(proj(1), qn_ref[...], GQA_Q_HEADS)
    k_heads = _rms_heads(proj(2), kn_ref[...], GQA_KV_HEADS)
    if rope:
        cos, sin = cos_ref[...], sin_ref[...]
        q_heads = [_rope(zh, cos, sin) for zh in q_heads]
        k_heads = [_rope(zh, cos, sin) for zh in k_heads]
    gq_ref[...] = jnp.concatenate([zh * sm_scale for zh in q_heads], axis=-1).astype(BF16)
    gk_ref[...] = jnp.concatenate(k_heads, axis=-1).astype(BF16)
    gv = proj(3)
    gv_ref[...] = gv.astype(BF16)
    gvt_ref[...] = gv.T.astype(BF16)
    fx_ref[...] = proj(4).astype(BF16)
    nq_ref[...] = (proj(5) * sm_scale).astype(BF16)
    nk_ref[...] = proj(6).astype(BF16)
    nv_ref[...] = proj(7).astype(BF16)


def in_projection(x, seq_len, mod3, k0, row_of_tile, norm_g, layer, w_in, q_norm, k_norm, rope_tables=None, *, tm=512):
    r, d = x.shape
    tm = min(tm, seq_len)
    tiles_per_seq = seq_len // tm
    rope = rope_tables is not None
    row = lambda i: (i, 0)
    const = lambda i: (0, 0)
    in_specs = [pl.BlockSpec((tm, d), row),
                _mod_spec(d, row_of_tile, k0), _mod_spec(d, row_of_tile, k0 + 1),
                pl.BlockSpec((1, d), const),
                pl.BlockSpec((None,) + w_in.shape[1:], lambda i: (layer, 0, 0), pipeline_mode=pl.Buffered(1)),
                pl.BlockSpec((1, HEAD_DIM), const), pl.BlockSpec((1, HEAD_DIM), const)]
    args = [x, mod3, mod3, norm_g.reshape(1, d), w_in, q_norm.reshape(1, HEAD_DIM), k_norm.reshape(1, HEAD_DIM)]
    if rope:
        tab = pl.BlockSpec((tm, HEAD_DIM), lambda i: (i % tiles_per_seq, 0))
        in_specs += [tab, tab]
        args += list(rope_tables)
    widths = (512, 512, 256, 256, 512, 512, 512, 512)
    dtypes = (F32,) + (BF16,) * 7
    out_specs = [pl.BlockSpec((tm, w), row) for w in widths]
    out_shape = [jax.ShapeDtypeStruct((r, w), dt) for w, dt in zip(widths, dtypes)]
    kv_width = widths[3]
    out_specs.append(pl.BlockSpec((kv_width, tm), lambda i: (i // tiles_per_seq, i % tiles_per_seq)))
    out_shape.append(jax.ShapeDtypeStruct((r // seq_len * kv_width, seq_len), BF16))
    blocks = _nbytes((tm, d), F32) + _nbytes(w_in.shape[1:], BF16) // 2 + _nbytes((tm, 4096), F32)
    temps = _nbytes((tm, d), BF16) + 2 * _nbytes((tm, d), F32) + 6 * _nbytes((tm, 512), F32)
    return pl.pallas_call(
        functools.partial(_inproj_kernel, rope=rope),
        grid=(r // tm,),
        in_specs=in_specs,
        out_specs=out_specs,
        out_shape=out_shape,
        compiler_params=_params(("parallel",), blocks, temps),
        name="in_projection",
    )(*args)


def _pool_kernel(prev_ref, cur_ref, next_ref, w_ref, sc_ref, o_ref, *, n, ts):
    tiles_per_seq = n // ts
    s = pl.program_id(0) % tiles_per_seq
    cur = cur_ref[...]
    prev = jnp.where(s == 0, 0.0, prev_ref[...])
    nxt = jnp.where(s == tiles_per_seq - 1, 0.0, next_ref[...])
    ext = jnp.concatenate([prev, cur, nxt], axis=0)
    rows = ts + 2 * POOL_HALO
    t = s * ts + lax.broadcasted_iota(jnp.int32, (ts, 1), 0)
    gw = cur.shape[1] // len(POOL_WINDOWS)
    for gi, w in enumerate(POOL_WINDOWS):
        e = ext[:, gi * gw:(gi + 1) * gw]
        width = 1
        while width < w:
            e = e + pltpu.roll(e, rows - width, 0)
            width *= 2
        lead = POOL_HALO - w // 2
        win = (pltpu.roll(e, rows - lead, 0) if lead else e)[:ts]
        lo = jnp.maximum(t - w // 2, 0)
        hi = jnp.minimum(t + w // 2 - 1, n - 1)
        diff = win / (hi - lo + 1).astype(F32) - cur[:, gi * gw:(gi + 1) * gw]
        y = jnp.dot(diff.astype(BF16), w_ref[gi], preferred_element_type=F32)
        o_ref[:, gi * gw:(gi + 1) * gw] = (y * sc_ref[:, gi * gw:(gi + 1) * gw]).astype(o_ref.dtype)


def pool_mix(px, n, w_pool, scale, *, ts=512):
    r, c = px.shape
    ts = min(ts, n)
    hb = ts // POOL_HALO
    last_halo = r // POOL_HALO - 1
    blocks = 2 * _nbytes((ts, c), F32) + _nbytes(w_pool.shape, BF16)
    return pl.pallas_call(
        functools.partial(_pool_kernel, n=n, ts=ts),
        grid=(r // ts,),
        in_specs=[pl.BlockSpec((POOL_HALO, c), lambda i: (jnp.maximum(i * hb - 1, 0), 0)),
                  pl.BlockSpec((ts, c), lambda i: (i, 0)),
                  pl.BlockSpec((POOL_HALO, c), lambda i: (jnp.minimum((i + 1) * hb, last_halo), 0)),
                  pl.BlockSpec(w_pool.shape, lambda i: (0, 0, 0)),
                  pl.BlockSpec((1, c), lambda i: (0, 0))],
        out_specs=pl.BlockSpec((ts, c), lambda i: (i, 0)),
        out_shape=jax.ShapeDtypeStruct((r, c), BF16),
        compiler_params=_params(("parallel",), blocks, 6 * _nbytes((ts + 2 * POOL_HALO, c), F32)),
        name="pool_mix",
    )(px, px, px, w_pool, scale.reshape(1, c))


def _dft_chan_kernel(u_ref, c_ref, s_ref, w_ref, ap_ref, aq_ref, alt_ref, pq_ref, *, norm, tiles_per_seq):
    gw = c_ref.shape[0]
    i = pl.program_id(0)

    @pl.when(i == 0)
    def _():
        for g in range(FNET_GROUPS):
            w = w_ref[g]
            p = jnp.dot(c_ref[...], w, preferred_element_type=F32, precision=lax.Precision.HIGHEST)
            q = jnp.dot(s_ref[...], w, preferred_element_type=F32, precision=lax.Precision.HIGHEST)
            pq_ref[g] = (jnp.concatenate([p, -q], axis=-1) * norm).astype(BF16)

    sign = (1 - 2 * (lax.broadcasted_iota(jnp.int32, (u_ref.shape[0], 1), 0) % 2)).astype(F32)
    alt = []
    for g in range(FNET_GROUPS):
        a = jnp.dot(u_ref[:, g * gw:(g + 1) * gw], pq_ref[g], preferred_element_type=F32)
        ap_ref[:, g * gw:(g + 1) * gw] = a[:, :gw].astype(BF16)
        aq_ref[:, g * gw:(g + 1) * gw] = a[:, gw:].astype(BF16)
        alt.append(jnp.sum(a[:, :gw] * sign, axis=0, keepdims=True))

    @pl.when(i % tiles_per_seq == 0)
    def _():
        alt_ref[...] = jnp.zeros_like(alt_ref)

    alt_ref[...] += jnp.concatenate(alt, axis=-1)


def _dft_seq_kernel(c_ref, s_ref, ap_ref, aq_ref, alt_ref, lo_ref, hi_ref, g_ref, h_ref, carry_ref, *, nn):
    i, j = pl.program_id(0), pl.program_id(1)
    nb, tk, _ = g_ref.shape

    @pl.when(j == 0)
    def _():
        g_ref[...] = jnp.zeros_like(g_ref)
        h_ref[...] = jnp.zeros_like(h_ref)

    @pl.when((i == 0) & (j == 0))
    def _():
        carry_ref[...] = alt_ref[...]

    for b in range(nb):
        g_ref[b] += jnp.dot(c_ref[...], ap_ref[b], preferred_element_type=F32)
        h_ref[b] += jnp.dot(s_ref[...], aq_ref[b], preferred_element_type=F32)

    @pl.when(j == nn - 1)
    def _():
        lo_ref[...] = (g_ref[...] + h_ref[...]).astype(lo_ref.dtype)
        row = lax.broadcasted_iota(jnp.int32, (tk, tk), 0)
        col = lax.broadcasted_iota(jnp.int32, (tk, tk), 1)
        mirror = ((row + col) % tk == 0).astype(BF16)
        first = lax.broadcasted_iota(jnp.int32, (tk, 1), 0) == 0
        for b in range(nb):
            d = g_ref[b] - h_ref[b]
            m = jnp.dot(mirror, d.astype(BF16), preferred_element_type=F32)
            hi_ref[b] = jnp.where(first, carry_ref[b], m).astype(hi_ref.dtype)
            carry_ref[b] = d[0:1, :]


def _dft_tables(n, dtype, rows=None, tile=512):
    rows = n if rows is None else rows

    def cos_sin(cols):
        k = jnp.arange(rows, dtype=jnp.int32)[:, None]
        ang = ((k * cols[None, :]) % n).astype(F32) * (2.0 * np.pi / n)
        return jnp.cos(ang), jnp.sin(ang)

    if n <= tile:
        c, s = cos_sin(jnp.arange(n, dtype=jnp.int32))
        return c.astype(dtype), s.astype(dtype)
    c1, s1 = cos_sin(jnp.arange(tile, dtype=jnp.int32))
    c0, s0 = cos_sin(jnp.arange(n // tile, dtype=jnp.int32) * tile)
    c = c0[:, :, None] * c1[:, None, :] - s0[:, :, None] * s1[:, None, :]
    s = s0[:, :, None] * c1[:, None, :] + c0[:, :, None] * s1[:, None, :]
    return c.reshape(rows, n).astype(dtype), s.reshape(rows, n).astype(dtype)


def fourier_mix(fx, nb, n, w_fnet, seq_tables, chan_tables, *, tm=512, tk=1024, tn=512):
    r, c = fx.shape
    gw = c // FNET_GROUPS
    half = n // 2
    tm, tk, tn = min(tm, n), min(tk, half), min(tn, n)
    cc, sc = chan_tables
    tiles_per_seq = n // tm
    ap, aq, alt = pl.pallas_call(
        functools.partial(_dft_chan_kernel, norm=float(1.0 / np.sqrt(n * gw)), tiles_per_seq=tiles_per_seq),
        grid=(r // tm,),
        in_specs=[pl.BlockSpec((tm, c), lambda i: (i, 0)),
                  pl.BlockSpec((gw, gw), lambda i: (0, 0)),
                  pl.BlockSpec((gw, gw), lambda i: (0, 0)),
                  pl.BlockSpec(w_fnet.shape, lambda i: (0, 0, 0))],
        out_specs=[pl.BlockSpec((tm, c), lambda i: (i, 0))] * 2
                  + [pl.BlockSpec((None, 1, c), lambda i: (i // tiles_per_seq, 0, 0))],
        out_shape=[jax.ShapeDtypeStruct((r, c), BF16)] * 2 + [jax.ShapeDtypeStruct((nb, 1, c), F32)],
        scratch_shapes=[pltpu.VMEM((FNET_GROUPS, gw, 2 * gw), BF16)],
        compiler_params=_params(("arbitrary",), 3 * _nbytes((tm, c), BF16) + 6 * _nbytes((gw, gw), F32),
                                4 * _nbytes((tm, c), F32)),
        name="dft_channels",
    )(fx, cc, sc, w_fnet)
    cn, sn = seq_tables
    nn = n // tn
    blocks = 2 * _nbytes((tk, tn), BF16) + 2 * _nbytes((nb, tn, c), BF16) + 2 * _nbytes((nb, tk, c), BF16)
    nt = half // tk
    y_lo, y_hi = pl.pallas_call(
        functools.partial(_dft_seq_kernel, nn=nn),
        grid=(nt, nn),
        in_specs=[pl.BlockSpec((tk, tn), lambda i, j: (nt - 1 - i, j)),
                  pl.BlockSpec((tk, tn), lambda i, j: (nt - 1 - i, j)),
                  pl.BlockSpec((nb, tn, c), lambda i, j: (0, j, 0)),
                  pl.BlockSpec((nb, tn, c), lambda i, j: (0, j, 0)),
                  pl.BlockSpec((nb, 1, c), lambda i, j: (0, 0, 0))],
        out_specs=[pl.BlockSpec((nb, tk, c), lambda i, j: (0, nt - 1 - i, 0)),
                   pl.BlockSpec((nb, tk, c), lambda i, j: (0, i, 0))],
        out_shape=[jax.ShapeDtypeStruct((nb, half, c), BF16)] * 2,
        scratch_shapes=[pltpu.VMEM((nb, tk, c), F32)] * 2 + [pltpu.VMEM((nb, 1, c), F32)],
        compiler_params=_params(("arbitrary", "arbitrary"), blocks, 4 * _nbytes((nb, tk, c), F32)),
        name="dft_positions",
    )(cn, sn, ap.reshape(nb, n, c), aq.reshape(nb, n, c), alt)
    return jnp.concatenate([y_lo, y_hi], axis=1).reshape(r, c)


def _attend(q, segments):
    dn = (((1,), (1,)), ((), ()))
    scores = []
    for k, _, bias in segments:
        s = lax.dot_general(q, k, dn, preferred_element_type=F32)
        scores.append(s if bias is None else s + bias)
    m = functools.reduce(jnp.maximum, [jnp.max(s, axis=-1, keepdims=True) for s in scores])
    probs = [jnp.exp2(s - m) for s in scores]
    l = sum(jnp.sum(p, axis=-1, keepdims=True) for p in probs)
    o = sum(jnp.dot(p.astype(BF16), v, preferred_element_type=F32) for p, (_, v, _) in zip(probs, segments))
    return o / l


def _ctx_attn_kernel(q_ref, kc_ref, vc_ref, o_ref, *, group):
    segments = [(kc_ref[...], vc_ref[...], None)]
    for g in range(group):
        hs = slice(g * HEAD_DIM, (g + 1) * HEAD_DIM)
        o_ref[:, hs] = _attend(q_ref[:, hs], segments).astype(o_ref.dtype)


def context_attention(q, nb, nq, kv_heads, k_ctx, v_ctx, m_ctx):
    group = q.shape[1] // HEAD_DIM // kv_heads
    gd = group * HEAD_DIM
    ctx = pl.BlockSpec((m_ctx, HEAD_DIM), lambda b, h: (b, h))
    blocks = 2 * _nbytes((nq, gd), BF16) + 2 * _nbytes((m_ctx, HEAD_DIM), BF16)
    return pl.pallas_call(
        functools.partial(_ctx_attn_kernel, group=group),
        grid=(nb, kv_heads),
        in_specs=[pl.BlockSpec((nq, gd), lambda b, h: (b, h)), ctx, ctx],
        out_specs=pl.BlockSpec((nq, gd), lambda b, h: (b, h)),
        out_shape=jax.ShapeDtypeStruct(q.shape, BF16),
        compiler_params=_params(("parallel", "parallel"), blocks, 6 * _nbytes((nq, m_ctx), F32)),
        name="context_attention",
    )(q, k_ctx, v_ctx)


def _dense_attn_kernel(*refs, group, n_casts):
    q_ref, kl_ref, vtl_ref, kc_ref, vtc_ref = refs[:5]
    cast_src = refs[5:5 + n_casts]
    o_ref = refs[5 + n_casts]
    cast_dst = refs[6 + n_casts:6 + 2 * n_casts]
    s_ref = refs[6 + 2 * n_casts]
    dn = (((1,), (1,)), ((), ()))
    k = jnp.concatenate([kl_ref[...], kc_ref[...]], axis=0)
    vt = jnp.concatenate([vtl_ref[...], vtc_ref[...]], axis=1)
    m = []
    for g in range(group):
        s = lax.dot_general(k, q_ref[:, g * HEAD_DIM:(g + 1) * HEAD_DIM], dn, preferred_element_type=F32)
        s_ref[g] = s
        m.append(jnp.max(s, axis=0, keepdims=True))
    for g in range(group):
        p = jnp.exp2(s_ref[g] - m[g])
        l = jnp.sum(p, axis=0, keepdims=True)
        o = jnp.dot(vt, p.astype(BF16), preferred_element_type=F32)
        o_ref[:, g * HEAD_DIM:(g + 1) * HEAD_DIM] = (o / l).T.astype(o_ref.dtype)
    for src, dst in zip(cast_src, cast_dst):
        dst[...] = src[...].astype(dst.dtype)


def dense_attention(q, nb, nq, kv_heads, k_ctx, v_ctx, m_ctx, k_lat, v_lat, casts=(), *, tq=512):
    heads = q.shape[1] // HEAD_DIM
    group = heads // kv_heads
    nqt = nq // tq
    gd = group * HEAD_DIM
    keys = nq + m_ctx
    qmap = lambda b, h, i: (b * nqt + i, h)
    lat = pl.BlockSpec((nq, HEAD_DIM), lambda b, h, i: (b, h))
    ctx = pl.BlockSpec((m_ctx, HEAD_DIM), lambda b, h, i: (b, h))
    lat_t = pl.BlockSpec((HEAD_DIM, nq), lambda b, h, i: (b * kv_heads + h, 0))
    ctx_t = pl.BlockSpec((HEAD_DIM, m_ctx), lambda b, h, i: (b * kv_heads + h, 0))
    blocks = 2 * _nbytes((tq, gd), BF16) + 2 * _nbytes((keys, HEAD_DIM), BF16)
    steps = nb * kv_heads * nqt
    cast_specs_in, cast_specs_out, cast_shapes = [], [], []
    for w, layer in casts:
        _, rows_w, cols_w = w.shape
        col_blocks = next(c for c in (1, 2, 4, 8) if steps % c == 0 and rows_w % (steps // c) == 0
                          and (rows_w // (steps // c)) % BF16_SUBLANES == 0 and cols_w % (c * LANES) == 0)
        slab = (rows_w // (steps // col_blocks), cols_w // col_blocks)

        def slab_index(b, h, i, col_blocks=col_blocks):
            step = (b * kv_heads + h) * nqt + i
            return step // col_blocks, step % col_blocks

        cast_specs_in.append(pl.BlockSpec((None,) + slab, lambda b, h, i, f=slab_index, layer=layer: (layer,) + f(b, h, i)))
        cast_specs_out.append(pl.BlockSpec((None,) + slab, lambda b, h, i, f=slab_index: (0,) + f(b, h, i)))
        cast_shapes.append(jax.ShapeDtypeStruct((1, rows_w, cols_w), BF16))
        blocks += _nbytes(slab, F32) + _nbytes(slab, BF16)
    outs = pl.pallas_call(
        functools.partial(_dense_attn_kernel, group=group, n_casts=len(casts)),
        grid=(nb, kv_heads, nqt),
        in_specs=[pl.BlockSpec((tq, gd), qmap), lat, lat_t, ctx, ctx_t] + cast_specs_in,
        out_specs=[pl.BlockSpec((tq, gd), qmap)] + cast_specs_out,
        out_shape=[jax.ShapeDtypeStruct(q.shape, BF16)] + cast_shapes,
        scratch_shapes=[pltpu.VMEM((group, keys, tq), F32)],
        compiler_params=_params(("parallel", "parallel", "parallel"), blocks, 2 * _nbytes((group, keys, tq), F32)),
        name="dense_attention",
    )(q, k_lat, v_lat, k_ctx, v_ctx, *[w for w, _ in casts])
    return outs[0], list(outs[1:])


def _na_plan(rows, rq):
    wr = min(NA_WIN_R, rows)
    band = min(rq + wr - 1, rows)
    nblk = rows // rq
    starts = [int(np.clip(rq * j - wr // 2, 0, rows - band)) for j in range(nblk)]
    deltas = [starts[j] - rq * j for j in range(nblk)]
    uniq = sorted(set(deltas))
    var = [uniq.index(dl) for dl in deltas]
    reps = [deltas.index(dl) for dl in uniq]
    return starts, var, band, reps


def _na_bias_kernel(rpb_ref, o_ref, *, rows, rq, band, starts, reps):
    h = pl.program_id(0)
    wr = min(NA_WIN_R, rows)
    qc = lax.broadcasted_iota(jnp.int32, (GRID_W, GRID_W), 0)
    kc = lax.broadcasted_iota(jnp.int32, (GRID_W, GRID_W), 1)
    dc = kc - qc + (NA_WIN_C - 1)
    cs = jnp.clip(qc - NA_WIN_C // 2, 0, GRID_W - NA_WIN_C)
    col_ok = (kc >= cs) & (kc < cs + NA_WIN_C)
    masked = jnp.full((GRID_W, GRID_W), MASK_VALUE, F32)
    by_dr = []
    for dr in range(2 * NA_WIN_R - 1):
        t = masked
        for d in range(2 * NA_WIN_C - 1):
            t = jnp.where(dc == d, rpb_ref[h, dr, d] * LOG2E, t)
        by_dr.append(jnp.where(col_ok, t, MASK_VALUE))
    for v, j in enumerate(reps):
        for e in range(rq):
            qr = rq * j + e
            rs = int(np.clip(qr - wr // 2, 0, rows - wr))
            for a in range(band):
                kr = starts[j] + a
                blk = by_dr[kr - qr + NA_WIN_R - 1] if rs <= kr < rs + wr else masked
                o_ref[v, e * GRID_W:(e + 1) * GRID_W, a * GRID_W:(a + 1) * GRID_W] = blk


def _na_bias(rpb, rows, rq, band, starts, reps):
    nv = len(reps)
    tq, bk = rq * GRID_W, band * GRID_W
    return pl.pallas_call(
        functools.partial(_na_bias_kernel, rows=rows, rq=rq, band=band, starts=starts, reps=reps),
        grid=(NA_HEADS,),
        in_specs=[pl.BlockSpec(memory_space=pltpu.SMEM)],
        out_specs=pl.BlockSpec((nv, None, tq, bk), lambda h: (0, h, 0, 0)),
        out_shape=jax.ShapeDtypeStruct((nv, NA_HEADS, tq, bk), F32),
        compiler_params=_params(("parallel",), _nbytes((nv, tq, bk), F32)),
        name="na_bias",
    )(rpb)


def _na_kernel(start_ref, var_ref, q_ref, k_ref, v_ref, kc_ref, vc_ref, bias_ref, o_ref, s_ref, *, band_keys):
    del var_ref
    j = pl.program_id(1)
    st = pl.multiple_of(start_ref[j] * GRID_W, GRID_W)
    n_ctx = kc_ref.shape[0]
    dn = (((1,), (1,)), ((), ()))
    m = []
    for h in range(NA_HEADS):
        hs = slice(h * HEAD_DIM, (h + 1) * HEAD_DIM)
        q = q_ref[:, hs]
        s_ctx = lax.dot_general(q, kc_ref[:, hs], dn, preferred_element_type=F32)
        s_loc = lax.dot_general(q, k_ref[pl.ds(st, band_keys), hs], dn, preferred_element_type=F32) + bias_ref[h]
        s_ref[h, :, :n_ctx] = s_ctx
        s_ref[h, :, n_ctx:] = s_loc
        m.append(jnp.maximum(jnp.max(s_ctx, axis=-1, keepdims=True), jnp.max(s_loc, axis=-1, keepdims=True)))
    for h in range(NA_HEADS):
        hs = slice(h * HEAD_DIM, (h + 1) * HEAD_DIM)
        p = jnp.exp2(s_ref[h] - m[h])
        l = jnp.sum(p, axis=-1, keepdims=True)
        p = p.astype(BF16)
        o = (jnp.dot(p[:, :n_ctx], vc_ref[:, hs], preferred_element_type=F32)
             + jnp.dot(p[:, n_ctx:], v_ref[pl.ds(st, band_keys), hs], preferred_element_type=F32))
        o_ref[:, hs] = (o / l).astype(o_ref.dtype)


def neighbourhood_attention(q, k, v, k_ctx, v_ctx, nb, n, m_ctx, rpb, *, rq=4):
    rows = n // GRID_W
    rq = min(rq, rows)
    starts, var, band, reps = _na_plan(rows, rq)
    bias = _na_bias(rpb, rows, rq, band, starts, reps)
    tq, band_keys = rq * GRID_W, band * GRID_W
    nblk = rows // rq
    c = q.shape[1]
    blocks = (_nbytes((tq, c), BF16) * 2 + 2 * _nbytes((n, c), BF16) + 2 * _nbytes((m_ctx, c), BF16)
              + _nbytes((NA_HEADS, tq, band_keys), F32))
    grid_spec = pltpu.PrefetchScalarGridSpec(
        num_scalar_prefetch=2,
        grid=(nb, nblk),
        in_specs=[pl.BlockSpec((tq, c), lambda b, j, st, vr: (b * nblk + j, 0)),
                  pl.BlockSpec((n, c), lambda b, j, st, vr: (b, 0)),
                  pl.BlockSpec((n, c), lambda b, j, st, vr: (b, 0)),
                  pl.BlockSpec((m_ctx, c), lambda b, j, st, vr: (b, 0)),
                  pl.BlockSpec((m_ctx, c), lambda b, j, st, vr: (b, 0)),
                  pl.BlockSpec((None, NA_HEADS, tq, band_keys), lambda b, j, st, vr: (vr[j], 0, 0, 0))],
        out_specs=pl.BlockSpec((tq, c), lambda b, j, st, vr: (b * nblk + j, 0)),
        scratch_shapes=[pltpu.VMEM((NA_HEADS, tq, m_ctx + band_keys), F32)],
    )
    return pl.pallas_call(
        functools.partial(_na_kernel, band_keys=band_keys),
        grid_spec=grid_spec,
        out_shape=jax.ShapeDtypeStruct(q.shape, BF16),
        compiler_params=_params(("parallel", "arbitrary"), blocks, 6 * _nbytes((tq, band_keys + m_ctx), F32)),
        name="neighbourhood_attention",
    )(jnp.asarray(starts, jnp.int32), jnp.asarray(var, jnp.int32), q, k, v, k_ctx, v_ctx, bias)


def _outproj_kernel(x_ref, gt_ref, y0_ref, y1_ref, y2_ref, y3_ref, w_ref, o_ref):
    y = jnp.concatenate([y0_ref[...], y1_ref[...], y2_ref[...], y3_ref[...]], axis=-1)
    o_ref[...] = x_ref[...] + gt_ref[...] * jnp.dot(y, w_ref[...], preferred_element_type=F32)


def out_projection(x, mod3, k_gate, row_of_tile, ys, layer, w_out, *, tm=512):
    r, d = x.shape
    tm = min(tm, r)
    row = lambda i: (i, 0)
    blocks = 2 * _nbytes((tm, d), F32) + _nbytes((tm, d), BF16) + _nbytes(w_out.shape[1:], BF16) // 2
    return pl.pallas_call(
        _outproj_kernel,
        grid=(r // tm,),
        in_specs=[pl.BlockSpec((tm, d), row), _mod_spec(d, row_of_tile, k_gate)]
                 + [pl.BlockSpec((tm, MIX_GROUP), row)] * 4
                 + [pl.BlockSpec((None,) + w_out.shape[1:], lambda i: (layer, 0, 0), pipeline_mode=pl.Buffered(1))],
        out_specs=pl.BlockSpec((tm, d), row),
        out_shape=jax.ShapeDtypeStruct((r, d), F32),
        compiler_params=_params(("parallel",), blocks, 3 * _nbytes((tm, d), F32)),
        name="out_projection",
    )(x, mod3, *ys, w_out)


def _rope_tables(n):
    t = jnp.arange(n, dtype=jnp.int32)
    row = (t // GRID_W).astype(F32)
    col = (t % GRID_W).astype(F32)
    n_freq = HEAD_DIM // 4
    inv = 1.0 / (ROPE_THETA ** (jnp.arange(n_freq, dtype=F32) / n_freq))
    ang = jnp.concatenate([row[:, None] * inv, col[:, None] * inv], axis=-1)
    cos = jnp.repeat(jnp.cos(ang), 2, axis=-1)
    sin = jnp.repeat(jnp.sin(ang), 2, axis=-1)
    sign = jnp.tile(jnp.asarray([-1.0, 1.0], F32), HEAD_DIM // 2)
    return cos, sin * sign


def kernel(x, c, ctx, c_ctx, w_mod, b_mod, ffn1_norm, ffn1_gate, ffn1_up, ffn1_down, mix_norm, w_in, w_out,
           pool_w, pool_scale, q_norm, k_norm, fnet_w, na_rpb, ffn2_norm, ffn2_gate, ffn2_up, ffn2_down, final_norm):
    nb, n, d = x.shape
    m = ctx.shape[1]
    depth = w_mod.shape[0]
    assert nb + 1 <= MOD_ROWS

    cvec = jnp.zeros((MOD_ROWS, d), F32).at[:nb].set(c).at[nb].set(c_ctx)
    mod = modulation(cvec, w_mod, b_mod)

    rope = _rope_tables(n)
    seq_tables = _dft_tables(n, BF16, rows=n // 2)
    ctx_tables = _dft_tables(m, BF16, rows=m // 2)
    chan_tables = _dft_tables(MIX_GROUP // FNET_GROUPS, F32)

    f1 = tuple(cast_layer(w, 0) for w in (ffn1_gate, ffn1_up, ffn1_down))
    w_in_b = cast_layer(w_in, 0)
    pool_w_b = pool_w.astype(BF16)
    xl = x.reshape(nb * n, d)
    xc = ctx.reshape(nb * m, d)
    ctx_row = lambda i: nb

    def lat_row(tm):
        return lambda i: (i * tm) // n

    for l in range(depth):
        last = l == depth - 1
        mod3 = mod[l].reshape(MOD_ROWS, 1, N_MOD * d)

        xl = ffn(xl, mod3, 0, lat_row(FFN_TM), ffn1_norm[l], 0, *f1, tm=FFN_TM)
        xc = ffn(xc, mod3, 0, ctx_row, ffn1_norm[l], 0, *f1, tm=FFN_TM)

        px, gq, gk, _, fx, nq, nk, nv, gvt = in_projection(
            xl, n, mod3, 3, lat_row(512), mix_norm[l], 0, w_in_b, q_norm[l], k_norm[l], rope)
        pc, gqc, gkc, gvc, fc, nqc, nkc, nvc, gvct = in_projection(
            xc, m, mod3, 3, ctx_row, mix_norm[l], 0, w_in_b, q_norm[l], k_norm[l])

        casts = [(w, l) for w in (w_out, ffn2_gate, ffn2_up, ffn2_down)]
        if not last:
            casts += [(w, l + 1) for w in (ffn1_gate, ffn1_up, ffn1_down, w_in)]
        y_gqa, cast_w = dense_attention(gq, nb, n, GQA_KV_HEADS, gkc, gvct, m, gk, gvt, casts)
        w_out_b, f2 = cast_w[0], cast_w[1:4]
        if not last:
            f1, w_in_b = cast_w[4:7], cast_w[7]

        ys = (pool_mix(px, n, pool_w_b[l], pool_scale[l]),
              y_gqa,
              fourier_mix(fx, nb, n, fnet_w[l], seq_tables, chan_tables),
              neighbourhood_attention(nq, nk, nv, nkc, nvc, nb, n, m, na_rpb[l]))
        xl = out_projection(xl, mod3, 5, lat_row(512), ys, 0, w_out_b)

        if not last:
            ycs = (pool_mix(pc, m, pool_w_b[l], pool_scale[l]),
                   context_attention(gqc, nb, m, GQA_KV_HEADS, gkc, gvc, m),
                   fourier_mix(fc, nb, m, fnet_w[l], ctx_tables, chan_tables),
                   context_attention(nqc, nb, m, NA_HEADS, nkc, nvc, m))
            xc = out_projection(xc, mod3, 5, ctx_row, ycs, 0, w_out_b)
            xc = ffn(xc, mod3, 6, ctx_row, ffn2_norm[l], 0, *f2, tm=FFN_TM)

        xl = ffn(xl, mod3, 6, lat_row(FFN_TM), ffn2_norm[l], 0, *f2, final_g=final_norm if last else None, tm=FFN_TM)
    return xl.reshape(nb, n, d)
```

```python
import functools

import numpy as np
import jax
import jax.numpy as jnp
from jax import lax
from jax.experimental import pallas as pl
from jax.experimental.pallas import tpu as pltpu

F32 = jnp.float32
BF16 = jnp.bfloat16

GRID_W = 64
HEAD_DIM = 128
POOL_WINDOWS = (2, 4, 8, 16)
POOL_HALO = 8
GQA_Q_HEADS = 4
GQA_KV_HEADS = 2
FNET_GROUPS = 4
NA_HEADS = 4
NA_WIN_R = 8
NA_WIN_C = 16
ROPE_THETA = 10000.0
IN_SPLITS = (512, 512, 256, 256, 512, 512, 512, 512)
MIX_GROUP = 512
EPS = 1e-6
N_MOD = 9
MOD_ROWS = 8
MASK_VALUE = -1e30
LOG2E = float(np.log2(np.e))

V7X_VMEM_BYTES = 64 * 1024 * 1024
VMEM_CAP = V7X_VMEM_BYTES - 4 * 1024 * 1024
FFN_TM = 1024
BF16_SUBLANES = 16
LANES = 128


def _params(semantics, block_bytes, scratch_bytes=0):
    need = 2 * block_bytes + scratch_bytes
    limit = min(max(need + need // 2, 32 * 1024 * 1024), VMEM_CAP)
    return pltpu.CompilerParams(dimension_semantics=semantics, vmem_limit_bytes=limit)


def _nbytes(shape, dtype):
    return int(np.prod(shape)) * jnp.dtype(dtype).itemsize


def _norm_modulate(x, gain, shift, scale):
    inv = lax.rsqrt(jnp.mean(x * x, axis=-1, keepdims=True) + EPS)
    return (x * inv) * (gain * (1.0 + scale)) + shift


def _mod_spec(d, row_of_tile, k):
    return pl.BlockSpec((None, 1, d), lambda i, *_: (row_of_tile(i), 0, k))


def _mod_kernel(c_ref, w_ref, b_ref, o_ref):
    c = c_ref[...]
    s = c * jax.nn.sigmoid(c)
    o_ref[...] = jnp.dot(s.astype(BF16), w_ref[...].astype(BF16), preferred_element_type=F32) + b_ref[...]


def modulation(cvec, w_mod, b_mod, *, tn=1024):
    nl, d, nd = w_mod.shape
    blocks = _nbytes((d, tn), F32) + _nbytes((MOD_ROWS, d), F32) + 2 * _nbytes((MOD_ROWS, tn), F32)
    return pl.pallas_call(
        _mod_kernel,
        grid=(nl, nd // tn),
        in_specs=[pl.BlockSpec((MOD_ROWS, d), lambda l, j: (0, 0)),
                  pl.BlockSpec((None, d, tn), lambda l, j: (l, 0, j)),
                  pl.BlockSpec((None, 1, tn), lambda l, j: (l, 0, j))],
        out_specs=pl.BlockSpec((None, MOD_ROWS, tn), lambda l, j: (l, 0, j)),
        out_shape=jax.ShapeDtypeStruct((nl, MOD_ROWS, nd), F32),
        compiler_params=_params(("parallel", "parallel"), blocks, _nbytes((d, tn), BF16)),
        name="modulation",
    )(cvec, w_mod, b_mod.reshape(nl, 1, nd))


def _cast_kernel(src_ref, dst_ref):
    dst_ref[...] = src_ref[...].astype(dst_ref.dtype)


def cast_layer(w, layer, *, rows=512):
    _, a, b = w.shape
    rows = min(rows, a)
    return pl.pallas_call(
        _cast_kernel,
        grid=(a // rows,),
        in_specs=[pl.BlockSpec((None, rows, b), lambda i: (layer, i, 0))],
        out_specs=pl.BlockSpec((None, rows, b), lambda i: (0, i, 0)),
        out_shape=jax.ShapeDtypeStruct((1, a, b), BF16),
        compiler_params=_params(("parallel",), _nbytes((rows, b), F32) + _nbytes((rows, b), BF16)),
        name="cast_layer",
    )(w)


def _ffn_kernel(*refs, nj, final):
    if final:
        x_ref, sh_ref, sc_ref, gt_ref, ng_ref, wg_ref, wu_ref, wd_ref, fg_ref, o_ref, h_ref = refs
    else:
        x_ref, sh_ref, sc_ref, gt_ref, ng_ref, wg_ref, wu_ref, wd_ref, o_ref, h_ref = refs
    j = pl.program_id(1)

    def partial_out():
        h = h_ref[...]
        g = jnp.dot(h, wg_ref[...], preferred_element_type=F32)
        u = jnp.dot(h, wu_ref[...], preferred_element_type=F32)
        a = (g * jax.nn.sigmoid(g) * u).astype(BF16)
        return jnp.dot(a, wd_ref[...], preferred_element_type=F32)

    @pl.when(j == 0)
    def _():
        h_ref[...] = _norm_modulate(x_ref[...], ng_ref[...], sh_ref[...], sc_ref[...]).astype(BF16)
        o_ref[...] = partial_out()

    @pl.when((j > 0) & (j < nj - 1))
    def _():
        o_ref[...] += partial_out()

    @pl.when(j == nj - 1)
    def _():
        r = x_ref[...] + 0.5 * gt_ref[...] * (o_ref[...] + partial_out())
        if final:
            r = r * lax.rsqrt(jnp.mean(r * r, axis=-1, keepdims=True) + EPS) * fg_ref[...]
        o_ref[...] = r


def ffn(x, mod3, k0, row_of_tile, norm_g, layer, wg, wu, wd, final_g=None, *, tm=512, tf=512):
    r, d = x.shape
    f = wg.shape[2]
    tm = min(tm, r)
    nj = f // tf
    assert nj >= 2
    final = final_g is not None
    row = lambda i, j: (i, 0)
    in_specs = [pl.BlockSpec((tm, d), row),
                _mod_spec(d, row_of_tile, k0), _mod_spec(d, row_of_tile, k0 + 1), _mod_spec(d, row_of_tile, k0 + 2),
                pl.BlockSpec((1, d), lambda i, j: (0, 0)),
                pl.BlockSpec((None, d, tf), lambda i, j: (layer, 0, j)),
                pl.BlockSpec((None, d, tf), lambda i, j: (layer, 0, j)),
                pl.BlockSpec((None, tf, d), lambda i, j: (layer, j, 0))]
    args = [x, mod3, mod3, mod3, norm_g.reshape(1, d), wg, wu, wd]
    if final:
        in_specs.append(pl.BlockSpec((1, d), lambda i, j: (0, 0)))
        args.append(final_g.reshape(1, d))
    blocks = 2 * _nbytes((tm, d), F32) + 3 * _nbytes((d, tf), BF16) + 5 * _nbytes((1, d), F32)
    temps = _nbytes((tm, d), BF16) + 4 * _nbytes((tm, tf), F32) + _nbytes((tm, d), F32)
    return pl.pallas_call(
        functools.partial(_ffn_kernel, nj=nj, final=final),
        grid=(r // tm, nj),
        in_specs=in_specs,
        out_specs=pl.BlockSpec((tm, d), row),
        out_shape=jax.ShapeDtypeStruct((r, d), F32),
        scratch_shapes=[pltpu.VMEM((tm, d), BF16)],
        compiler_params=_params(("parallel", "arbitrary"), blocks, temps),
        name="ffn",
    )(*args)


def _rms_heads(z, gain, nheads):
    outs = []
    for hd in range(nheads):
        zh = z[:, hd * HEAD_DIM:(hd + 1) * HEAD_DIM]
        outs.append(zh * lax.rsqrt(jnp.mean(zh * zh, axis=-1, keepdims=True) + EPS) * gain)
    return outs


def _rope(zh, cos, sin_signed):
    lane = lax.broadcasted_iota(jnp.int32, zh.shape, 1)
    partner = jnp.where(lane % 2 == 0, pltpu.roll(zh, HEAD_DIM - 1, 1), pltpu.roll(zh, 1, 1))
    return zh * cos + partner * sin_signed


def _inproj_kernel(*refs, rope):
    if rope:
        (x_ref, sh_ref, sc_ref, ng_ref, w_ref, qn_ref, kn_ref, cos_ref, sin_ref,
         px_ref, gq_ref, gk_ref, gv_ref, fx_ref, nq_ref, nk_ref, nv_ref, gvt_ref) = refs
    else:
        (x_ref, sh_ref, sc_ref, ng_ref, w_ref, qn_ref, kn_ref,
         px_ref, gq_ref, gk_ref, gv_ref, fx_ref, nq_ref, nk_ref, nv_ref, gvt_ref) = refs
    h = _norm_modulate(x_ref[...], ng_ref[...], sh_ref[...], sc_ref[...]).astype(BF16)
    offs = np.cumsum((0,) + IN_SPLITS)

    def proj(s):
        return jnp.dot(h, w_ref[:, offs[s]:offs[s + 1]], preferred_element_type=F32)

    sm_scale = HEAD_DIM ** -0.5 * LOG2E
    px_ref[...] = proj(0)
    q_heads = _rms_heads(proj(1), qn_ref[...], GQA_Q_HEADS)
    k_heads = _rms_heads(proj(2), kn_ref[...], GQA_KV_HEADS)
    if rope:
        cos, sin = cos_ref[...], sin_ref[...]
        q_heads = [_rope(zh, cos, sin) for zh in q_heads]
        k_heads = [_rope(zh, cos, sin) for zh in k_heads]
    gq_ref[...] = jnp.concatenate([zh * sm_scale for zh in q_heads], axis=-1).astype(BF16)
    gk_ref[...] = jnp.concatenate(k_heads, axis=-1).astype(BF16)
    gv = proj(3)
    gv_ref[...] = gv.astype(BF16)
    gvt_ref[...] = gv.T.astype(BF16)
    fx_ref[...] = proj(4).astype(BF16)
    nq_ref[...] = (proj(5) * sm_scale).astype(BF16)
    nk_ref[...] = proj(6).astype(BF16)
    nv_ref[...] = proj(7).astype(BF16)


def in_projection(x, seq_len, mod3, k0, row_of_tile, norm_g, layer, w_in, q_norm, k_norm, rope_tables=None, *, tm=512):
    r, d = x.shape
    tm = min(tm, seq_len)
    tiles_per_seq = seq_len // tm
    rope = rope_tables is not None
    row = lambda i: (i, 0)
    const = lambda i: (0, 0)
    in_specs = [pl.BlockSpec((tm, d), row),
                _mod_spec(d, row_of_tile, k0), _mod_spec(d, row_of_tile, k0 + 1),
                pl.BlockSpec((1, d), const),
                pl.BlockSpec((None,) + w_in.shape[1:], lambda i: (layer, 0, 0), pipeline_mode=pl.Buffered(1)),
                pl.BlockSpec((1, HEAD_DIM), const), pl.BlockSpec((1, HEAD_DIM), const)]
    args = [x, mod3, mod3, norm_g.reshape(1, d), w_in, q_norm.reshape(1, HEAD_DIM), k_norm.reshape(1, HEAD_DIM)]
    if rope:
        tab = pl.BlockSpec((tm, HEAD_DIM), lambda i: (i % tiles_per_seq, 0))
        in_specs += [tab, tab]
        args += list(rope_tables)
    widths = (512, 512, 256, 256, 512, 512, 512, 512)
    dtypes = (F32,) + (BF16,) * 7
    out_specs = [pl.BlockSpec((tm, w), row) for w in widths]
    out_shape = [jax.ShapeDtypeStruct((r, w), dt) for w, dt in zip(widths, dtypes)]
    kv_width = widths[3]
    out_specs.append(pl.BlockSpec((kv_width, tm), lambda i: (i // tiles_per_seq, i % tiles_per_seq)))
    out_shape.append(jax.ShapeDtypeStruct((r // seq_len * kv_width, seq_len), BF16))
    blocks = _nbytes((tm, d), F32) + _nbytes(w_in.shape[1:], BF16) // 2 + _nbytes((tm, 4096), F32)
    temps = _nbytes((tm, d), BF16) + 2 * _nbytes((tm, d), F32) + 6 * _nbytes((tm, 512), F32)
    return pl.pallas_call(
        functools.partial(_inproj_kernel, rope=rope),
        grid=(r // tm,),
        in_specs=in_specs,
        out_specs=out_specs,
        out_shape=out_shape,
        compiler_params=_params(("parallel",), blocks, temps),
        name="in_projection",
    )(*args)


def _pool_kernel(prev_ref, cur_ref, next_ref, w_ref, sc_ref, o_ref, *, n, ts):
    tiles_per_seq = n // ts
    s = pl.program_id(0) % tiles_per_seq
    cur = cur_ref[...]
    prev = jnp.where(s == 0, 0.0, prev_ref[...])
    nxt = jnp.where(s == tiles_per_seq - 1, 0.0, next_ref[...])
    ext = jnp.concatenate([prev, cur, nxt], axis=0)
    rows = ts + 2 * POOL_HALO
    t = s * ts + lax.broadcasted_iota(jnp.int32, (ts, 1), 0)
    gw = cur.shape[1] // len(POOL_WINDOWS)
    for gi, w in enumerate(POOL_WINDOWS):
        e = ext[:, gi * gw:(gi + 1) * gw]
        width = 1
        while width < w:
            e = e + pltpu.roll(e, rows - width, 0)
            width *= 2
        lead = POOL_HALO - w // 2
        win = (pltpu.roll(e, rows - lead, 0) if lead else e)[:ts]
        lo = jnp.maximum(t - w // 2, 0)
        hi = jnp.minimum(t + w // 2 - 1, n - 1)
        diff = win / (hi - lo + 1).astype(F32) - cur[:, gi * gw:(gi + 1) * gw]
        y = jnp.dot(diff.astype(BF16), w_ref[gi], preferred_element_type=F32)
        o_ref[:, gi * gw:(gi + 1) * gw] = (y * sc_ref[:, gi * gw:(gi + 1) * gw]).astype(o_ref.dtype)


def pool_mix(px, n, w_pool, scale, *, ts=512):
    r, c = px.shape
    ts = min(ts, n)
    hb = ts // POOL_HALO
    last_halo = r // POOL_HALO - 1
    blocks = 2 * _nbytes((ts, c), F32) + _nbytes(w_pool.shape, BF16)
    return pl.pallas_call(
        functools.partial(_pool_kernel, n=n, ts=ts),
        grid=(r // ts,),
        in_specs=[pl.BlockSpec((POOL_HALO, c), lambda i: (jnp.maximum(i * hb - 1, 0), 0)),
                  pl.BlockSpec((ts, c), lambda i: (i, 0)),
                  pl.BlockSpec((POOL_HALO, c), lambda i: (jnp.minimum((i + 1) * hb, last_halo), 0)),
                  pl.BlockSpec(w_pool.shape, lambda i: (0, 0, 0)),
                  pl.BlockSpec((1, c), lambda i: (0, 0))],
        out_specs=pl.BlockSpec((ts, c), lambda i: (i, 0)),
        out_shape=jax.ShapeDtypeStruct((r, c), BF16),
        compiler_params=_params(("parallel",), blocks, 6 * _nbytes((ts + 2 * POOL_HALO, c), F32)),
        name="pool_mix",
    )(px, px, px, w_pool, scale.reshape(1, c))


def _dft_chan_kernel(u_ref, c_ref, s_ref, w_ref, ap_ref, aq_ref, alt_ref, pq_ref, *, norm, tiles_per_seq):
    gw = c_ref.shape[0]
    i = pl.program_id(0)

    @pl.when(i == 0)
    def _():
        for g in range(FNET_GROUPS):
            w = w_ref[g]
            p = jnp.dot(c_ref[...], w, preferred_element_type=F32, precision=lax.Precision.HIGHEST)
            q = jnp.dot(s_ref[...], w, preferred_element_type=F32, precision=lax.Precision.HIGHEST)
            pq_ref[g] = (jnp.concatenate([p, -q], axis=-1) * norm).astype(BF16)

    sign = (1 - 2 * (lax.broadcasted_iota(jnp.int32, (u_ref.shape[0], 1), 0) % 2)).astype(F32)
    alt = []
    for g in range(FNET_GROUPS):
        a = jnp.dot(u_ref[:, g * gw:(g + 1) * gw], pq_ref[g], preferred_element_type=F32)
        ap_ref[:, g * gw:(g + 1) * gw] = a[:, :gw].astype(BF16)
        aq_ref[:, g * gw:(g + 1) * gw] = a[:, gw:].astype(BF16)
        alt.append(jnp.sum(a[:, :gw] * sign, axis=0, keepdims=True))

    @pl.when(i % tiles_per_seq == 0)
    def _():
        alt_ref[...] = jnp.zeros_like(alt_ref)

    alt_ref[...] += jnp.concatenate(alt, axis=-1)


def _dft_seq_kernel(c_ref, s_ref, ap_ref, aq_ref, alt_ref, lo_ref, hi_ref, g_ref, h_ref, carry_ref, *, nn):
    i, j = pl.program_id(0), pl.program_id(1)
    nb, tk, _ = g_ref.shape

    @pl.when(j == 0)
    def _():
        g_ref[...] = jnp.zeros_like(g_ref)
        h_ref[...] = jnp.zeros_like(h_ref)

    @pl.when((i == 0) & (j == 0))
    def _():
        carry_ref[...] = alt_ref[...]

    for b in range(nb):
        g_ref[b] += jnp.dot(c_ref[...], ap_ref[b], preferred_element_type=F32)
        h_ref[b] += jnp.dot(s_ref[...], aq_ref[b], preferred_element_type=F32)

    @pl.when(j == nn - 1)
    def _():
        lo_ref[...] = (g_ref[...] + h_ref[...]).astype(lo_ref.dtype)
        row = lax.broadcasted_iota(jnp.int32, (tk, tk), 0)
        col = lax.broadcasted_iota(jnp.int32, (tk, tk), 1)
        mirror = ((row + col) % tk == 0).astype(BF16)
        first = lax.broadcasted_iota(jnp.int32, (tk, 1), 0) == 0
        for b in range(nb):
            d = g_ref[b] - h_ref[b]
            m = jnp.dot(mirror, d.astype(BF16), preferred_element_type=F32)
            hi_ref[b] = jnp.where(first, carry_ref[b], m).astype(hi_ref.dtype)
            carry_ref[b] = d[0:1, :]


def _dft_tables(n, dtype, rows=None, tile=512):
    rows = n if rows is None else rows

    def cos_sin(cols):
        k = jnp.arange(rows, dtype=jnp.int32)[:, None]
        ang = ((k * cols[None, :]) % n).astype(F32) * (2.0 * np.pi / n)
        return jnp.cos(ang), jnp.sin(ang)

    if n <= tile:
        c, s = cos_sin(jnp.arange(n, dtype=jnp.int32))
        return c.astype(dtype), s.astype(dtype)
    c1, s1 = cos_sin(jnp.arange(tile, dtype=jnp.int32))
    c0, s0 = cos_sin(jnp.arange(n // tile, dtype=jnp.int32) * tile)
    c = c0[:, :, None] * c1[:, None, :] - s0[:, :, None] * s1[:, None, :]
    s = s0[:, :, None] * c1[:, None, :] + c0[:, :, None] * s1[:, None, :]
    return c.reshape(rows, n).astype(dtype), s.reshape(rows, n).astype(dtype)


def fourier_mix(fx, nb, n, w_fnet, seq_tables, chan_tables, *, tm=512, tk=1024, tn=512):
    r, c = fx.shape
    gw = c // FNET_GROUPS
    half = n // 2
    tm, tk, tn = min(tm, n), min(tk, half), min(tn, n)
    cc, sc = chan_tables
    tiles_per_seq = n // tm
    ap, aq, alt = pl.pallas_call(
        functools.partial(_dft_chan_kernel, norm=float(1.0 / np.sqrt(n * gw)), tiles_per_seq=tiles_per_seq),
        grid=(r // tm,),
        in_specs=[pl.BlockSpec((tm, c), lambda i: (i, 0)),
                  pl.BlockSpec((gw, gw), lambda i: (0, 0)),
                  pl.BlockSpec((gw, gw), lambda i: (0, 0)),
                  pl.BlockSpec(w_fnet.shape, lambda i: (0, 0, 0))],
        out_specs=[pl.BlockSpec((tm, c), lambda i: (i, 0))] * 2
                  + [pl.BlockSpec((None, 1, c), lambda i: (i // tiles_per_seq, 0, 0))],
        out_shape=[jax.ShapeDtypeStruct((r, c), BF16)] * 2 + [jax.ShapeDtypeStruct((nb, 1, c), F32)],
        scratch_shapes=[pltpu.VMEM((FNET_GROUPS, gw, 2 * gw), BF16)],
        compiler_params=_params(("arbitrary",), 3 * _nbytes((tm, c), BF16) + 6 * _nbytes((gw, gw), F32),
                                4 * _nbytes((tm, c), F32)),
        name="dft_channels",
    )(fx, cc, sc, w_fnet)
    cn, sn = seq_tables
    nn = n // tn
    blocks = 2 * _nbytes((tk, tn), BF16) + 2 * _nbytes((nb, tn, c), BF16) + 2 * _nbytes((nb, tk, c), BF16)
    nt = half // tk
    y_lo, y_hi = pl.pallas_call(
        functools.partial(_dft_seq_kernel, nn=nn),
        grid=(nt, nn),
        in_specs=[pl.BlockSpec((tk, tn), lambda i, j: (nt - 1 - i, j)),
                  pl.BlockSpec((tk, tn), lambda i, j: (nt - 1 - i, j)),
                  pl.BlockSpec((nb, tn, c), lambda i, j: (0, j, 0)),
                  pl.BlockSpec((nb, tn, c), lambda i, j: (0, j, 0)),
                  pl.BlockSpec((nb, 1, c), lambda i, j: (0, 0, 0))],
        out_specs=[pl.BlockSpec((nb, tk, c), lambda i, j: (0, nt - 1 - i, 0)),
                   pl.BlockSpec((nb, tk, c), lambda i, j: (0, i, 0))],
        out_shape=[jax.ShapeDtypeStruct((nb, half, c), BF16)] * 2,
        scratch_shapes=[pltpu.VMEM((nb, tk, c), F32)] * 2 + [pltpu.VMEM((nb, 1, c), F32)],
        compiler_params=_params(("arbitrary", "arbitrary"), blocks, 4 * _nbytes((nb, tk, c), F32)),
        name="dft_positions",
    )(cn, sn, ap.reshape(nb, n, c), aq.reshape(nb, n, c), alt)
    return jnp.concatenate([y_lo, y_hi], axis=1).reshape(r, c)


def _attend(q, segments):
    dn = (((1,), (1,)), ((), ()))
    scores = []
    for k, _, bias in segments:
        s = lax.dot_general(q, k, dn, preferred_element_type=F32)
        scores.append(s if bias is None else s + bias)
    m = functools.reduce(jnp.maximum, [jnp.max(s, axis=-1, keepdims=True) for s in scores])
    probs = [jnp.exp2(s - m) for s in scores]
    l = sum(jnp.sum(p, axis=-1, keepdims=True) for p in probs)
    o = sum(jnp.dot(p.astype(BF16), v, preferred_element_type=F32) for p, (_, v, _) in zip(probs, segments))
    return o / l


def _ctx_attn_kernel(q_ref, kc_ref, vc_ref, o_ref, *, group):
    segments = [(kc_ref[...], vc_ref[...], None)]
    for g in range(group):
        hs = slice(g * HEAD_DIM, (g + 1) * HEAD_DIM)
        o_ref[:, hs] = _attend(q_ref[:, hs], segments).astype(o_ref.dtype)


def context_attention(q, nb, nq, kv_heads, k_ctx, v_ctx, m_ctx):
    group = q.shape[1] // HEAD_DIM // kv_heads
    gd = group * HEAD_DIM
    ctx = pl.BlockSpec((m_ctx, HEAD_DIM), lambda b, h: (b, h))
    blocks = 2 * _nbytes((nq, gd), BF16) + 2 * _nbytes((m_ctx, HEAD_DIM), BF16)
    return pl.pallas_call(
        functools.partial(_ctx_attn_kernel, group=group),
        grid=(nb, kv_heads),
        in_specs=[pl.BlockSpec((nq, gd), lambda b, h: (b, h)), ctx, ctx],
        out_specs=pl.BlockSpec((nq, gd), lambda b, h: (b, h)),
        out_shape=jax.ShapeDtypeStruct(q.shape, BF16),
        compiler_params=_params(("parallel", "parallel"), blocks, 6 * _nbytes((nq, m_ctx), F32)),
        name="context_attention",
    )(q, k_ctx, v_ctx)


def _dense_attn_kernel(*refs, group, n_casts):
    q_ref, kl_ref, vtl_ref, kc_ref, vtc_ref = refs[:5]
    cast_src = refs[5:5 + n_casts]
    o_ref = refs[5 + n_casts]
    cast_dst = refs[6 + n_casts:6 + 2 * n_casts]
    s_ref = refs[6 + 2 * n_casts]
    dn = (((1,), (1,)), ((), ()))
    k = jnp.concatenate([kl_ref[...], kc_ref[...]], axis=0)
    vt = jnp.concatenate([vtl_ref[...], vtc_ref[...]], axis=1)
    m = []
    for g in range(group):
        s = lax.dot_general(k, q_ref[:, g * HEAD_DIM:(g + 1) * HEAD_DIM], dn, preferred_element_type=F32)
        s_ref[g] = s
        m.append(jnp.max(s, axis=0, keepdims=True))
    for g in range(group):
        p = jnp.exp2(s_ref[g] - m[g])
        l = jnp.sum(p, axis=0, keepdims=True)
        o = jnp.dot(vt, p.astype(BF16), preferred_element_type=F32)
        o_ref[:, g * HEAD_DIM:(g + 1) * HEAD_DIM] = (o / l).T.astype(o_ref.dtype)
    for src, dst in zip(cast_src, cast_dst):
        dst[...] = src[...].astype(dst.dtype)


def dense_attention(q, nb, nq, kv_heads, k_ctx, v_ctx, m_ctx, k_lat, v_lat, casts=(), *, tq=512):
    heads = q.shape[1] // HEAD_DIM
    group = heads // kv_heads
    nqt = nq // tq
    gd = group * HEAD_DIM
    keys = nq + m_ctx
    qmap = lambda b, h, i: (b * nqt + i, h)
    lat = pl.BlockSpec((nq, HEAD_DIM), lambda b, h, i: (b, h))
    ctx = pl.BlockSpec((m_ctx, HEAD_DIM), lambda b, h, i: (b, h))
    lat_t = pl.BlockSpec((HEAD_DIM, nq), lambda b, h, i: (b * kv_heads + h, 0))
    ctx_t = pl.BlockSpec((HEAD_DIM, m_ctx), lambda b, h, i: (b * kv_heads + h, 0))
    blocks = 2 * _nbytes((tq, gd), BF16) + 2 * _nbytes((keys, HEAD_DIM), BF16)
    steps = nb * kv_heads * nqt
    cast_specs_in, cast_specs_out, cast_shapes = [], [], []
    for w, layer in casts:
        _, rows_w, cols_w = w.shape
        col_blocks = next(c for c in (1, 2, 4, 8) if steps % c == 0 and rows_w % (steps // c) == 0
                          and (rows_w // (steps // c)) % BF16_SUBLANES == 0 and cols_w % (c * LANES) == 0)
        slab = (rows_w // (steps // col_blocks), cols_w // col_blocks)

        def slab_index(b, h, i, col_blocks=col_blocks):
            step = (b * kv_heads + h) * nqt + i
            return step // col_blocks, step % col_blocks

        cast_specs_in.append(pl.BlockSpec((None,) + slab, lambda b, h, i, f=slab_index, layer=layer: (layer,) + f(b, h, i)))
        cast_specs_out.append(pl.BlockSpec((None,) + slab, lambda b, h, i, f=slab_index: (0,) + f(b, h, i)))
        cast_shapes.append(jax.ShapeDtypeStruct((1, rows_w, cols_w), BF16))
        blocks += _nbytes(slab, F32) + _nbytes(slab, BF16)
    outs = pl.pallas_call(
        functools.partial(_dense_attn_kernel, group=group, n_casts=len(casts)),
        grid=(nb, kv_heads, nqt),
        in_specs=[pl.BlockSpec((tq, gd), qmap), lat, lat_t, ctx, ctx_t] + cast_specs_in,
        out_specs=[pl.BlockSpec((tq, gd), qmap)] + cast_specs_out,
        out_shape=[jax.ShapeDtypeStruct(q.shape, BF16)] + cast_shapes,
        scratch_shapes=[pltpu.VMEM((group, keys, tq), F32)],
        compiler_params=_params(("parallel", "parallel", "parallel"), blocks, 2 * _nbytes((group, keys, tq), F32)),
        name="dense_attention",
    )(q, k_lat, v_lat, k_ctx, v_ctx, *[w for w, _ in casts])
    return outs[0], list(outs[1:])


def _na_plan(rows, rq):
    wr = min(NA_WIN_R, rows)
    band = min(rq + wr - 1, rows)
    nblk = rows // rq
    starts = [int(np.clip(rq * j - wr // 2, 0, rows - band)) for j in range(nblk)]
    deltas = [starts[j] - rq * j for j in range(nblk)]
    uniq = sorted(set(deltas))
    var = [uniq.index(dl) for dl in deltas]
    reps = [deltas.index(dl) for dl in uniq]
    return starts, var, band, reps


def _na_bias_kernel(rpb_ref, o_ref, *, rows, rq, band, starts, reps):
    h = pl.program_id(0)
    wr = min(NA_WIN_R, rows)
    qc = lax.broadcasted_iota(jnp.int32, (GRID_W, GRID_W), 0)
    kc = lax.broadcasted_iota(jnp.int32, (GRID_W, GRID_W), 1)
    dc = kc - qc + (NA_WIN_C - 1)
    cs = jnp.clip(qc - NA_WIN_C // 2, 0, GRID_W - NA_WIN_C)
    col_ok = (kc >= cs) & (kc < cs + NA_WIN_C)
    masked = jnp.full((GRID_W, GRID_W), MASK_VALUE, F32)
    by_dr = []
    for dr in range(2 * NA_WIN_R - 1):
        t = masked
        for d in range(2 * NA_WIN_C - 1):
            t = jnp.where(dc == d, rpb_ref[h, dr, d] * LOG2E, t)
        by_dr.append(jnp.where(col_ok, t, MASK_VALUE))
    for v, j in enumerate(reps):
        for e in range(rq):
            qr = rq * j + e
            rs = int(np.clip(qr - wr // 2, 0, rows - wr))
            for a in range(band):
                kr = starts[j] + a
                blk = by_dr[kr - qr + NA_WIN_R - 1] if rs <= kr < rs + wr else masked
                o_ref[v, e * GRID_W:(e + 1) * GRID_W, a * GRID_W:(a + 1) * GRID_W] = blk


def _na_bias(rpb, rows, rq, band, starts, reps):
    nv = len(reps)
    tq, bk = rq * GRID_W, band * GRID_W
    return pl.pallas_call(
        functools.partial(_na_bias_kernel, rows=rows, rq=rq, band=band, starts=starts, reps=reps),
        grid=(NA_HEADS,),
        in_specs=[pl.BlockSpec(memory_space=pltpu.SMEM)],
        out_specs=pl.BlockSpec((nv, None, tq, bk), lambda h: (0, h, 0, 0)),
        out_shape=jax.ShapeDtypeStruct((nv, NA_HEADS, tq, bk), F32),
        compiler_params=_params(("parallel",), _nbytes((nv, tq, bk), F32)),
        name="na_bias",
    )(rpb)


def _na_kernel(start_ref, var_ref, q_ref, k_ref, v_ref, kc_ref, vc_ref, bias_ref, o_ref, s_ref, *, band_keys):
    del var_ref
    j = pl.program_id(1)
    st = pl.multiple_of(start_ref[j] * GRID_W, GRID_W)
    n_ctx = kc_ref.shape[0]
    dn = (((1,), (1,)), ((), ()))
    m = []
    for h in range(NA_HEADS):
        hs = slice(h * HEAD_DIM, (h + 1) * HEAD_DIM)
        q = q_ref[:, hs]
        s_ctx = lax.dot_general(q, kc_ref[:, hs], dn, preferred_element_type=F32)
        s_loc = lax.dot_general(q, k_ref[pl.ds(st, band_keys), hs], dn, preferred_element_type=F32) + bias_ref[h]
        s_ref[h, :, :n_ctx] = s_ctx
        s_ref[h, :, n_ctx:] = s_loc
        m.append(jnp.maximum(jnp.max(s_ctx, axis=-1, keepdims=True), jnp.max(s_loc, axis=-1, keepdims=True)))
    for h in range(NA_HEADS):
        hs = slice(h * HEAD_DIM, (h + 1) * HEAD_DIM)
        p = jnp.exp2(s_ref[h] - m[h])
        l = jnp.sum(p, axis=-1, keepdims=True)
        p = p.astype(BF16)
        o = (jnp.dot(p[:, :n_ctx], vc_ref[:, hs], preferred_element_type=F32)
             + jnp.dot(p[:, n_ctx:], v_ref[pl.ds(st, band_keys), hs], preferred_element_type=F32))
        o_ref[:, hs] = (o / l).astype(o_ref.dtype)


def neighbourhood_attention(q, k, v, k_ctx, v_ctx, nb, n, m_ctx, rpb, *, rq=4):
    rows = n // GRID_W
    rq = min(rq, rows)
    starts, var, band, reps = _na_plan(rows, rq)
    bias = _na_bias(rpb, rows, rq, band, starts, reps)
    tq, band_keys = rq * GRID_W, band * GRID_W
    nblk = rows // rq
    c = q.shape[1]
    blocks = (_nbytes((tq, c), BF16) * 2 + 2 * _nbytes((n, c), BF16) + 2 * _nbytes((m_ctx, c), BF16)
              + _nbytes((NA_HEADS, tq, band_keys), F32))
    grid_spec = pltpu.PrefetchScalarGridSpec(
        num_scalar_prefetch=2,
        grid=(nb, nblk),
        in_specs=[pl.BlockSpec((tq, c), lambda b, j, st, vr: (b * nblk + j, 0)),
                  pl.BlockSpec((n, c), lambda b, j, st, vr: (b, 0)),
                  pl.BlockSpec((n, c), lambda b, j, st, vr: (b, 0)),
                  pl.BlockSpec((m_ctx, c), lambda b, j, st, vr: (b, 0)),
                  pl.BlockSpec((m_ctx, c), lambda b, j, st, vr: (b, 0)),
                  pl.BlockSpec((None, NA_HEADS, tq, band_keys), lambda b, j, st, vr: (vr[j], 0, 0, 0))],
        out_specs=pl.BlockSpec((tq, c), lambda b, j, st, vr: (b * nblk + j, 0)),
        scratch_shapes=[pltpu.VMEM((NA_HEADS, tq, m_ctx + band_keys), F32)],
    )
    return pl.pallas_call(
        functools.partial(_na_kernel, band_keys=band_keys),
        grid_spec=grid_spec,
        out_shape=jax.ShapeDtypeStruct(q.shape, BF16),
        compiler_params=_params(("parallel", "arbitrary"), blocks, 6 * _nbytes((tq, band_keys + m_ctx), F32)),
        name="neighbourhood_attention",
    )(jnp.asarray(starts, jnp.int32), jnp.asarray(var, jnp.int32), q, k, v, k_ctx, v_ctx, bias)


def _outproj_kernel(x_ref, gt_ref, y0_ref, y1_ref, y2_ref, y3_ref, w_ref, o_ref):
    y = jnp.concatenate([y0_ref[...], y1_ref[...], y2_ref[...], y3_ref[...]], axis=-1)
    o_ref[...] = x_ref[...] + gt_ref[...] * jnp.dot(y, w_ref[...], preferred_element_type=F32)


def out_projection(x, mod3, k_gate, row_of_tile, ys, layer, w_out, *, tm=512):
    r, d = x.shape
    tm = min(tm, r)
    row = lambda i: (i, 0)
    blocks = 2 * _nbytes((tm, d), F32) + _nbytes((tm, d), BF16) + _nbytes(w_out.shape[1:], BF16) // 2
    return pl.pallas_call(
        _outproj_kernel,
        grid=(r // tm,),
        in_specs=[pl.BlockSpec((tm, d), row), _mod_spec(d, row_of_tile, k_gate)]
                 + [pl.BlockSpec((tm, MIX_GROUP), row)] * 4
                 + [pl.BlockSpec((None,) + w_out.shape[1:], lambda i: (layer, 0, 0), pipeline_mode=pl.Buffered(1))],
        out_specs=pl.BlockSpec((tm, d), row),
        out_shape=jax.ShapeDtypeStruct((r, d), F32),
        compiler_params=_params(("parallel",), blocks, 3 * _nbytes((tm, d), F32)),
        name="out_projection",
    )(x, mod3, *ys, w_out)


def _rope_tables(n):
    t = jnp.arange(n, dtype=jnp.int32)
    row = (t // GRID_W).astype(F32)
    col = (t % GRID_W).astype(F32)
    n_freq = HEAD_DIM // 4
    inv = 1.0 / (ROPE_THETA ** (jnp.arange(n_freq, dtype=F32) / n_freq))
    ang = jnp.concatenate([row[:, None] * inv, col[:, None] * inv], axis=-1)
    cos = jnp.repeat(jnp.cos(ang), 2, axis=-1)
    sin = jnp.repeat(jnp.sin(ang), 2, axis=-1)
    sign = jnp.tile(jnp.asarray([-1.0, 1.0], F32), HEAD_DIM // 2)
    return cos, sin * sign


def kernel(x, c, ctx, c_ctx, w_mod, b_mod, ffn1_norm, ffn1_gate, ffn1_up, ffn1_down, mix_norm, w_in, w_out,
           pool_w, pool_scale, q_norm, k_norm, fnet_w, na_rpb, ffn2_norm, ffn2_gate, ffn2_up, ffn2_down, final_norm):
    nb, n, d = x.shape
    m = ctx.shape[1]
    depth = w_mod.shape[0]
    assert nb + 1 <= MOD_ROWS

    cvec = jnp.zeros((MOD_ROWS, d), F32).at[:nb].set(c).at[nb].set(c_ctx)
    mod = modulation(cvec, w_mod, b_mod)

    rope = _rope_tables(n)
    seq_tables = _dft_tables(n, BF16, rows=n // 2)
    ctx_tables = _dft_tables(m, BF16, rows=m // 2)
    chan_tables = _dft_tables(MIX_GROUP // FNET_GROUPS, F32)

    f1 = tuple(cast_layer(w, 0) for w in (ffn1_gate, ffn1_up, ffn1_down))
    w_in_b = cast_layer(w_in, 0)
    pool_w_b = pool_w.astype(BF16)
    xl = x.reshape(nb * n, d)
    xc = ctx.reshape(nb * m, d)
    ctx_row = lambda i: nb

    def lat_row(tm):
        return lambda i: (i * tm) // n

    for l in range(depth):
        last = l == depth - 1
        mod3 = mod[l].reshape(MOD_ROWS, 1, N_MOD * d)

        xl = ffn(xl, mod3, 0, lat_row(FFN_TM), ffn1_norm[l], 0, *f1, tm=FFN_TM)
        xc = ffn(xc, mod3, 0, ctx_row, ffn1_norm[l], 0, *f1, tm=FFN_TM)

        px, gq, gk, _, fx, nq, nk, nv, gvt = in_projection(
            xl, n, mod3, 3, lat_row(512), mix_norm[l], 0, w_in_b, q_norm[l], k_norm[l], rope)
        pc, gqc, gkc, gvc, fc, nqc, nkc, nvc, gvct = in_projection(
            xc, m, mod3, 3, ctx_row, mix_norm[l], 0, w_in_b, q_norm[l], k_norm[l])

        casts = [(w, l) for w in (w_out, ffn2_gate, ffn2_up, ffn2_down)]
        if not last:
            casts += [(w, l + 1) for w in (ffn1_gate, ffn1_up, ffn1_down, w_in)]
        y_gqa, cast_w = dense_attention(gq, nb, n, GQA_KV_HEADS, gkc, gvct, m, gk, gvt, casts)
        w_out_b, f2 = cast_w[0], cast_w[1:4]
        if not last:
            f1, w_in_b = cast_w[4:7], cast_w[7]

        ys = (pool_mix(px, n, pool_w_b[l], pool_scale[l]),
              y_gqa,
              fourier_mix(fx, nb, n, fnet_w[l], seq_tables, chan_tables),
              neighbourhood_attention(nq, nk, nv, nkc, nvc, nb, n, m, na_rpb[l]))
        xl = out_projection(xl, mod3, 5, lat_row(512), ys, 0, w_out_b)

        if not last:
            ycs = (pool_mix(pc, m, pool_w_b[l], pool_scale[l]),
                   context_attention(gqc, nb, m, GQA_KV_HEADS, gkc, gvc, m),
                   fourier_mix(fc, nb, m, fnet_w[l], ctx_tables, chan_tables),
                   context_attention(nqc, nb, m, NA_HEADS, nkc, nvc, m))
            xc = out_projection(xc, mod3, 5, ctx_row, ycs, 0, w_out_b)
            xc = ffn(xc, mod3, 6, ctx_row, ffn2_norm[l], 0, *f2, tm=FFN_TM)

        xl = ffn(xl, mod3, 6, lat_row(FFN_TM), ffn2_norm[l], 0, *f2, final_g=final_norm if last else None, tm=FFN_TM)
    return xl.reshape(nb, n, d)
```

```python
import functools

import numpy as np
import jax
import jax.numpy as jnp
from jax import lax
from jax.experimental import pallas as pl
from jax.experimental.pallas import tpu as pltpu

F32 = jnp.float32
BF16 = jnp.bfloat16

GRID_W = 64
HEAD_DIM = 128
POOL_WINDOWS = (2, 4, 8, 16)
POOL_HALO = 8
GQA_Q_HEADS = 4
GQA_KV_HEADS = 2
FNET_GROUPS = 4
NA_HEADS = 4
NA_WIN_R = 8
NA_WIN_C = 16
ROPE_THETA = 10000.0
IN_SPLITS = (512, 512, 256, 256, 512, 512, 512, 512)
MIX_GROUP = 512
EPS = 1e-6
N_MOD = 9
MOD_ROWS = 8
MASK_VALUE = -1e30
LOG2E = float(np.log2(np.e))

V7X_VMEM_BYTES = 64 * 1024 * 1024
VMEM_CAP = V7X_VMEM_BYTES - 4 * 1024 * 1024
FFN_TM = 1024
BF16_SUBLANES = 16
LANES = 128


def _params(semantics, block_bytes, scratch_bytes=0):
    need = 2 * block_bytes + scratch_bytes
    limit = min(max(need + need // 2, 32 * 1024 * 1024), VMEM_CAP)
    return pltpu.CompilerParams(dimension_semantics=semantics, vmem_limit_bytes=limit)


def _nbytes(shape, dtype):
    return int(np.prod(shape)) * jnp.dtype(dtype).itemsize


def _norm_modulate(x, gain, shift, scale):
    inv = lax.rsqrt(jnp.mean(x * x, axis=-1, keepdims=True) + EPS)
    return (x * inv) * (gain * (1.0 + scale)) + shift


def _mod_spec(d, row_of_tile, k):
    return pl.BlockSpec((None, 1, d), lambda i, *_: (row_of_tile(i), 0, k))


def _mod_kernel(c_ref, w_ref, b_ref, o_ref):
    c = c_ref[...]
    s = c * jax.nn.sigmoid(c)
    o_ref[...] = jnp.dot(s.astype(BF16), w_ref[...].astype(BF16), preferred_element_type=F32) + b_ref[...]


def modulation(cvec, w_mod, b_mod, *, tn=1024):
    nl, d, nd = w_mod.shape
    blocks = _nbytes((d, tn), F32) + _nbytes((MOD_ROWS, d), F32) + 2 * _nbytes((MOD_ROWS, tn), F32)
    return pl.pallas_call(
        _mod_kernel,
        grid=(nl, nd // tn),
        in_specs=[pl.BlockSpec((MOD_ROWS, d), lambda l, j: (0, 0)),
                  pl.BlockSpec((None, d, tn), lambda l, j: (l, 0, j)),
                  pl.BlockSpec((None, 1, tn), lambda l, j: (l, 0, j))],
        out_specs=pl.BlockSpec((None, MOD_ROWS, tn), lambda l, j: (l, 0, j)),
        out_shape=jax.ShapeDtypeStruct((nl, MOD_ROWS, nd), F32),
        compiler_params=_params(("parallel", "parallel"), blocks, _nbytes((d, tn), BF16)),
        name="modulation",
    )(cvec, w_mod, b_mod.reshape(nl, 1, nd))


def _cast_kernel(src_ref, dst_ref):
    dst_ref[...] = src_ref[...].astype(dst_ref.dtype)


def cast_layer(w, layer, *, rows=256):
    _, a, b = w.shape
    rows = min(rows, a)
    return pl.pallas_call(
        _cast_kernel,
        grid=(a // rows,),
        in_specs=[pl.BlockSpec((None, rows, b), lambda i: (layer, i, 0))],
        out_specs=pl.BlockSpec((None, rows, b), lambda i: (0, i, 0)),
        out_shape=jax.ShapeDtypeStruct((1, a, b), BF16),
        compiler_params=_params(("parallel",), _nbytes((rows, b), F32) + _nbytes((rows, b), BF16)),
        name="cast_layer",
    )(w)


def _ffn_kernel(*refs, nj, final):
    if final:
        x_ref, sh_ref, sc_ref, gt_ref, ng_ref, wg_ref, wu_ref, wd_ref, fg_ref, o_ref, h_ref = refs
    else:
        x_ref, sh_ref, sc_ref, gt_ref, ng_ref, wg_ref, wu_ref, wd_ref, o_ref, h_ref = refs
    j = pl.program_id(1)

    def partial_out():
        h = h_ref[...]
        g = jnp.dot(h, wg_ref[...], preferred_element_type=F32)
        u = jnp.dot(h, wu_ref[...], preferred_element_type=F32)
        a = (g * jax.nn.sigmoid(g) * u).astype(BF16)
        return jnp.dot(a, wd_ref[...], preferred_element_type=F32)

    @pl.when(j == 0)
    def _():
        h_ref[...] = _norm_modulate(x_ref[...], ng_ref[...], sh_ref[...], sc_ref[...]).astype(BF16)
        o_ref[...] = partial_out()

    @pl.when((j > 0) & (j < nj - 1))
    def _():
        o_ref[...] += partial_out()

    @pl.when(j == nj - 1)
    def _():
        r = x_ref[...] + 0.5 * gt_ref[...] * (o_ref[...] + partial_out())
        if final:
            r = r * lax.rsqrt(jnp.mean(r * r, axis=-1, keepdims=True) + EPS) * fg_ref[...]
        o_ref[...] = r


def ffn(x, mod3, k0, row_of_tile, norm_g, layer, wg, wu, wd, final_g=None, *, tm=512, tf=512):
    r, d = x.shape
    f = wg.shape[2]
    tm = min(tm, r)
    nj = f // tf
    assert nj >= 2
    final = final_g is not None
    row = lambda i, j: (i, 0)
    in_specs = [pl.BlockSpec((tm, d), row),
                _mod_spec(d, row_of_tile, k0), _mod_spec(d, row_of_tile, k0 + 1), _mod_spec(d, row_of_tile, k0 + 2),
                pl.BlockSpec((1, d), lambda i, j: (0, 0)),
                pl.BlockSpec((None, d, tf), lambda i, j: (layer, 0, j)),
                pl.BlockSpec((None, d, tf), lambda i, j: (layer, 0, j)),
                pl.BlockSpec((None, tf, d), lambda i, j: (layer, j, 0))]
    args = [x, mod3, mod3, mod3, norm_g.reshape(1, d), wg, wu, wd]
    if final:
        in_specs.append(pl.BlockSpec((1, d), lambda i, j: (0, 0)))
        args.append(final_g.reshape(1, d))
    blocks = 2 * _nbytes((tm, d), F32) + 3 * _nbytes((d, tf), BF16) + 5 * _nbytes((1, d), F32)
    temps = _nbytes((tm, d), BF16) + 4 * _nbytes((tm, tf), F32) + _nbytes((tm, d), F32)
    return pl.pallas_call(
        functools.partial(_ffn_kernel, nj=nj, final=final),
        grid=(r // tm, nj),
        in_specs=in_specs,
        out_specs=pl.BlockSpec((tm, d), row),
        out_shape=jax.ShapeDtypeStruct((r, d), F32),
        scratch_shapes=[pltpu.VMEM((tm, d), BF16)],
        compiler_params=_params(("parallel", "arbitrary"), blocks, temps),
        name="ffn",
    )(*args)


def _rms_heads(z, gain, nheads):
    outs = []
    for hd in range(nheads):
        zh = z[:, hd * HEAD_DIM:(hd + 1) * HEAD_DIM]
        outs.append(zh * lax.rsqrt(jnp.mean(zh * zh, axis=-1, keepdims=True) + EPS) * gain)
    return outs


def _rope(zh, cos, sin_signed):
    lane = lax.broadcasted_iota(jnp.int32, zh.shape, 1)
    partner = jnp.where(lane % 2 == 0, pltpu.roll(zh, HEAD_DIM - 1, 1), pltpu.roll(zh, 1, 1))
    return zh * cos + partner * sin_signed


def _inproj_kernel(*refs, rope):
    if rope:
        (x_ref, sh_ref, sc_ref, ng_ref, w_ref, qn_ref, kn_ref, cos_ref, sin_ref,
         px_ref, gq_ref, gk_ref, gv_ref, fx_ref, nq_ref, nk_ref, nv_ref, gvt_ref) = refs
    else:
        (x_ref, sh_ref, sc_ref, ng_ref, w_ref, qn_ref, kn_ref,
         px_ref, gq_ref, gk_ref, gv_ref, fx_ref, nq_ref, nk_ref, nv_ref, gvt_ref) = refs
    h = _norm_modulate(x_ref[...], ng_ref[...], sh_ref[...], sc_ref[...]).astype(BF16)
    offs = np.cumsum((0,) + IN_SPLITS)

    def proj(s):
        return jnp.dot(h, w_ref[:, offs[s]:offs[s + 1]], preferred_element_type=F32)

    sm_scale = HEAD_DIM ** -0.5 * LOG2E
    px_ref[...] = proj(0)
    q_heads = _rms_heads(proj(1), qn_ref[...], GQA_Q_HEADS)
    k_heads = _rms_heads(proj(2), kn_ref[...], GQA_KV_HEADS)
    if rope:
        cos, sin = cos_ref[...], sin_ref[...]
        q_heads = [_rope(zh, cos, sin) for zh in q_heads]
        k_heads = [_rope(zh, cos, sin) for zh in k_heads]
    gq_ref[...] = jnp.concatenate([zh * sm_scale for zh in q_heads], axis=-1).astype(BF16)
    gk_ref[...] = jnp.concatenate(k_heads, axis=-1).astype(BF16)
    gv = proj(3)
    gv_ref[...] = gv.astype(BF16)
    gvt_ref[...] = gv.T.astype(BF16)
    fx_ref[...] = proj(4).astype(BF16)
    nq_ref[...] = (proj(5) * sm_scale).astype(BF16)
    nk_ref[...] = proj(6).astype(BF16)
    nv_ref[...] = proj(7).astype(BF16)


def in_projection(x, seq_len, mod3, k0, row_of_tile, norm_g, layer, w_in, q_norm, k_norm, rope_tables=None, *, tm=512):
    r, d = x.shape
    tm = min(tm, seq_len)
    tiles_per_seq = seq_len // tm
    rope = rope_tables is not None
    row = lambda i: (i, 0)
    const = lambda i: (0, 0)
    in_specs = [pl.BlockSpec((tm, d), row),
                _mod_spec(d, row_of_tile, k0), _mod_spec(d, row_of_tile, k0 + 1),
                pl.BlockSpec((1, d), const),
                pl.BlockSpec((None,) + w_in.shape[1:], lambda i: (layer, 0, 0), pipeline_mode=pl.Buffered(1)),
                pl.BlockSpec((1, HEAD_DIM), const), pl.BlockSpec((1, HEAD_DIM), const)]
    args = [x, mod3, mod3, norm_g.reshape(1, d), w_in, q_norm.reshape(1, HEAD_DIM), k_norm.reshape(1, HEAD_DIM)]
    if rope:
        tab = pl.BlockSpec((tm, HEAD_DIM), lambda i: (i % tiles_per_seq, 0))
        in_specs += [tab, tab]
        args += list(rope_tables)
    widths = (512, 512, 256, 256, 512, 512, 512, 512)
    dtypes = (F32,) + (BF16,) * 7
    out_specs = [pl.BlockSpec((tm, w), row) for w in widths]
    out_shape = [jax.ShapeDtypeStruct((r, w), dt) for w, dt in zip(widths, dtypes)]
    kv_width = widths[3]
    out_specs.append(pl.BlockSpec((kv_width, tm), lambda i: (i // tiles_per_seq, i % tiles_per_seq)))
    out_shape.append(jax.ShapeDtypeStruct((r // seq_len * kv_width, seq_len), BF16))
    blocks = _nbytes((tm, d), F32) + _nbytes(w_in.shape[1:], BF16) // 2 + _nbytes((tm, 4096), F32)
    temps = _nbytes((tm, d), BF16) + 2 * _nbytes((tm, d), F32) + 6 * _nbytes((tm, 512), F32)
    return pl.pallas_call(
        functools.partial(_inproj_kernel, rope=rope),
        grid=(r // tm,),
        in_specs=in_specs,
        out_specs=out_specs,
        out_shape=out_shape,
        compiler_params=_params(("parallel",), blocks, temps),
        name="in_projection",
    )(*args)


def _pool_kernel(prev_ref, cur_ref, next_ref, w_ref, sc_ref, o_ref, *, n, ts):
    tiles_per_seq = n // ts
    s = pl.program_id(0) % tiles_per_seq
    cur = cur_ref[...]
    prev = jnp.where(s == 0, 0.0, prev_ref[...])
    nxt = jnp.where(s == tiles_per_seq - 1, 0.0, next_ref[...])
    ext = jnp.concatenate([prev, cur, nxt], axis=0)
    rows = ts + 2 * POOL_HALO
    t = s * ts + lax.broadcasted_iota(jnp.int32, (ts, 1), 0)
    gw = cur.shape[1] // len(POOL_WINDOWS)
    for gi, w in enumerate(POOL_WINDOWS):
        e = ext[:, gi * gw:(gi + 1) * gw]
        width = 1
        while width < w:
            e = e + pltpu.roll(e, rows - width, 0)
            width *= 2
        lead = POOL_HALO - w // 2
        win = (pltpu.roll(e, rows - lead, 0) if lead else e)[:ts]
        lo = jnp.maximum(t - w // 2, 0)
        hi = jnp.minimum(t + w // 2 - 1, n - 1)
        diff = win / (hi - lo + 1).astype(F32) - cur[:, gi * gw:(gi + 1) * gw]
        y = jnp.dot(diff.astype(BF16), w_ref[gi], preferred_element_type=F32)
        o_ref[:, gi * gw:(gi + 1) * gw] = (y * sc_ref[:, gi * gw:(gi + 1) * gw]).astype(o_ref.dtype)


def pool_mix(px, n, w_pool, scale, *, ts=512):
    r, c = px.shape
    ts = min(ts, n)
    hb = ts // POOL_HALO
    last_halo = r // POOL_HALO - 1
    blocks = 2 * _nbytes((ts, c), F32) + _nbytes(w_pool.shape, BF16)
    return pl.pallas_call(
        functools.partial(_pool_kernel, n=n, ts=ts),
        grid=(r // ts,),
        in_specs=[pl.BlockSpec((POOL_HALO, c), lambda i: (jnp.maximum(i * hb - 1, 0), 0)),
                  pl.BlockSpec((ts, c), lambda i: (i, 0)),
                  pl.BlockSpec((POOL_HALO, c), lambda i: (jnp.minimum((i + 1) * hb, last_halo), 0)),
                  pl.BlockSpec(w_pool.shape, lambda i: (0, 0, 0)),
                  pl.BlockSpec((1, c), lambda i: (0, 0))],
        out_specs=pl.BlockSpec((ts, c), lambda i: (i, 0)),
        out_shape=jax.ShapeDtypeStruct((r, c), BF16),
        compiler_params=_params(("parallel",), blocks, 6 * _nbytes((ts + 2 * POOL_HALO, c), F32)),
        name="pool_mix",
    )(px, px, px, w_pool, scale.reshape(1, c))


def _dft_chan_kernel(u_ref, c_ref, s_ref, w_ref, ap_ref, aq_ref, alt_ref, pq_ref, *, norm, tiles_per_seq):
    gw = c_ref.shape[0]
    i = pl.program_id(0)

    @pl.when(i == 0)
    def _():
        for g in range(FNET_GROUPS):
            w = w_ref[g]
            p = jnp.dot(c_ref[...], w, preferred_element_type=F32, precision=lax.Precision.HIGHEST)
            q = jnp.dot(s_ref[...], w, preferred_element_type=F32, precision=lax.Precision.HIGHEST)
            pq_ref[g] = (jnp.concatenate([p, -q], axis=-1) * norm).astype(BF16)

    sign = (1 - 2 * (lax.broadcasted_iota(jnp.int32, (u_ref.shape[0], 1), 0) % 2)).astype(F32)
    alt = []
    for g in range(FNET_GROUPS):
        a = jnp.dot(u_ref[:, g * gw:(g + 1) * gw], pq_ref[g], preferred_element_type=F32)
        ap_ref[:, g * gw:(g + 1) * gw] = a[:, :gw].astype(BF16)
        aq_ref[:, g * gw:(g + 1) * gw] = a[:, gw:].astype(BF16)
        alt.append(jnp.sum(a[:, :gw] * sign, axis=0, keepdims=True))

    @pl.when(i % tiles_per_seq == 0)
    def _():
        alt_ref[...] = jnp.zeros_like(alt_ref)

    alt_ref[...] += jnp.concatenate(alt, axis=-1)


def _dft_seq_kernel(c_ref, s_ref, ap_ref, aq_ref, alt_ref, lo_ref, hi_ref, g_ref, h_ref, carry_ref, *, nn):
    i, j = pl.program_id(0), pl.program_id(1)
    nb, tk, _ = g_ref.shape

    @pl.when(j == 0)
    def _():
        g_ref[...] = jnp.zeros_like(g_ref)
        h_ref[...] = jnp.zeros_like(h_ref)

    @pl.when((i == 0) & (j == 0))
    def _():
        carry_ref[...] = alt_ref[...]

    for b in range(nb):
        g_ref[b] += jnp.dot(c_ref[...], ap_ref[b], preferred_element_type=F32)
        h_ref[b] += jnp.dot(s_ref[...], aq_ref[b], preferred_element_type=F32)

    @pl.when(j == nn - 1)
    def _():
        lo_ref[...] = (g_ref[...] + h_ref[...]).astype(lo_ref.dtype)
        row = lax.broadcasted_iota(jnp.int32, (tk, tk), 0)
        col = lax.broadcasted_iota(jnp.int32, (tk, tk), 1)
        mirror = ((row + col) % tk == 0).astype(BF16)
        first = lax.broadcasted_iota(jnp.int32, (tk, 1), 0) == 0
        for b in range(nb):
            d = g_ref[b] - h_ref[b]
            m = jnp.dot(mirror, d.astype(BF16), preferred_element_type=F32)
            hi_ref[b] = jnp.where(first, carry_ref[b], m).astype(hi_ref.dtype)
            carry_ref[b] = d[0:1, :]


def _dft_tables(n, dtype, rows=None, tile=512):
    rows = n if rows is None else rows

    def cos_sin(cols):
        k = jnp.arange(rows, dtype=jnp.int32)[:, None]
        ang = ((k * cols[None, :]) % n).astype(F32) * (2.0 * np.pi / n)
        return jnp.cos(ang), jnp.sin(ang)

    if n <= tile:
        c, s = cos_sin(jnp.arange(n, dtype=jnp.int32))
        return c.astype(dtype), s.astype(dtype)
    c1, s1 = cos_sin(jnp.arange(tile, dtype=jnp.int32))
    c0, s0 = cos_sin(jnp.arange(n // tile, dtype=jnp.int32) * tile)
    c = c0[:, :, None] * c1[:, None, :] - s0[:, :, None] * s1[:, None, :]
    s = s0[:, :, None] * c1[:, None, :] + c0[:, :, None] * s1[:, None, :]
    return c.reshape(rows, n).astype(dtype), s.reshape(rows, n).astype(dtype)


def fourier_mix(fx, nb, n, w_fnet, seq_tables, chan_tables, *, tm=512, tk=1024, tn=512):
    r, c = fx.shape
    gw = c // FNET_GROUPS
    half = n // 2
    tm, tk, tn = min(tm, n), min(tk, half), min(tn, n)
    cc, sc = chan_tables
    tiles_per_seq = n // tm
    ap, aq, alt = pl.pallas_call(
        functools.partial(_dft_chan_kernel, norm=float(1.0 / np.sqrt(n * gw)), tiles_per_seq=tiles_per_seq),
        grid=(r // tm,),
        in_specs=[pl.BlockSpec((tm, c), lambda i: (i, 0)),
                  pl.BlockSpec((gw, gw), lambda i: (0, 0)),
                  pl.BlockSpec((gw, gw), lambda i: (0, 0)),
                  pl.BlockSpec(w_fnet.shape, lambda i: (0, 0, 0))],
        out_specs=[pl.BlockSpec((tm, c), lambda i: (i, 0))] * 2
                  + [pl.BlockSpec((None, 1, c), lambda i: (i // tiles_per_seq, 0, 0))],
        out_shape=[jax.ShapeDtypeStruct((r, c), BF16)] * 2 + [jax.ShapeDtypeStruct((nb, 1, c), F32)],
        scratch_shapes=[pltpu.VMEM((FNET_GROUPS, gw, 2 * gw), BF16)],
        compiler_params=_params(("arbitrary",), 3 * _nbytes((tm, c), BF16) + 6 * _nbytes((gw, gw), F32),
                                4 * _nbytes((tm, c), F32)),
        name="dft_channels",
    )(fx, cc, sc, w_fnet)
    cn, sn = seq_tables
    nn = n // tn
    blocks = 2 * _nbytes((tk, tn), BF16) + 2 * _nbytes((nb, tn, c), BF16) + 2 * _nbytes((nb, tk, c), BF16)
    nt = half // tk
    y_lo, y_hi = pl.pallas_call(
        functools.partial(_dft_seq_kernel, nn=nn),
        grid=(nt, nn),
        in_specs=[pl.BlockSpec((tk, tn), lambda i, j: (nt - 1 - i, j)),
                  pl.BlockSpec((tk, tn), lambda i, j: (nt - 1 - i, j)),
                  pl.BlockSpec((nb, tn, c), lambda i, j: (0, j, 0)),
                  pl.BlockSpec((nb, tn, c), lambda i, j: (0, j, 0)),
                  pl.BlockSpec((nb, 1, c), lambda i, j: (0, 0, 0))],
        out_specs=[pl.BlockSpec((nb, tk, c), lambda i, j: (0, nt - 1 - i, 0)),
                   pl.BlockSpec((nb, tk, c), lambda i, j: (0, i, 0))],
        out_shape=[jax.ShapeDtypeStruct((nb, half, c), BF16)] * 2,
        scratch_shapes=[pltpu.VMEM((nb, tk, c), F32)] * 2 + [pltpu.VMEM((nb, 1, c), F32)],
        compiler_params=_params(("arbitrary", "arbitrary"), blocks, 4 * _nbytes((nb, tk, c), F32)),
        name="dft_positions",
    )(cn, sn, ap.reshape(nb, n, c), aq.reshape(nb, n, c), alt)
    return y_lo, y_hi


def _attend(q, segments):
    dn = (((1,), (1,)), ((), ()))
    scores = []
    for k, _, bias in segments:
        s = lax.dot_general(q, k, dn, preferred_element_type=F32)
        scores.append(s if bias is None else s + bias)
    m = functools.reduce(jnp.maximum, [jnp.max(s, axis=-1, keepdims=True) for s in scores])
    probs = [jnp.exp2(s - m) for s in scores]
    l = sum(jnp.sum(p, axis=-1, keepdims=True) for p in probs)
    o = sum(jnp.dot(p.astype(BF16), v, preferred_element_type=F32) for p, (_, v, _) in zip(probs, segments))
    return o / l


def _ctx_attn_kernel(q_ref, kc_ref, vc_ref, o_ref, *, group):
    segments = [(kc_ref[...], vc_ref[...], None)]
    for g in range(group):
        hs = slice(g * HEAD_DIM, (g + 1) * HEAD_DIM)
        o_ref[:, hs] = _attend(q_ref[:, hs], segments).astype(o_ref.dtype)


def context_attention(q, nb, nq, kv_heads, k_ctx, v_ctx, m_ctx):
    group = q.shape[1] // HEAD_DIM // kv_heads
    gd = group * HEAD_DIM
    ctx = pl.BlockSpec((m_ctx, HEAD_DIM), lambda b, h: (b, h))
    blocks = 2 * _nbytes((nq, gd), BF16) + 2 * _nbytes((m_ctx, HEAD_DIM), BF16)
    return pl.pallas_call(
        functools.partial(_ctx_attn_kernel, group=group),
        grid=(nb, kv_heads),
        in_specs=[pl.BlockSpec((nq, gd), lambda b, h: (b, h)), ctx, ctx],
        out_specs=pl.BlockSpec((nq, gd), lambda b, h: (b, h)),
        out_shape=jax.ShapeDtypeStruct(q.shape, BF16),
        compiler_params=_params(("parallel", "parallel"), blocks, 6 * _nbytes((nq, m_ctx), F32)),
        name="context_attention",
    )(q, k_ctx, v_ctx)


def _dense_attn_kernel(*refs, group, n_casts):
    q_ref, kl_ref, vtl_ref, kc_ref, vtc_ref = refs[:5]
    cast_src = refs[5:5 + n_casts]
    o_ref = refs[5 + n_casts]
    cast_dst = refs[6 + n_casts:6 + 2 * n_casts]
    s_ref = refs[6 + 2 * n_casts]
    dn = (((1,), (1,)), ((), ()))
    k = jnp.concatenate([kl_ref[...], kc_ref[...]], axis=0)
    vt = jnp.concatenate([vtl_ref[...], vtc_ref[...]], axis=1)
    m = []
    for g in range(group):
        s = lax.dot_general(k, q_ref[:, g * HEAD_DIM:(g + 1) * HEAD_DIM], dn, preferred_element_type=F32)
        s_ref[g] = s
        m.append(jnp.max(s, axis=0, keepdims=True))
    for g in range(group):
        p = jnp.exp2(s_ref[g] - m[g])
        l = jnp.sum(p, axis=0, keepdims=True)
        o = jnp.dot(vt, p.astype(BF16), preferred_element_type=F32)
        o_ref[:, g * HEAD_DIM:(g + 1) * HEAD_DIM] = (o / l).T.astype(o_ref.dtype)
    for src, dst in zip(cast_src, cast_dst):
        dst[...] = src[...].astype(dst.dtype)


def dense_attention(q, nb, nq, kv_heads, k_ctx, v_ctx, m_ctx, k_lat, v_lat, casts=(), *, tq=512):
    heads = q.shape[1] // HEAD_DIM
    group = heads // kv_heads
    nqt = nq // tq
    gd = group * HEAD_DIM
    keys = nq + m_ctx
    qmap = lambda b, h, i: (b * nqt + i, h)
    lat = pl.BlockSpec((nq, HEAD_DIM), lambda b, h, i: (b, h))
    ctx = pl.BlockSpec((m_ctx, HEAD_DIM), lambda b, h, i: (b, h))
    lat_t = pl.BlockSpec((HEAD_DIM, nq), lambda b, h, i: (b * kv_heads + h, 0))
    ctx_t = pl.BlockSpec((HEAD_DIM, m_ctx), lambda b, h, i: (b * kv_heads + h, 0))
    blocks = 2 * _nbytes((tq, gd), BF16) + 2 * _nbytes((keys, HEAD_DIM), BF16)
    steps = nb * kv_heads * nqt
    cast_specs_in, cast_specs_out, cast_shapes = [], [], []
    for w, layer in casts:
        _, rows_w, cols_w = w.shape
        col_blocks = next(c for c in (1, 2, 4, 8) if steps % c == 0 and rows_w % (steps // c) == 0
                          and (rows_w // (steps // c)) % BF16_SUBLANES == 0 and cols_w % (c * LANES) == 0)
        slab = (rows_w // (steps // col_blocks), cols_w // col_blocks)

        def slab_index(b, h, i, col_blocks=col_blocks):
            step = (b * kv_heads + h) * nqt + i
            return step // col_blocks, step % col_blocks

        cast_specs_in.append(pl.BlockSpec((None,) + slab, lambda b, h, i, f=slab_index, layer=layer: (layer,) + f(b, h, i)))
        cast_specs_out.append(pl.BlockSpec((None,) + slab, lambda b, h, i, f=slab_index: (0,) + f(b, h, i)))
        cast_shapes.append(jax.ShapeDtypeStruct((1, rows_w, cols_w), BF16))
        blocks += _nbytes(slab, F32) + _nbytes(slab, BF16)
    outs = pl.pallas_call(
        functools.partial(_dense_attn_kernel, group=group, n_casts=len(casts)),
        grid=(nb, kv_heads, nqt),
        in_specs=[pl.BlockSpec((tq, gd), qmap), lat, lat_t, ctx, ctx_t] + cast_specs_in,
        out_specs=[pl.BlockSpec((tq, gd), qmap)] + cast_specs_out,
        out_shape=[jax.ShapeDtypeStruct(q.shape, BF16)] + cast_shapes,
        scratch_shapes=[pltpu.VMEM((group, keys, tq), F32)],
        compiler_params=_params(("parallel", "parallel", "parallel"), blocks, 2 * _nbytes((group, keys, tq), F32)),
        name="dense_attention",
    )(q, k_lat, v_lat, k_ctx, v_ctx, *[w for w, _ in casts])
    return outs[0], list(outs[1:])


def _na_plan(rows, rq):
    wr = min(NA_WIN_R, rows)
    band = min(rq + wr - 1, rows)
    nblk = rows // rq
    starts = [int(np.clip(rq * j - wr // 2, 0, rows - band)) for j in range(nblk)]
    deltas = [starts[j] - rq * j for j in range(nblk)]
    uniq = sorted(set(deltas))
    var = [uniq.index(dl) for dl in deltas]
    reps = [deltas.index(dl) for dl in uniq]
    return starts, var, band, reps


def _na_bias_kernel(rpb_ref, o_ref, *, rows, rq, band, starts, reps):
    h = pl.program_id(0)
    wr = min(NA_WIN_R, rows)
    qc = lax.broadcasted_iota(jnp.int32, (GRID_W, GRID_W), 0)
    kc = lax.broadcasted_iota(jnp.int32, (GRID_W, GRID_W), 1)
    dc = kc - qc + (NA_WIN_C - 1)
    cs = jnp.clip(qc - NA_WIN_C // 2, 0, GRID_W - NA_WIN_C)
    col_ok = (kc >= cs) & (kc < cs + NA_WIN_C)
    masked = jnp.full((GRID_W, GRID_W), MASK_VALUE, F32)
    by_dr = []
    for dr in range(2 * NA_WIN_R - 1):
        t = masked
        for d in range(2 * NA_WIN_C - 1):
            t = jnp.where(dc == d, rpb_ref[h, dr, d] * LOG2E, t)
        by_dr.append(jnp.where(col_ok, t, MASK_VALUE))
    for v, j in enumerate(reps):
        for e in range(rq):
            qr = rq * j + e
            rs = int(np.clip(qr - wr // 2, 0, rows - wr))
            for a in range(band):
                kr = starts[j] + a
                blk = by_dr[kr - qr + NA_WIN_R - 1] if rs <= kr < rs + wr else masked
                o_ref[v, e * GRID_W:(e + 1) * GRID_W, a * GRID_W:(a + 1) * GRID_W] = blk


def _na_bias(rpb, rows, rq, band, starts, reps):
    nv = len(reps)
    tq, bk = rq * GRID_W, band * GRID_W
    return pl.pallas_call(
        functools.partial(_na_bias_kernel, rows=rows, rq=rq, band=band, starts=starts, reps=reps),
        grid=(NA_HEADS,),
        in_specs=[pl.BlockSpec(memory_space=pltpu.SMEM)],
        out_specs=pl.BlockSpec((nv, None, tq, bk), lambda h: (0, h, 0, 0)),
        out_shape=jax.ShapeDtypeStruct((nv, NA_HEADS, tq, bk), F32),
        compiler_params=_params(("parallel",), _nbytes((nv, tq, bk), F32)),
        name="na_bias",
    )(rpb)


def _na_kernel(start_ref, var_ref, q_ref, k_ref, v_ref, kc_ref, vc_ref, bias_ref, o_ref, s_ref, *, band_keys):
    del var_ref
    j = pl.program_id(1)
    st = pl.multiple_of(start_ref[j] * GRID_W, GRID_W)
    n_ctx = kc_ref.shape[0]
    dn = (((1,), (1,)), ((), ()))
    m = []
    for h in range(NA_HEADS):
        hs = slice(h * HEAD_DIM, (h + 1) * HEAD_DIM)
        q = q_ref[:, hs]
        s_ctx = lax.dot_general(q, kc_ref[:, hs], dn, preferred_element_type=F32)
        s_loc = lax.dot_general(q, k_ref[pl.ds(st, band_keys), hs], dn, preferred_element_type=F32) + bias_ref[h]
        s_ref[h, :, :n_ctx] = s_ctx
        s_ref[h, :, n_ctx:] = s_loc
        m.append(jnp.maximum(jnp.max(s_ctx, axis=-1, keepdims=True), jnp.max(s_loc, axis=-1, keepdims=True)))
    for h in range(NA_HEADS):
        hs = slice(h * HEAD_DIM, (h + 1) * HEAD_DIM)
        p = jnp.exp2(s_ref[h] - m[h])
        l = jnp.sum(p, axis=-1, keepdims=True)
        p = p.astype(BF16)
        o = (jnp.dot(p[:, :n_ctx], vc_ref[:, hs], preferred_element_type=F32)
             + jnp.dot(p[:, n_ctx:], v_ref[pl.ds(st, band_keys), hs], preferred_element_type=F32))
        o_ref[:, hs] = (o / l).astype(o_ref.dtype)


def neighbourhood_attention(q, k, v, k_ctx, v_ctx, nb, n, m_ctx, rpb, *, rq=4):
    rows = n // GRID_W
    rq = min(rq, rows)
    starts, var, band, reps = _na_plan(rows, rq)
    bias = _na_bias(rpb, rows, rq, band, starts, reps)
    tq, band_keys = rq * GRID_W, band * GRID_W
    nblk = rows // rq
    c = q.shape[1]
    blocks = (_nbytes((tq, c), BF16) * 2 + 2 * _nbytes((n, c), BF16) + 2 * _nbytes((m_ctx, c), BF16)
              + _nbytes((NA_HEADS, tq, band_keys), F32))
    grid_spec = pltpu.PrefetchScalarGridSpec(
        num_scalar_prefetch=2,
        grid=(nb, nblk),
        in_specs=[pl.BlockSpec((tq, c), lambda b, j, st, vr: (b * nblk + j, 0)),
                  pl.BlockSpec((n, c), lambda b, j, st, vr: (b, 0)),
                  pl.BlockSpec((n, c), lambda b, j, st, vr: (b, 0)),
                  pl.BlockSpec((m_ctx, c), lambda b, j, st, vr: (b, 0)),
                  pl.BlockSpec((m_ctx, c), lambda b, j, st, vr: (b, 0)),
                  pl.BlockSpec((None, NA_HEADS, tq, band_keys), lambda b, j, st, vr: (vr[j], 0, 0, 0))],
        out_specs=pl.BlockSpec((tq, c), lambda b, j, st, vr: (b * nblk + j, 0)),
        scratch_shapes=[pltpu.VMEM((NA_HEADS, tq, m_ctx + band_keys), F32)],
    )
    return pl.pallas_call(
        functools.partial(_na_kernel, band_keys=band_keys),
        grid_spec=grid_spec,
        out_shape=jax.ShapeDtypeStruct(q.shape, BF16),
        compiler_params=_params(("parallel", "arbitrary"), blocks, 6 * _nbytes((tq, band_keys + m_ctx), F32)),
        name="neighbourhood_attention",
    )(jnp.asarray(starts, jnp.int32), jnp.asarray(var, jnp.int32), q, k, v, k_ctx, v_ctx, bias)


def _outproj_kernel(*refs, tiles_half):
    if tiles_half:
        x_ref, gt_ref, y0_ref, y1_ref, lo_ref, hi_ref, y3_ref, w_ref, o_ref = refs
        in_low_half = pl.program_id(0) % (2 * tiles_half) < tiles_half
        y2 = jnp.where(in_low_half, lo_ref[...], hi_ref[...])
    else:
        x_ref, gt_ref, y0_ref, y1_ref, y2_ref, y3_ref, w_ref, o_ref = refs
        y2 = y2_ref[...]
    y = jnp.concatenate([y0_ref[...], y1_ref[...], y2, y3_ref[...]], axis=-1)
    o_ref[...] = x_ref[...] + gt_ref[...] * jnp.dot(y, w_ref[...], preferred_element_type=F32)


def out_projection(x, mod3, k_gate, row_of_tile, ys, layer, w_out, *, tm=512):
    r, d = x.shape
    tm = min(tm, r)
    row = lambda i: (i, 0)
    mix = pl.BlockSpec((tm, MIX_GROUP), row)
    y_lo, y_hi = ys[2]
    nb, half, c = y_lo.shape
    if half % tm == 0:
        tiles_half = half // tm
        seq_tile = lambda i: i % (2 * tiles_half)
        lo = pl.BlockSpec((None, tm, c), lambda i: (i // (2 * tiles_half), jnp.minimum(seq_tile(i), tiles_half - 1), 0))
        hi = pl.BlockSpec((None, tm, c), lambda i: (i // (2 * tiles_half), jnp.maximum(seq_tile(i) - tiles_half, 0), 0))
        mix_specs, mix_args = [mix, mix, lo, hi, mix], [ys[0], ys[1], y_lo, y_hi, ys[3]]
    else:
        tiles_half = 0
        y2 = jnp.concatenate([y_lo, y_hi], axis=1).reshape(r, c)
        mix_specs, mix_args = [mix] * 4, [ys[0], ys[1], y2, ys[3]]
    blocks = 2 * _nbytes((tm, d), F32) + 5 * _nbytes((tm, MIX_GROUP), BF16) + _nbytes(w_out.shape[1:], BF16) // 2
    return pl.pallas_call(
        functools.partial(_outproj_kernel, tiles_half=tiles_half),
        grid=(r // tm,),
        in_specs=[pl.BlockSpec((tm, d), row), _mod_spec(d, row_of_tile, k_gate)] + mix_specs
                 + [pl.BlockSpec((None,) + w_out.shape[1:], lambda i: (layer, 0, 0), pipeline_mode=pl.Buffered(1))],
        out_specs=pl.BlockSpec((tm, d), row),
        out_shape=jax.ShapeDtypeStruct((r, d), F32),
        compiler_params=_params(("parallel",), blocks, 3 * _nbytes((tm, d), F32)),
        name="out_projection",
    )(x, mod3, *mix_args, w_out)


def _rope_tables(n):
    t = jnp.arange(n, dtype=jnp.int32)
    row = (t // GRID_W).astype(F32)
    col = (t % GRID_W).astype(F32)
    n_freq = HEAD_DIM // 4
    inv = 1.0 / (ROPE_THETA ** (jnp.arange(n_freq, dtype=F32) / n_freq))
    ang = jnp.concatenate([row[:, None] * inv, col[:, None] * inv], axis=-1)
    cos = jnp.repeat(jnp.cos(ang), 2, axis=-1)
    sin = jnp.repeat(jnp.sin(ang), 2, axis=-1)
    sign = jnp.tile(jnp.asarray([-1.0, 1.0], F32), HEAD_DIM // 2)
    return cos, sin * sign


def kernel(x, c, ctx, c_ctx, w_mod, b_mod, ffn1_norm, ffn1_gate, ffn1_up, ffn1_down, mix_norm, w_in, w_out,
           pool_w, pool_scale, q_norm, k_norm, fnet_w, na_rpb, ffn2_norm, ffn2_gate, ffn2_up, ffn2_down, final_norm):
    nb, n, d = x.shape
    m = ctx.shape[1]
    depth = w_mod.shape[0]
    assert nb + 1 <= MOD_ROWS

    cvec = jnp.zeros((MOD_ROWS, d), F32).at[:nb].set(c).at[nb].set(c_ctx)
    mod = modulation(cvec, w_mod, b_mod)

    rope = _rope_tables(n)
    seq_tables = _dft_tables(n, BF16, rows=n // 2)
    ctx_tables = _dft_tables(m, BF16, rows=m // 2)
    chan_tables = _dft_tables(MIX_GROUP // FNET_GROUPS, F32)

    f1 = tuple(cast_layer(w, 0) for w in (ffn1_gate, ffn1_up, ffn1_down))
    w_in_b = cast_layer(w_in, 0)
    pool_w_b = pool_w.astype(BF16)
    xl = x.reshape(nb * n, d)
    xc = ctx.reshape(nb * m, d)
    ctx_row = lambda i: nb

    def lat_row(tm):
        return lambda i: (i * tm) // n

    for l in range(depth):
        last = l == depth - 1
        mod3 = mod[l].reshape(MOD_ROWS, 1, N_MOD * d)

        xl = ffn(xl, mod3, 0, lat_row(FFN_TM), ffn1_norm[l], 0, *f1, tm=FFN_TM)
        xc = ffn(xc, mod3, 0, ctx_row, ffn1_norm[l], 0, *f1, tm=FFN_TM)

        px, gq, gk, _, fx, nq, nk, nv, gvt = in_projection(
            xl, n, mod3, 3, lat_row(512), mix_norm[l], 0, w_in_b, q_norm[l], k_norm[l], rope)
        pc, gqc, gkc, gvc, fc, nqc, nkc, nvc, gvct = in_projection(
            xc, m, mod3, 3, ctx_row, mix_norm[l], 0, w_in_b, q_norm[l], k_norm[l])

        casts = [(w, l) for w in (w_out, ffn2_gate, ffn2_up, ffn2_down)]
        if not last:
            casts += [(w, l + 1) for w in (ffn1_gate, ffn1_up, ffn1_down, w_in)]
        y_gqa, cast_w = dense_attention(gq, nb, n, GQA_KV_HEADS, gkc, gvct, m, gk, gvt, casts)
        w_out_b, f2 = cast_w[0], cast_w[1:4]
        if not last:
            f1, w_in_b = cast_w[4:7], cast_w[7]

        ys = (pool_mix(px, n, pool_w_b[l], pool_scale[l]),
              y_gqa,
              fourier_mix(fx, nb, n, fnet_w[l], seq_tables, chan_tables),
              neighbourhood_attention(nq, nk, nv, nkc, nvc, nb, n, m, na_rpb[l]))
        xl = out_projection(xl, mod3, 5, lat_row(512), ys, 0, w_out_b)

        if not last:
            ycs = (pool_mix(pc, m, pool_w_b[l], pool_scale[l]),
                   context_attention(gqc, nb, m, GQA_KV_HEADS, gkc, gvc, m),
                   fourier_mix(fc, nb, m, fnet_w[l], ctx_tables, chan_tables),
                   context_attention(nqc, nb, m, NA_HEADS, nkc, nvc, m))
            xc = out_projection(xc, mod3, 5, ctx_row, ycs, 0, w_out_b)
            xc = ffn(xc, mod3, 6, ctx_row, ffn2_norm[l], 0, *f2, tm=FFN_TM)

        xl = ffn(xl, mod3, 6, lat_row(FFN_TM), ffn2_norm[l], 0, *f2, final_g=final_norm if last else None, tm=FFN_TM)
    return xl.reshape(nb, n, d)
```
